```python
import math
import jax, jax.numpy as jnp
from jax import lax
import numpy as np

D_MODEL = 2048
BATCH = 8
SEQ = 2048
DEPTH = 1

CHUNK = 64
N_META = 16
Q_BLOCK = 128
EPS = 1e-6

D_SSM = D_MODEL // 2
SSM_GROUP = 16
N_SSM_GROUPS = D_SSM // SSM_GROUP
SSM_STATE = 64
DT_MIN = 1e-3
DT_MAX = 1e-1

MLA_HEADS = 8
QK_NOPE = 128
QK_ROPE = 64
V_HEAD = 128
Q_LORA = 512
KV_LORA = 256
D_ATTN = MLA_HEADS * V_HEAD
ROPE_BASE = 10000.0

D_MIX = D_SSM + D_ATTN
D_IN = D_SSM + Q_LORA + KV_LORA + QK_ROPE

D_FF = 5504
CONV_W = 3

kernel_name = "hybrid_s5_mla_convffn_block"


def rmsnorm(x, g):
    xf = x.astype(jnp.float32)
    y = xf * lax.rsqrt(jnp.mean(xf * xf, axis=-1, keepdims=True) + EPS)
    return (y * g.astype(jnp.float32)).astype(x.dtype)


def rotary(x, cos, sin):
    x1, x2 = jnp.split(x, 2, axis=-1)
    return jnp.concatenate([x1 * cos - x2 * sin, x2 * cos + x1 * sin], axis=-1)


def s5_mixer(u, lam_re, lam_im, log_dt, b_re, b_im, c_re, c_im, d_skip, w_glu, b_glu):
    bsz, L, _ = u.shape
    f32 = jnp.float32
    uf = u.astype(f32).reshape(bsz, L, N_SSM_GROUPS, SSM_GROUP)
    lam = lax.complex(lam_re.astype(f32), lam_im.astype(f32))
    dt = jnp.exp(log_dt.astype(f32))[:, None]
    lam_bar = jnp.exp(lam * dt)
    b = lax.complex(b_re.astype(f32), b_im.astype(f32))
    b_bar = ((lam_bar - 1.0) / lam)[..., None] * b
    bu = jnp.einsum('blgc,gpc->blgp', uf.astype(jnp.complex64), b_bar)
    a = jnp.broadcast_to(lam_bar, bu.shape)

    def combine(e1, e2):
        a1, s1 = e1
        a2, s2 = e2
        return a1 * a2, a2 * s1 + s2

    _, h = lax.associative_scan(combine, (a, bu), axis=1)
    c = lax.complex(c_re.astype(f32), c_im.astype(f32))
    y = jnp.real(jnp.einsum('blgp,gcp->blgc', h, c))
    y = y + d_skip.astype(f32).reshape(N_SSM_GROUPS, SSM_GROUP) * uf
    y = y.reshape(bsz, L, D_SSM)
    g = jax.nn.gelu(y)
    out = g * jax.nn.sigmoid(g @ w_glu.astype(f32) + b_glu.astype(f32))
    return out.astype(u.dtype)


def mla_mixer(q_a, kv_a, k_pe, q_a_norm, w_q_b, kv_a_norm, w_kv_b, cos, sin, chunk_id):
    bsz, L, _ = q_a.shape
    q = (rmsnorm(q_a, q_a_norm) @ w_q_b).reshape(bsz, L, MLA_HEADS, QK_NOPE + QK_ROPE)
    q_nope, q_pe = q[..., :QK_NOPE], q[..., QK_NOPE:]
    q_pe = rotary(q_pe, cos[:, None, :], sin[:, None, :])
    kv = (rmsnorm(kv_a, kv_a_norm) @ w_kv_b).reshape(bsz, L, MLA_HEADS, QK_NOPE + V_HEAD)
    k_nope, v = kv[..., :QK_NOPE], kv[..., QK_NOPE:]
    k_pe = rotary(k_pe, cos, sin)
    scale = 1.0 / math.sqrt(QK_NOPE + QK_ROPE)

    n_blk = -(-L // Q_BLOCK)
    pad = n_blk * Q_BLOCK - L

    def to_blocks(t):
        t = jnp.pad(t, ((0, 0), (0, pad)) + ((0, 0),) * (t.ndim - 2))
        return jnp.moveaxis(t.reshape(bsz, n_blk, Q_BLOCK, *t.shape[2:]), 1, 0)

    q_cid = jnp.pad(chunk_id, (0, pad), constant_values=2 ** 30).reshape(n_blk, Q_BLOCK)

    def attend(args):
        qn, qp, qc = args
        s = jnp.einsum('bqhd,bkhd->bhqk', qn, k_nope, preferred_element_type=jnp.float32)
        s = s + jnp.einsum('bqhr,bkr->bhqk', qp, k_pe, preferred_element_type=jnp.float32)
        mask = chunk_id[None, :] <= qc[:, None]
        s = jnp.where(mask[None, None], s * scale, jnp.finfo(jnp.float32).min)
        p = jax.nn.softmax(s, axis=-1).astype(v.dtype)
        return jnp.einsum('bhqk,bkhd->bqhd', p, v)

    o = lax.map(attend, (to_blocks(q_nope), to_blocks(q_pe), q_cid))
    o = jnp.moveaxis(o, 0, 1).reshape(bsz, n_blk * Q_BLOCK, D_ATTN)[:, :L]
    return o


def conv_ffn(x, w_up, conv_w, conv_b, w_down):
    L = x.shape[1]
    gate, val = jnp.split(x @ w_up, 2, axis=-1)
    gp = jnp.pad(gate, ((0, 0), (CONV_W - 1, 0), (0, 0)))
    gate = sum(conv_w[k] * gp[:, k:k + L] for k in range(CONV_W)) + conv_b
    return (jax.nn.silu(gate) * val) @ w_down


def setup_inputs(seed: int = 0) -> dict:
    key = jax.random.key(seed)
    ks = jax.random.split(key, 32)
    f32 = jnp.float32
    nrm = lambda k, shape, s: jax.random.normal(k, shape, f32) * s
    gain = lambda k, shape: 1.0 + 0.01 * jax.random.normal(k, shape, f32)
    G, P, C = N_SSM_GROUPS, SSM_STATE, SSM_GROUP
    lam_re = -0.5 + 0.01 * jax.random.normal(ks[4], (DEPTH, G, P), f32)
    lam_im = jnp.pi * jnp.arange(P, dtype=f32)[None, None, :] + 0.01 * jax.random.normal(ks[5], (DEPTH, G, P), f32)
    log_dt = jax.random.uniform(ks[6], (DEPTH, G), f32, math.log(DT_MIN), math.log(DT_MAX))
    return {
        "x": jax.random.normal(ks[0], (BATCH, SEQ, D_MODEL), f32),
        "meta_tokens": nrm(ks[1], (N_META, D_MODEL), 1.0),
        "mix_norm": gain(ks[2], (DEPTH, D_MODEL)),
        "w_in": nrm(ks[3], (DEPTH, D_MODEL, D_IN), D_MODEL ** -0.5),
        "lam_re": lam_re,
        "lam_im": lam_im,
        "log_dt": log_dt,
        "b_re": nrm(ks[7], (DEPTH, G, P, C), (2 * C) ** -0.5),
        "b_im": nrm(ks[8], (DEPTH, G, P, C), (2 * C) ** -0.5),
        "c_re": nrm(ks[9], (DEPTH, G, C, P), (2 * P) ** -0.5),
        "c_im": nrm(ks[10], (DEPTH, G, C, P), (2 * P) ** -0.5),
        "d_skip": nrm(ks[11], (DEPTH, D_SSM), 1.0),
        "w_glu": nrm(ks[12], (DEPTH, D_SSM, D_SSM), D_SSM ** -0.5),
        "b_glu": nrm(ks[13], (DEPTH, D_SSM), 0.01),
        "q_a_norm": gain(ks[14], (DEPTH, Q_LORA)),
        "w_q_b": nrm(ks[15], (DEPTH, Q_LORA, MLA_HEADS * (QK_NOPE + QK_ROPE)), Q_LORA ** -0.5),
        "kv_a_norm": gain(ks[16], (DEPTH, KV_LORA)),
        "w_kv_b": nrm(ks[17], (DEPTH, KV_LORA, MLA_HEADS * (QK_NOPE + V_HEAD)), KV_LORA ** -0.5),
        "out_norm_ssm": gain(ks[18], (DEPTH, D_SSM)),
        "out_norm_attn": gain(ks[19], (DEPTH, D_ATTN)),
        "w_out": nrm(ks[20], (DEPTH, D_MIX, D_MODEL), D_MIX ** -0.5),
        "ffn_norm": gain(ks[21], (DEPTH, D_MODEL)),
        "w_up": nrm(ks[22], (DEPTH, D_MODEL, 2 * D_FF), D_MODEL ** -0.5),
        "conv_w": nrm(ks[23], (DEPTH, CONV_W, D_FF), CONV_W ** -0.5),
        "conv_b": nrm(ks[24], (DEPTH, D_FF), 0.01),
        "w_down": nrm(ks[25], (DEPTH, D_FF, D_MODEL), D_FF ** -0.5),
        "final_norm": gain(ks[26], (D_MODEL,)),
    }


def reference(x, meta_tokens, mix_norm, w_in, lam_re, lam_im, log_dt, b_re, b_im, c_re, c_im,
              d_skip, w_glu, b_glu, q_a_norm, w_q_b, kv_a_norm, w_kv_b, out_norm_ssm,
              out_norm_attn, w_out, ffn_norm, w_up, conv_w, conv_b, w_down, final_norm):
    bsz = x.shape[0]
    meta = jnp.broadcast_to(meta_tokens.astype(x.dtype)[None], (bsz, N_META, D_MODEL))
    h = jnp.concatenate([meta, x], axis=1)
    L = h.shape[1]

    pos = jnp.arange(L, dtype=jnp.int32)
    chunk_id = jnp.where(pos < N_META, 0, 1 + (pos - N_META) // CHUNK)
    inv_freq = 1.0 / (ROPE_BASE ** (jnp.arange(0, QK_ROPE, 2, dtype=jnp.float32) / QK_ROPE))
    ang = pos.astype(jnp.float32)[:, None] * inv_freq[None, :]
    cos = jnp.cos(ang).astype(x.dtype)
    sin = jnp.sin(ang).astype(x.dtype)

    for i in range(DEPTH):
        xn = rmsnorm(h, mix_norm[i])
        z = xn @ w_in[i]
        o1 = D_SSM
        o2 = o1 + Q_LORA
        o3 = o2 + KV_LORA
        u, q_a, kv_a, k_pe = z[..., :o1], z[..., o1:o2], z[..., o2:o3], z[..., o3:]
        ya = s5_mixer(u, lam_re[i], lam_im[i], log_dt[i], b_re[i], b_im[i], c_re[i], c_im[i],
                      d_skip[i], w_glu[i], b_glu[i])
        yb = mla_mixer(q_a, kv_a, k_pe, q_a_norm[i], w_q_b[i], kv_a_norm[i], w_kv_b[i],
                       cos, sin, chunk_id)
        y = jnp.concatenate([rmsnorm(ya, out_norm_ssm[i]), rmsnorm(yb, out_norm_attn[i])], axis=-1)
        h = h + y @ w_out[i]
        h = h + conv_ffn(rmsnorm(h, ffn_norm[i]), w_up[i], conv_w[i], conv_b[i], w_down[i])

    return rmsnorm(h, final_norm)[:, N_META:]
```

```python
import functools
import math

import jax
import jax.numpy as jnp
from jax import lax
from jax.experimental import pallas as pl
from jax.experimental.pallas import tpu as pltpu

F32 = jnp.float32
BF16 = jnp.bfloat16

EPS = 1e-6
CHUNK = 64
N_META = 16
SSM_GROUP = 16
SSM_STATE = 64
HEADS = 8
QK_NOPE = 128
QK_ROPE = 64
V_HEAD = 128
ROPE_BASE = 10000.0
CONV_W = 3

LANES = 128
SUBLANES = 8
MXU_DIM = 256
VMEM_BYTES = 64 * 1024 * 1024

HEAD_W = QK_NOPE + 2 * QK_ROPE
GROUPS_PER_MXU = MXU_DIM // SSM_GROUP
STATE_W = GROUPS_PER_MXU * SSM_STATE

ATT_TQ = 256
ATT_TK = 256
FFN_TF = 512


def _vmem_limit(nbytes):
    return int(min(nbytes + (8 << 20), VMEM_BYTES - (4 << 20)))


def _rms(x, g):
    x = x.astype(F32)
    return x * lax.rsqrt(jnp.mean(x * x, axis=-1, keepdims=True) + EPS) * g


def _const_spec(shape):
    nd = len(shape)
    return pl.BlockSpec(shape, lambda *_: (0,) * nd, pipeline_mode=pl.Buffered(1))


def _inproj_kernel(x_ref, g_ref, win_ref, qg_ref, wq_ref, kvg_ref, wk_ref, wv_ref, cs_ref,
                   u_ref, q_ref, k_ref, v_ref, *, scale):
    d_ssm = u_ref.shape[1]
    q_lora = wq_ref.shape[0]
    kv_lora = wk_ref.shape[0]
    xn = _rms(x_ref[...], g_ref[...]).astype(BF16)
    z = jnp.dot(xn, win_ref[...], preferred_element_type=F32)
    o1, o2, o3 = d_ssm, d_ssm + q_lora, d_ssm + q_lora + kv_lora
    u_ref[...] = z[:, :o1].astype(u_ref.dtype)

    cs = cs_ref[...]
    qn = _rms(z[:, o1:o2], qg_ref[...]).astype(BF16)
    q = jnp.dot(qn, wq_ref[...], preferred_element_type=F32)
    cs_q = cs * scale
    for h in range(HEADS):
        c0 = h * HEAD_W
        q_ref[:, c0:c0 + QK_NOPE] = (q[:, c0:c0 + QK_NOPE] * scale).astype(q_ref.dtype)
        q_ref[:, c0 + QK_NOPE:c0 + HEAD_W] = (q[:, c0 + QK_NOPE:c0 + HEAD_W] * cs_q).astype(q_ref.dtype)

    kvn = _rms(z[:, o2:o3], kvg_ref[...]).astype(BF16)
    kn = jnp.dot(kvn, wk_ref[...], preferred_element_type=F32)
    v_ref[...] = jnp.dot(kvn, wv_ref[...], preferred_element_type=F32).astype(v_ref.dtype)
    t = z[:, o3:o3 + 2 * QK_ROPE] * cs
    krot = (t + pltpu.roll(t, QK_ROPE, 1)).astype(k_ref.dtype)
    for h in range(HEADS):
        c0 = h * HEAD_W
        k_ref[:, c0:c0 + QK_NOPE] = kn[:, h * QK_NOPE:(h + 1) * QK_NOPE].astype(k_ref.dtype)
        k_ref[:, c0 + QK_NOPE:c0 + HEAD_W] = krot


def _inproj(x2d, cs, w, nb, nt, tm):
    rows, d = x2d.shape
    d_ssm = w["d_ssm"]
    n_in = w["win"].shape[1]
    row_spec = lambda width: pl.BlockSpec((tm, width), lambda b, j: (b * nt + j, 0))
    in_specs = [
        row_spec(d),
        _const_spec((1, d)),
        _const_spec(w["win"].shape),
        _const_spec(w["qg"].shape),
        _const_spec(w["wq"].shape),
        _const_spec(w["kvg"].shape),
        _const_spec(w["wk"].shape),
        _const_spec(w["wv"].shape),
        pl.BlockSpec((tm, LANES), lambda b, j: (j, 0)),
    ]
    out_shape = (
        jax.ShapeDtypeStruct((nt * tm, nb * d_ssm), BF16),
        jax.ShapeDtypeStruct((rows, HEADS * HEAD_W), BF16),
        jax.ShapeDtypeStruct((rows, HEADS * HEAD_W), BF16),
        jax.ShapeDtypeStruct((rows, HEADS * V_HEAD), BF16),
    )
    out_specs = (
        pl.BlockSpec((tm, d_ssm), lambda b, j: (j, b)),
        row_spec(HEADS * HEAD_W),
        row_spec(HEADS * HEAD_W),
        row_spec(HEADS * V_HEAD),
    )
    weights = 2 * (w["win"].size + w["wq"].size + w["wk"].size + w["wv"].size)
    tiles = 2 * tm * (4 * d + 2 * d_ssm + 4 * HEADS * HEAD_W + 2 * HEADS * V_HEAD + 4 * LANES)
    temps = 4 * tm * (d + n_in + 2 * HEADS * HEAD_W + 2 * HEADS * V_HEAD)
    return pl.pallas_call(
        functools.partial(_inproj_kernel, scale=w["scale"]),
        grid=(nb, nt),
        in_specs=in_specs,
        out_specs=out_specs,
        out_shape=out_shape,
        compiler_params=pltpu.CompilerParams(
            dimension_semantics=("arbitrary", "arbitrary"),
            vmem_limit_bytes=_vmem_limit(weights + tiles + temps)),
        name="inproj",
    )(x2d, w["mix_g"], w["win"], w["qg"], w["wq"], w["kvg"], w["wk"], w["wv"], cs)


def _s5_kernel(u_ref, h0_ref, bm_ref, cm_ref, are_ref, aim_ref, dskip_ref, wglu_ref, bglu_ref,
               y_ref, hT_ref, hbuf, state, *, steps):
    n_gb = bm_ref.shape[0]
    sw = are_ref.shape[2]
    cin = bm_ref.shape[1]

    @pl.when(pl.program_id(0) == 0)
    def _():
        state[...] = h0_ref[...]

    u = u_ref[...]
    for gb in range(n_gb):
        hbuf[:, 2 * sw * gb:2 * sw * (gb + 1)] = jnp.dot(
            u[:, cin * gb:cin * (gb + 1)], bm_ref[gb], preferred_element_type=F32)

    for gb in range(n_gb):
        are = are_ref[gb]
        aim = aim_ref[gb]
        c_re = 2 * sw * gb
        c_im = c_re + sw

        def tstep(t, carry, c_re=c_re, c_im=c_im, are=are, aim=aim):
            hre, him = carry
            r0 = pl.multiple_of(t * SUBLANES, SUBLANES)
            nre = are * hre - aim * him + hbuf[pl.ds(r0, SUBLANES), c_re:c_re + sw]
            nim = are * him + aim * hre + hbuf[pl.ds(r0, SUBLANES), c_im:c_im + sw]
            hbuf[pl.ds(r0, SUBLANES), c_re:c_re + sw] = nre
            hbuf[pl.ds(r0, SUBLANES), c_im:c_im + sw] = nim
            return nre, nim

        hre, him = lax.fori_loop(0, steps, tstep,
                                 (state[:, c_re:c_re + sw], state[:, c_im:c_im + sw]),
                                 unroll=min(steps, 8))
        state[:, c_re:c_re + sw] = hre
        state[:, c_im:c_im + sw] = him

    ys = [jnp.dot(hbuf[:, 2 * sw * gb:2 * sw * (gb + 1)].astype(BF16), cm_ref[gb],
                  preferred_element_type=F32) for gb in range(n_gb)]
    y = jnp.concatenate(ys, axis=1) + dskip_ref[...] * u.astype(F32)
    g = jax.nn.gelu(y)
    gate = jnp.dot(g.astype(BF16), wglu_ref[...], preferred_element_type=F32) + bglu_ref[...]
    y_ref[...] = (g * jax.nn.sigmoid(gate)).astype(y_ref.dtype)
    hT_ref[...] = state[...]


def _s5(u_rows, h0, w, steps):
    rows, d_ssm = u_rows.shape
    r = steps * SUBLANES
    n_state = h0.shape[1]
    in_specs = [
        pl.BlockSpec((r, d_ssm), lambda i: (i, 0)),
        _const_spec(h0.shape),
        _const_spec(w["bm"].shape),
        _const_spec(w["cm"].shape),
        _const_spec(w["are"].shape),
        _const_spec(w["aim"].shape),
        _const_spec((1, d_ssm)),
        _const_spec(w["wglu"].shape),
        _const_spec((1, d_ssm)),
    ]
    out_shape = (jax.ShapeDtypeStruct((rows, d_ssm), BF16),
                 jax.ShapeDtypeStruct(h0.shape, F32))
    out_specs = (pl.BlockSpec((r, d_ssm), lambda i: (i, 0)),
                 pl.BlockSpec(h0.shape, lambda i: (0, 0)))
    weights = 2 * (w["bm"].size + w["cm"].size + w["wglu"].size) + 4 * (w["are"].size + w["aim"].size)
    tiles = 2 * r * d_ssm * 2 * 2 + 3 * 4 * h0.size
    scratch = 4 * r * n_state + 4 * h0.size
    temps = 2 * r * n_state + 6 * 4 * r * d_ssm
    return pl.pallas_call(
        functools.partial(_s5_kernel, steps=steps),
        grid=(rows // r,),
        in_specs=in_specs,
        out_specs=out_specs,
        out_shape=out_shape,
        scratch_shapes=[pltpu.VMEM((r, n_state), F32), pltpu.VMEM(h0.shape, F32)],
        compiler_params=pltpu.CompilerParams(
            dimension_semantics=("arbitrary",),
            vmem_limit_bytes=_vmem_limit(weights + tiles + scratch + temps)),
        name="s5",
    )(u_rows, h0, w["bm"], w["cm"], w["are"], w["aim"], w["dskip"], w["wglu"], w["bglu"])


_NT = (((1,), (1,)), ((), ()))


def _attn_update(h, s, v, m_scr, l_scr, acc_scr, first):
    n_rep = s.shape[1] // LANES
    rowmax = jnp.max(s, axis=1, keepdims=True)
    if first:
        m_new = jnp.broadcast_to(rowmax, (s.shape[0], LANES))
    else:
        m_old = m_scr[h]
        m_new = jnp.maximum(m_old, rowmax)
        alpha = jnp.exp(m_old - m_new)
    m_wide = jnp.concatenate([m_new] * n_rep, axis=1) if n_rep > 1 else m_new[:, :s.shape[1]]
    p = jnp.exp(s - m_wide)
    psum = jnp.sum(p, axis=1, keepdims=True)
    pv = jnp.dot(p.astype(v.dtype), v, preferred_element_type=F32)
    if first:
        l_scr[h] = jnp.broadcast_to(psum, m_new.shape)
        acc_scr[h] = pv
    else:
        l_scr[h] = alpha * l_scr[h] + psum
        acc_scr[h] = alpha * acc_scr[h] + pv
    m_scr[h] = m_new


def _attn_kernel(q_ref, k_ref, v_ref, km_ref, vm_ref, o_ref, m_scr, l_scr, acc_scr):
    i = pl.program_id(1)
    tq = q_ref.shape[0]

    def qh(h):
        return q_ref[:, h * HEAD_W:(h + 1) * HEAD_W]

    for h in range(HEADS):
        s = lax.dot_general(qh(h), km_ref[:, h * HEAD_W:(h + 1) * HEAD_W], _NT,
                            preferred_element_type=F32)
        _attn_update(h, s, vm_ref[:, h * V_HEAD:(h + 1) * V_HEAD], m_scr, l_scr, acc_scr, True)

    def full_block(j, carry):
        r0 = pl.multiple_of(j * ATT_TK, ATT_TK)
        for h in range(HEADS):
            s = lax.dot_general(qh(h), k_ref[pl.ds(r0, ATT_TK), h * HEAD_W:(h + 1) * HEAD_W], _NT,
                                preferred_element_type=F32)
            _attn_update(h, s, v_ref[pl.ds(r0, ATT_TK), h * V_HEAD:(h + 1) * V_HEAD],
                         m_scr, l_scr, acc_scr, False)
        return carry

    lax.fori_loop(0, i, full_block, 0)

    r0 = pl.multiple_of(i * ATT_TK, ATT_TK)
    qc = lax.broadcasted_iota(jnp.int32, (tq, ATT_TK), 0) // CHUNK
    kc = lax.broadcasted_iota(jnp.int32, (tq, ATT_TK), 1) // CHUNK
    visible = kc <= qc
    for h in range(HEADS):
        s = lax.dot_general(qh(h), k_ref[pl.ds(r0, ATT_TK), h * HEAD_W:(h + 1) * HEAD_W], _NT,
                            preferred_element_type=F32)
        s = jnp.where(visible, s, jnp.finfo(F32).min)
        _attn_update(h, s, v_ref[pl.ds(r0, ATT_TK), h * V_HEAD:(h + 1) * V_HEAD],
                     m_scr, l_scr, acc_scr, False)

    for h in range(HEADS):
        o_ref[:, h * V_HEAD:(h + 1) * V_HEAD] = (acc_scr[h] / l_scr[h]).astype(o_ref.dtype)


def _attention(q, k, v, km, vm, nb, seq):
    assert ATT_TQ == ATT_TK and ATT_TQ % CHUNK == 0 and seq % ATT_TQ == 0
    nq = seq // ATT_TQ
    wq, wv = q.shape[1], v.shape[1]
    in_specs = [
        pl.BlockSpec((ATT_TQ, wq), lambda b, i: (b * nq + i, 0)),
        pl.BlockSpec((seq, wq), lambda b, i: (b, 0)),
        pl.BlockSpec((seq, wv), lambda b, i: (b, 0)),
        _const_spec(km.shape),
        _const_spec(vm.shape),
    ]
    stat = pltpu.VMEM((HEADS, ATT_TQ, LANES), F32)
    tiles = 2 * 2 * (ATT_TQ * wq + seq * wq + seq * wv + ATT_TQ * wv) + 2 * (km.size + vm.size)
    scratch = 3 * 4 * HEADS * ATT_TQ * LANES
    temps = 16 * 4 * ATT_TQ * ATT_TK
    return pl.pallas_call(
        _attn_kernel,
        grid=(nb, nq),
        in_specs=in_specs,
        out_specs=pl.BlockSpec((ATT_TQ, wv), lambda b, i: (b * nq + i, 0)),
        out_shape=jax.ShapeDtypeStruct((nb * seq, wv), BF16),
        scratch_shapes=[stat, stat, stat],
        compiler_params=pltpu.CompilerParams(
            dimension_semantics=("arbitrary", "arbitrary"),
            vmem_limit_bytes=_vmem_limit(tiles + scratch + temps)),
        name="attn",
    )(q, k, v, km, vm)


def _attn_meta_kernel(q_ref, k_ref, v_ref, o_ref):
    for h in range(HEADS):
        s = lax.dot_general(q_ref[:, h * HEAD_W:(h + 1) * HEAD_W],
                            k_ref[:, h * HEAD_W:(h + 1) * HEAD_W], _NT, preferred_element_type=F32)
        p = jnp.exp(s - jnp.max(s, axis=1, keepdims=True))
        pv = jnp.dot(p.astype(v_ref.dtype), v_ref[:, h * V_HEAD:(h + 1) * V_HEAD],
                     preferred_element_type=F32)
        o_ref[:, h * V_HEAD:(h + 1) * V_HEAD] = (pv / jnp.sum(p, axis=1, keepdims=True)).astype(o_ref.dtype)


def _attention_meta(q, k, v):
    return pl.pallas_call(
        _attn_meta_kernel,
        out_shape=jax.ShapeDtypeStruct(v.shape, BF16),
        name="attn_meta",
    )(q, k, v)


def _outproj_kernel(ya_ref, yb_ref, x_ref, ga_ref, gb_ref, wa_ref, wb_ref, o_ref):
    a = _rms(ya_ref[...], ga_ref[...]).astype(BF16)
    b = _rms(yb_ref[...], gb_ref[...]).astype(BF16)
    mix = jnp.dot(a, wa_ref[...], preferred_element_type=F32)
    mix = mix + jnp.dot(b, wb_ref[...], preferred_element_type=F32)
    o_ref[...] = x_ref[...].astype(F32) + mix


def _outproj(ya_t, yb, x2d, w, nb, nt, tm):
    rows, d = x2d.shape
    d_ssm, d_attn = w["wo_a"].shape[0], w["wo_b"].shape[0]
    in_specs = [
        pl.BlockSpec((tm, d_ssm), lambda b, j: (j, b)),
        pl.BlockSpec((tm, d_attn), lambda b, j: (b * nt + j, 0)),
        pl.BlockSpec((tm, d), lambda b, j: (b * nt + j, 0)),
        _const_spec((1, d_ssm)),
        _const_spec((1, d_attn)),
        _const_spec(w["wo_a"].shape),
        _const_spec(w["wo_b"].shape),
    ]
    weights = 2 * (w["wo_a"].size + w["wo_b"].size)
    tiles = 2 * tm * (2 * d_ssm + 2 * d_attn + 4 * d + 4 * d)
    temps = 4 * tm * (2 * d_ssm + 2 * d_attn + 2 * d)
    return pl.pallas_call(
        _outproj_kernel,
        grid=(nb, nt),
        in_specs=in_specs,
        out_specs=pl.BlockSpec((tm, d), lambda b, j: (b * nt + j, 0)),
        out_shape=jax.ShapeDtypeStruct((rows, d), F32),
        compiler_params=pltpu.CompilerParams(
            dimension_semantics=("arbitrary", "arbitrary"),
            vmem_limit_bytes=_vmem_limit(weights + tiles + temps)),
        name="outproj",
    )(ya_t, yb, x2d, w["ga"], w["gb"], w["wo_a"], w["wo_b"])


def _gate_kernel(h_ref, g_ref, wup_ref, o_ref):
    tf = o_ref.shape[1]
    xn = _rms(h_ref[...], g_ref[...]).astype(BF16)
    o_ref[...] = jnp.dot(xn, wup_ref[:, :tf], preferred_element_type=F32)


def _meta_gate(h_meta, w):
    rows, d = h_meta.shape
    n_ff = w["wdn"].shape[0]
    return pl.pallas_call(
        _gate_kernel,
        grid=(n_ff // FFN_TF,),
        in_specs=[_const_spec((rows, d)), _const_spec((1, d)),
                  pl.BlockSpec((d, 2 * FFN_TF), lambda j: (0, j))],
        out_specs=pl.BlockSpec((rows, FFN_TF), lambda j: (0, j)),
        out_shape=jax.ShapeDtypeStruct((rows, n_ff), F32),
        compiler_params=pltpu.CompilerParams(dimension_semantics=("arbitrary",)),
        name="meta_gate",
    )(h_meta, w["ffn_g"], w["wup"])


def _ffn_kernel(h_ref, g_ref, wup_ref, cw_ref, cb_ref, wdn_ref, mh_ref, fg_ref, o_ref,
                xn_scr, halo_scr, *, tiles_per_batch):
    i = pl.program_id(0)
    j = pl.program_id(1)
    tm = h_ref.shape[0]
    tf = wdn_ref.shape[0]

    @pl.when(j == 0)
    def _():
        xn_scr[...] = _rms(h_ref[...], g_ref[...]).astype(BF16)

    up = jnp.dot(xn_scr[...], wup_ref[...], preferred_element_type=F32)
    gate, val = up[:, :tf], up[:, tf:]
    @pl.when((i % tiles_per_batch) == 0)
    def _():
        halo_scr[j] = mh_ref[...]

    prev = halo_scr[j]
    halo_scr[j] = gate[tm - SUBLANES:, :]
    ext = jnp.concatenate([prev, gate], axis=0)
    cw = cw_ref[...]
    conv = (cw[0:1] * ext[SUBLANES - 2:SUBLANES - 2 + tm]
            + cw[1:2] * ext[SUBLANES - 1:SUBLANES - 1 + tm]
            + cw[2:3] * gate + cb_ref[...])
    act = (jax.nn.silu(conv) * val).astype(BF16)
    contrib = jnp.dot(act, wdn_ref[...], preferred_element_type=F32)

    @pl.when(j == 0)
    def _():
        o_ref[...] = contrib

    @pl.when(j > 0)
    def _():
        o_ref[...] += contrib

    @pl.when(j == pl.num_programs(1) - 1)
    def _():
        o_ref[...] = _rms(h_ref[...] + o_ref[...], fg_ref[...])


def _ffn(h1, meta_halo, w, tiles_per_batch, tm):
    rows, d = h1.shape
    n_ff = w["wdn"].shape[0]
    nj = n_ff // FFN_TF
    in_specs = [
        pl.BlockSpec((tm, d), lambda i, j: (i, 0)),
        _const_spec((1, d)),
        pl.BlockSpec((d, 2 * FFN_TF), lambda i, j: (0, j)),
        pl.BlockSpec((CONV_W, FFN_TF), lambda i, j: (0, j)),
        pl.BlockSpec((1, FFN_TF), lambda i, j: (0, j)),
        pl.BlockSpec((FFN_TF, d), lambda i, j: (j, 0)),
        pl.BlockSpec((SUBLANES, FFN_TF), lambda i, j: (0, j)),
        _const_spec((1, d)),
    ]
    tiles = 2 * (2 * 4 * tm * d + 2 * d * 2 * FFN_TF + 2 * FFN_TF * d)
    scratch = 2 * tm * d + 4 * nj * SUBLANES * FFN_TF
    temps = 4 * tm * FFN_TF * 8 + 4 * tm * d
    return pl.pallas_call(
        functools.partial(_ffn_kernel, tiles_per_batch=tiles_per_batch),
        grid=(rows // tm, nj),
        in_specs=in_specs,
        out_specs=pl.BlockSpec((tm, d), lambda i, j: (i, 0)),
        out_shape=jax.ShapeDtypeStruct((rows, d), F32),
        scratch_shapes=[pltpu.VMEM((tm, d), BF16), pltpu.VMEM((nj, SUBLANES, FFN_TF), F32)],
        compiler_params=pltpu.CompilerParams(
            dimension_semantics=("arbitrary", "arbitrary"),
            vmem_limit_bytes=_vmem_limit(tiles + scratch + temps)),
        name="ffn",
    )(h1, w["ffn_g"], w["wup"], w["conv_w"], w["conv_b"], w["wdn"], meta_halo, w["final_g"])


def _swap_halves(w):
    half = w.shape[-1] // 2
    return jnp.concatenate([-w[..., half:], w[..., :half]], axis=-1)


def _block_diag(blocks):
    n, r, c = blocks.shape
    eye = jnp.eye(n, dtype=blocks.dtype)
    return (blocks[:, :, None, :] * eye[:, None, :, None]).reshape(n * r, n * c)


def _prepare(p):
    d_ssm = p["d_skip"].shape[-1]
    q_lora = p["q_a_norm"].shape[-1]
    kv_lora = p["kv_a_norm"].shape[-1]
    d_ff = p["w_down"].shape[0]
    w = {"d_ssm": d_ssm, "scale": 1.0 / math.sqrt(QK_NOPE + QK_ROPE)}

    row = lambda v: v.reshape(1, -1).astype(F32)
    w["mix_g"] = row(p["mix_norm"])
    w_in = p["w_in"]
    o3 = d_ssm + q_lora + kv_lora
    w["win"] = jnp.concatenate([w_in, _swap_halves(w_in[:, o3:])], axis=1).astype(BF16)
    w["qg"] = row(p["q_a_norm"])
    w["kvg"] = row(p["kv_a_norm"])
    wq = p["w_q_b"].reshape(q_lora, HEADS, QK_NOPE + QK_ROPE)
    wq_pe = wq[..., QK_NOPE:]
    w["wq"] = jnp.concatenate([wq, _swap_halves(wq_pe)], axis=-1).reshape(q_lora, HEADS * HEAD_W).astype(BF16)
    wkv = p["w_kv_b"].reshape(kv_lora, HEADS, QK_NOPE + V_HEAD)
    w["wk"] = wkv[..., :QK_NOPE].reshape(kv_lora, HEADS * QK_NOPE).astype(BF16)
    w["wv"] = wkv[..., QK_NOPE:].reshape(kv_lora, HEADS * V_HEAD).astype(BF16)

    lam_re, lam_im = p["lam_re"].astype(F32), p["lam_im"].astype(F32)
    dt = jnp.exp(p["log_dt"].astype(F32))[:, None]
    mag = jnp.exp(lam_re * dt)
    a_re, a_im = mag * jnp.cos(lam_im * dt), mag * jnp.sin(lam_im * dt)
    den = lam_re * lam_re + lam_im * lam_im
    f_re = ((a_re - 1.0) * lam_re + a_im * lam_im) / den
    f_im = (a_im * lam_re - (a_re - 1.0) * lam_im) / den
    b_re, b_im = p["b_re"].astype(F32), p["b_im"].astype(F32)
    bb_re = f_re[..., None] * b_re - f_im[..., None] * b_im
    bb_im = f_re[..., None] * b_im + f_im[..., None] * b_re
    n_g = lam_re.shape[0]
    n_gb = n_g // GROUPS_PER_MXU
    grp = lambda t: t.reshape(n_gb, GROUPS_PER_MXU, *t.shape[1:])
    bd = jax.vmap(_block_diag)
    w["bm"] = jnp.concatenate([bd(grp(jnp.swapaxes(bb_re, 1, 2))),
                               bd(grp(jnp.swapaxes(bb_im, 1, 2)))], axis=2).astype(BF16)
    c_re, c_im = p["c_re"].astype(F32), p["c_im"].astype(F32)
    w["cm"] = jnp.concatenate([bd(grp(jnp.swapaxes(c_re, 1, 2))),
                               bd(grp(jnp.swapaxes(-c_im, 1, 2)))], axis=1).astype(BF16)
    bcast = lambda a: jnp.broadcast_to(a.reshape(n_gb, 1, STATE_W), (n_gb, SUBLANES, STATE_W))
    w["are"], w["aim"] = bcast(a_re), bcast(a_im)
    w["dskip"] = row(p["d_skip"])
    w["wglu"] = p["w_glu"].astype(BF16)
    w["bglu"] = row(p["b_glu"])

    w["ga"], w["gb"] = row(p["out_norm_ssm"]), row(p["out_norm_attn"])
    w["wo_a"] = p["w_out"][:d_ssm].astype(BF16)
    w["wo_b"] = p["w_out"][d_ssm:].astype(BF16)

    n_ff = -(-d_ff // FFN_TF) * FFN_TF
    pad = n_ff - d_ff
    d = p["w_up"].shape[0]
    wg = jnp.pad(p["w_up"][:, :d_ff], ((0, 0), (0, pad))).reshape(d, n_ff // FFN_TF, FFN_TF)
    wv = jnp.pad(p["w_up"][:, d_ff:], ((0, 0), (0, pad))).reshape(d, n_ff // FFN_TF, FFN_TF)
    w["wup"] = jnp.concatenate([wg, wv], axis=2).reshape(d, 2 * n_ff).astype(BF16)
    w["wdn"] = jnp.pad(p["w_down"], ((0, pad), (0, 0))).astype(BF16)
    w["conv_w"] = jnp.pad(p["conv_w"].astype(F32), ((0, 0), (0, pad)))
    w["conv_b"] = jnp.pad(row(p["conv_b"]), ((0, 0), (0, pad)))
    w["ffn_g"] = row(p["ffn_norm"])
    w["final_g"] = row(p["final_norm"])
    return w


def _rope_table(n_pos):
    pos = jnp.arange(n_pos, dtype=F32)
    inv_freq = 1.0 / (ROPE_BASE ** (jnp.arange(0, QK_ROPE, 2, dtype=F32) / QK_ROPE))
    ang = pos[:, None] * inv_freq[None, :]
    cos, sin = jnp.cos(ang), jnp.sin(ang)
    return jnp.concatenate([cos, cos, sin, sin], axis=1)


def kernel(x, meta_tokens, mix_norm, w_in, lam_re, lam_im, log_dt, b_re, b_im, c_re, c_im, d_skip, w_glu, b_glu, q_a_norm, w_q_b, kv_a_norm, w_kv_b, out_norm_ssm, out_norm_attn, w_out, ffn_norm, w_up, conv_w, conv_b, w_down, final_norm):
    bsz, seq, d = x.shape
    assert meta_tokens.shape == (N_META, d) and bsz == SUBLANES
    p = dict(mix_norm=mix_norm[0], w_in=w_in[0], lam_re=lam_re[0], lam_im=lam_im[0], log_dt=log_dt[0],
             b_re=b_re[0], b_im=b_im[0], c_re=c_re[0], c_im=c_im[0], d_skip=d_skip[0],
             w_glu=w_glu[0], b_glu=b_glu[0], q_a_norm=q_a_norm[0], w_q_b=w_q_b[0],
             kv_a_norm=kv_a_norm[0], w_kv_b=w_kv_b[0], out_norm_ssm=out_norm_ssm[0],
             out_norm_attn=out_norm_attn[0], w_out=w_out[0], ffn_norm=ffn_norm[0], w_up=w_up[0],
             conv_w=conv_w[0], conv_b=conv_b[0], w_down=w_down[0], final_norm=final_norm)
    w = _prepare(p)
    d_ssm = w["d_ssm"]
    cs = _rope_table(N_META + seq)
    x2d = x.reshape(bsz * seq, d)
    meta = meta_tokens.astype(x.dtype)

    u_m, q_m, k_m, v_m = _inproj(meta, cs[:N_META], w, 1, 1, N_META)
    n_state = 2 * STATE_W * w["bm"].shape[0]
    ya_m8, h_meta_state = _s5(jnp.repeat(u_m, SUBLANES, axis=0), jnp.zeros((SUBLANES, n_state), F32),
                              w, N_META)
    ya_m = ya_m8[::SUBLANES]
    yb_m = _attention_meta(q_m, k_m, v_m)
    h1_m = _outproj(ya_m, yb_m, meta, w, 1, 1, N_META)
    meta_halo = _meta_gate(h1_m, w)[N_META - SUBLANES:]

    tm = 512
    nt = seq // tm
    u_t, q, k, v = _inproj(x2d, cs[N_META:], w, bsz, nt, tm)
    ya_rows, _ = _s5(u_t.reshape(seq * bsz, d_ssm), h_meta_state, w, 64)
    yb = _attention(q, k, v, k_m, v_m, bsz, seq)
    h1 = _outproj(ya_rows.reshape(seq, bsz * d_ssm), yb, x2d, w, bsz, nt, tm)
    out = _ffn(h1, meta_halo, w, nt, tm)
    return out.reshape(bsz, seq, d)
```

```python
import functools
import math

import jax
import jax.numpy as jnp
from jax import lax
from jax.experimental import pallas as pl
from jax.experimental.pallas import tpu as pltpu

F32 = jnp.float32
BF16 = jnp.bfloat16

EPS = 1e-6
CHUNK = 64
N_META = 16
SSM_GROUP = 16
SSM_STATE = 64
HEADS = 8
QK_NOPE = 128
QK_ROPE = 64
V_HEAD = 128
ROPE_BASE = 10000.0
CONV_W = 3

LANES = 128
SUBLANES = 8
MXU_DIM = 256
VMEM_BYTES = 64 * 1024 * 1024

HEAD_W = QK_NOPE + 2 * QK_ROPE
GROUPS_PER_MXU = MXU_DIM // SSM_GROUP
STATE_W = GROUPS_PER_MXU * SSM_STATE

ATT_TQ = 256
ATT_TK = 256
FFN_TF = 512
FFN_SUB = 2


def _vmem_limit(nbytes):
    return int(min(nbytes + (8 << 20), VMEM_BYTES - (4 << 20)))


def _rms(x, g):
    x = x.astype(F32)
    return x * lax.rsqrt(jnp.mean(x * x, axis=-1, keepdims=True) + EPS) * g


def _const_spec(shape):
    nd = len(shape)
    return pl.BlockSpec(shape, lambda *_: (0,) * nd, pipeline_mode=pl.Buffered(1))


def _inproj_kernel(x_ref, g_ref, win_ref, qg_ref, wq_ref, kvg_ref, wk_ref, wv_ref, cs_ref,
                   u_ref, q_ref, k_ref, v_ref, *, scale):
    d_ssm = u_ref.shape[1]
    q_lora = wq_ref.shape[0]
    kv_lora = wk_ref.shape[0]
    xn = _rms(x_ref[...], g_ref[...]).astype(BF16)
    z = jnp.dot(xn, win_ref[...], preferred_element_type=F32)
    o1, o2, o3 = d_ssm, d_ssm + q_lora, d_ssm + q_lora + kv_lora
    u_ref[...] = z[:, :o1].astype(u_ref.dtype)

    cs = cs_ref[...]
    qn = _rms(z[:, o1:o2], qg_ref[...]).astype(BF16)
    q = jnp.dot(qn, wq_ref[...], preferred_element_type=F32)
    cs_q = cs * scale
    for h in range(HEADS):
        c0 = h * HEAD_W
        q_ref[:, c0:c0 + QK_NOPE] = (q[:, c0:c0 + QK_NOPE] * scale).astype(q_ref.dtype)
        q_ref[:, c0 + QK_NOPE:c0 + HEAD_W] = (q[:, c0 + QK_NOPE:c0 + HEAD_W] * cs_q).astype(q_ref.dtype)

    kvn = _rms(z[:, o2:o3], kvg_ref[...]).astype(BF16)
    kn = jnp.dot(kvn, wk_ref[...], preferred_element_type=F32)
    v_ref[...] = jnp.dot(kvn, wv_ref[...], preferred_element_type=F32).astype(v_ref.dtype)
    t = z[:, o3:o3 + 2 * QK_ROPE] * cs
    krot = (t + pltpu.roll(t, QK_ROPE, 1)).astype(k_ref.dtype)
    for h in range(HEADS):
        c0 = h * HEAD_W
        k_ref[:, c0:c0 + QK_NOPE] = kn[:, h * QK_NOPE:(h + 1) * QK_NOPE].astype(k_ref.dtype)
        k_ref[:, c0 + QK_NOPE:c0 + HEAD_W] = krot


def _inproj(x2d, cs, w, nb, nt, tm):
    rows, d = x2d.shape
    d_ssm = w["d_ssm"]
    n_in = w["win"].shape[1]
    row_spec = lambda width: pl.BlockSpec((tm, width), lambda b, j: (b * nt + j, 0))
    in_specs = [
        row_spec(d),
        _const_spec((1, d)),
        _const_spec(w["win"].shape),
        _const_spec(w["qg"].shape),
        _const_spec(w["wq"].shape),
        _const_spec(w["kvg"].shape),
        _const_spec(w["wk"].shape),
        _const_spec(w["wv"].shape),
        pl.BlockSpec((tm, LANES), lambda b, j: (j, 0)),
    ]
    out_shape = (
        jax.ShapeDtypeStruct((rows, d_ssm), BF16),
        jax.ShapeDtypeStruct((rows, HEADS * HEAD_W), BF16),
        jax.ShapeDtypeStruct((rows, HEADS * HEAD_W), BF16),
        jax.ShapeDtypeStruct((rows, HEADS * V_HEAD), BF16),
    )
    out_specs = (
        row_spec(d_ssm),
        row_spec(HEADS * HEAD_W),
        row_spec(HEADS * HEAD_W),
        row_spec(HEADS * V_HEAD),
    )
    weights = 2 * (w["win"].size + w["wq"].size + w["wk"].size + w["wv"].size)
    tiles = 2 * tm * (4 * d + 2 * d_ssm + 4 * HEADS * HEAD_W + 2 * HEADS * V_HEAD + 4 * LANES)
    temps = 4 * tm * (d + n_in + 2 * HEADS * HEAD_W + 2 * HEADS * V_HEAD)
    return pl.pallas_call(
        functools.partial(_inproj_kernel, scale=w["scale"]),
        grid=(nb, nt),
        in_specs=in_specs,
        out_specs=out_specs,
        out_shape=out_shape,
        compiler_params=pltpu.CompilerParams(
            dimension_semantics=("arbitrary", "arbitrary"),
            vmem_limit_bytes=_vmem_limit(weights + tiles + temps)),
        name="inproj",
    )(x2d, w["mix_g"], w["win"], w["qg"], w["wq"], w["kvg"], w["wk"], w["wv"], cs)


def _s5_kernel(u_ref, h0_ref, perm_ref, bm_ref, cm_ref, are_ref, aim_ref, dskip_ref, wglu_ref, bglu_ref,
               y_ref, hT_ref, hbuf, state, *, steps):
    n_gb = bm_ref.shape[0]
    sw = are_ref.shape[2]
    cin = bm_ref.shape[1]
    nb, _, d_ssm = u_ref.shape

    @pl.when(pl.program_id(0) == 0)
    def _():
        state[...] = h0_ref[...]

    u = jnp.dot(perm_ref[0], u_ref[...].reshape(nb * steps, d_ssm),
                preferred_element_type=F32).astype(BF16)
    for gb in range(n_gb):
        hbuf[:, 2 * sw * gb:2 * sw * (gb + 1)] = jnp.dot(
            u[:, cin * gb:cin * (gb + 1)], bm_ref[gb], preferred_element_type=F32)

    for gb in range(n_gb):
        are = are_ref[gb]
        aim = aim_ref[gb]
        c_re = 2 * sw * gb
        c_im = c_re + sw

        def tstep(t, carry, c_re=c_re, c_im=c_im, are=are, aim=aim):
            hre, him = carry
            r0 = pl.multiple_of(t * SUBLANES, SUBLANES)
            nre = are * hre - aim * him + hbuf[pl.ds(r0, SUBLANES), c_re:c_re + sw]
            nim = are * him + aim * hre + hbuf[pl.ds(r0, SUBLANES), c_im:c_im + sw]
            hbuf[pl.ds(r0, SUBLANES), c_re:c_re + sw] = nre
            hbuf[pl.ds(r0, SUBLANES), c_im:c_im + sw] = nim
            return nre, nim

        hre, him = lax.fori_loop(0, steps, tstep,
                                 (state[:, c_re:c_re + sw], state[:, c_im:c_im + sw]),
                                 unroll=min(steps, 8))
        state[:, c_re:c_re + sw] = hre
        state[:, c_im:c_im + sw] = him

    ys = [jnp.dot(hbuf[:, 2 * sw * gb:2 * sw * (gb + 1)].astype(BF16), cm_ref[gb],
                  preferred_element_type=F32) for gb in range(n_gb)]
    y = jnp.concatenate(ys, axis=1) + dskip_ref[...] * u.astype(F32)
    g = jax.nn.gelu(y)
    gate = jnp.dot(g.astype(BF16), wglu_ref[...], preferred_element_type=F32) + bglu_ref[...]
    out = (g * jax.nn.sigmoid(gate)).astype(BF16)
    y_ref[...] = jnp.dot(perm_ref[1], out, preferred_element_type=F32).astype(y_ref.dtype).reshape(y_ref.shape)
    hT_ref[...] = state[...]


def _s5(u, h0, w, steps):
    nb, t_len, d_ssm = u.shape
    assert nb == SUBLANES
    r = steps * SUBLANES
    n_state = h0.shape[1]
    src = (jnp.arange(r) % SUBLANES) * steps + jnp.arange(r) // SUBLANES
    fwd = (src[:, None] == jnp.arange(r)[None, :]).astype(BF16)
    perm = jnp.stack([fwd, fwd.T])
    blk = pl.BlockSpec((nb, steps, d_ssm), lambda i: (0, i, 0))
    in_specs = [
        blk,
        _const_spec(h0.shape),
        _const_spec(perm.shape),
        _const_spec(w["bm"].shape),
        _const_spec(w["cm"].shape),
        _const_spec(w["are"].shape),
        _const_spec(w["aim"].shape),
        _const_spec((1, d_ssm)),
        _const_spec(w["wglu"].shape),
        _const_spec((1, d_ssm)),
    ]
    out_shape = (jax.ShapeDtypeStruct(u.shape, BF16),
                 jax.ShapeDtypeStruct(h0.shape, F32))
    out_specs = (blk, pl.BlockSpec(h0.shape, lambda i: (0, 0)))
    weights = (2 * (w["bm"].size + w["cm"].size + w["wglu"].size + perm.size)
               + 4 * (w["are"].size + w["aim"].size))
    tiles = 2 * r * d_ssm * 2 * 2 + 3 * 4 * h0.size
    scratch = 4 * r * n_state + 4 * h0.size
    temps = 2 * r * n_state + 6 * 4 * r * d_ssm
    return pl.pallas_call(
        functools.partial(_s5_kernel, steps=steps),
        grid=(t_len // steps,),
        in_specs=in_specs,
        out_specs=out_specs,
        out_shape=out_shape,
        scratch_shapes=[pltpu.VMEM((r, n_state), F32), pltpu.VMEM(h0.shape, F32)],
        compiler_params=pltpu.CompilerParams(
            dimension_semantics=("arbitrary",),
            vmem_limit_bytes=_vmem_limit(weights + tiles + scratch + temps)),
        name="s5",
    )(u, h0, perm, w["bm"], w["cm"], w["are"], w["aim"], w["dskip"], w["wglu"], w["bglu"])


_NT = (((1,), (1,)), ((), ()))


def _attn_update(h, s, v, m_scr, l_scr, acc_scr, first):
    n_rep = s.shape[1] // LANES
    rowmax = jnp.max(s, axis=1, keepdims=True)
    if first:
        m_new = jnp.broadcast_to(rowmax, (s.shape[0], LANES))
    else:
        m_old = m_scr[h]
        m_new = jnp.maximum(m_old, rowmax)
        alpha = jnp.exp(m_old - m_new)
    m_wide = jnp.concatenate([m_new] * n_rep, axis=1) if n_rep > 1 else m_new[:, :s.shape[1]]
    p = jnp.exp(s - m_wide)
    psum = jnp.sum(p, axis=1, keepdims=True)
    pv = jnp.dot(p.astype(v.dtype), v, preferred_element_type=F32)
    if first:
        l_scr[h] = jnp.broadcast_to(psum, m_new.shape)
        acc_scr[h] = pv
    else:
        l_scr[h] = alpha * l_scr[h] + psum
        acc_scr[h] = alpha * acc_scr[h] + pv
    m_scr[h] = m_new


def _attn_kernel(q_ref, k_ref, v_ref, km_ref, vm_ref, o_ref, m_scr, l_scr, acc_scr):
    i = pl.program_id(1)
    tq = q_ref.shape[0]

    def qh(h):
        return q_ref[:, h * HEAD_W:(h + 1) * HEAD_W]

    for h in range(HEADS):
        s = lax.dot_general(qh(h), km_ref[:, h * HEAD_W:(h + 1) * HEAD_W], _NT,
                            preferred_element_type=F32)
        _attn_update(h, s, vm_ref[:, h * V_HEAD:(h + 1) * V_HEAD], m_scr, l_scr, acc_scr, True)

    def full_block(j, carry):
        r0 = pl.multiple_of(j * ATT_TK, ATT_TK)
        for h in range(HEADS):
            s = lax.dot_general(qh(h), k_ref[pl.ds(r0, ATT_TK), h * HEAD_W:(h + 1) * HEAD_W], _NT,
                                preferred_element_type=F32)
            _attn_update(h, s, v_ref[pl.ds(r0, ATT_TK), h * V_HEAD:(h + 1) * V_HEAD],
                         m_scr, l_scr, acc_scr, False)
        return carry

    lax.fori_loop(0, i, full_block, 0)

    r0 = pl.multiple_of(i * ATT_TK, ATT_TK)
    qc = lax.broadcasted_iota(jnp.int32, (tq, ATT_TK), 0) // CHUNK
    kc = lax.broadcasted_iota(jnp.int32, (tq, ATT_TK), 1) // CHUNK
    visible = kc <= qc
    for h in range(HEADS):
        s = lax.dot_general(qh(h), k_ref[pl.ds(r0, ATT_TK), h * HEAD_W:(h + 1) * HEAD_W], _NT,
                            preferred_element_type=F32)
        s = jnp.where(visible, s, jnp.finfo(F32).min)
        _attn_update(h, s, v_ref[pl.ds(r0, ATT_TK), h * V_HEAD:(h + 1) * V_HEAD],
                     m_scr, l_scr, acc_scr, False)

    for h in range(HEADS):
        o_ref[:, h * V_HEAD:(h + 1) * V_HEAD] = (acc_scr[h] / l_scr[h]).astype(o_ref.dtype)


def _attention(q, k, v, km, vm, nb, seq):
    assert ATT_TQ == ATT_TK and ATT_TQ % CHUNK == 0 and seq % ATT_TQ == 0
    nq = seq // ATT_TQ
    wq, wv = q.shape[1], v.shape[1]
    in_specs = [
        pl.BlockSpec((ATT_TQ, wq), lambda b, i: (b * nq + i, 0)),
        pl.BlockSpec((seq, wq), lambda b, i: (b, 0)),
        pl.BlockSpec((seq, wv), lambda b, i: (b, 0)),
        _const_spec(km.shape),
        _const_spec(vm.shape),
    ]
    stat = pltpu.VMEM((HEADS, ATT_TQ, LANES), F32)
    tiles = 2 * 2 * (ATT_TQ * wq + seq * wq + seq * wv + ATT_TQ * wv) + 2 * (km.size + vm.size)
    scratch = 3 * 4 * HEADS * ATT_TQ * LANES
    temps = 16 * 4 * ATT_TQ * ATT_TK
    return pl.pallas_call(
        _attn_kernel,
        grid=(nb, nq),
        in_specs=in_specs,
        out_specs=pl.BlockSpec((ATT_TQ, wv), lambda b, i: (b * nq + i, 0)),
        out_shape=jax.ShapeDtypeStruct((nb * seq, wv), BF16),
        scratch_shapes=[stat, stat, stat],
        compiler_params=pltpu.CompilerParams(
            dimension_semantics=("arbitrary", "arbitrary"),
            vmem_limit_bytes=_vmem_limit(tiles + scratch + temps)),
        name="attn",
    )(q, k, v, km, vm)


def _attn_meta_kernel(q_ref, k_ref, v_ref, o_ref):
    for h in range(HEADS):
        s = lax.dot_general(q_ref[:, h * HEAD_W:(h + 1) * HEAD_W],
                            k_ref[:, h * HEAD_W:(h + 1) * HEAD_W], _NT, preferred_element_type=F32)
        p = jnp.exp(s - jnp.max(s, axis=1, keepdims=True))
        pv = jnp.dot(p.astype(v_ref.dtype), v_ref[:, h * V_HEAD:(h + 1) * V_HEAD],
                     preferred_element_type=F32)
        o_ref[:, h * V_HEAD:(h + 1) * V_HEAD] = (pv / jnp.sum(p, axis=1, keepdims=True)).astype(o_ref.dtype)


def _attention_meta(q, k, v):
    return pl.pallas_call(
        _attn_meta_kernel,
        out_shape=jax.ShapeDtypeStruct(v.shape, BF16),
        name="attn_meta",
    )(q, k, v)


def _outproj_kernel(ya_ref, yb_ref, x_ref, ga_ref, gb_ref, wa_ref, wb_ref, o_ref):
    a = _rms(ya_ref[...], ga_ref[...]).astype(BF16)
    b = _rms(yb_ref[...], gb_ref[...]).astype(BF16)
    mix = jnp.dot(a, wa_ref[...], preferred_element_type=F32)
    mix = mix + jnp.dot(b, wb_ref[...], preferred_element_type=F32)
    o_ref[...] = x_ref[...].astype(F32) + mix


def _outproj(ya, yb, x2d, w, nb, nt, tm):
    rows, d = x2d.shape
    d_ssm, d_attn = w["wo_a"].shape[0], w["wo_b"].shape[0]
    in_specs = [
        pl.BlockSpec((tm, d_ssm), lambda b, j: (b * nt + j, 0)),
        pl.BlockSpec((tm, d_attn), lambda b, j: (b * nt + j, 0)),
        pl.BlockSpec((tm, d), lambda b, j: (b * nt + j, 0)),
        _const_spec((1, d_ssm)),
        _const_spec((1, d_attn)),
        _const_spec(w["wo_a"].shape),
        _const_spec(w["wo_b"].shape),
    ]
    weights = 2 * (w["wo_a"].size + w["wo_b"].size)
    tiles = 2 * tm * (2 * d_ssm + 2 * d_attn + 4 * d + 4 * d)
    temps = 4 * tm * (2 * d_ssm + 2 * d_attn + 2 * d)
    return pl.pallas_call(
        _outproj_kernel,
        grid=(nb, nt),
        in_specs=in_specs,
        out_specs=pl.BlockSpec((tm, d), lambda b, j: (b * nt + j, 0)),
        out_shape=jax.ShapeDtypeStruct((rows, d), F32),
        compiler_params=pltpu.CompilerParams(
            dimension_semantics=("arbitrary", "arbitrary"),
            vmem_limit_bytes=_vmem_limit(weights + tiles + temps)),
        name="outproj",
    )(ya, yb, x2d, w["ga"], w["gb"], w["wo_a"], w["wo_b"])


def _gate_kernel(h_ref, g_ref, wg_ref, o_ref):
    xn = _rms(h_ref[...], g_ref[...]).astype(BF16)
    o_ref[...] = jnp.dot(xn, wg_ref[...], preferred_element_type=F32)


def _meta_gate(h_meta, w):
    rows, d = h_meta.shape
    n_ff = w["wdn"].shape[0]
    return pl.pallas_call(
        _gate_kernel,
        grid=(n_ff // FFN_TF,),
        in_specs=[_const_spec((rows, d)), _const_spec((1, d)),
                  pl.BlockSpec((d, FFN_TF), lambda j: (0, j))],
        out_specs=pl.BlockSpec((rows, FFN_TF), lambda j: (0, j)),
        out_shape=jax.ShapeDtypeStruct((rows, n_ff), F32),
        compiler_params=pltpu.CompilerParams(dimension_semantics=("arbitrary",)),
        name="meta_gate",
    )(h_meta, w["ffn_g"], w["wg"])


def _shift_rows(x, prev, k):
    rolled = pltpu.roll(x, k, 0)
    head = jnp.concatenate([prev, x[:SUBLANES]], axis=0)[SUBLANES - k:2 * SUBLANES - k]
    return jnp.concatenate([head, rolled[SUBLANES:]], axis=0)


def _ffn_kernel(h_ref, g_ref, wg_ref, wv_ref, cw_ref, cb_ref, wdn_ref, mh_ref, fg_ref, o_ref,
                xn_scr, halo_scr, *, tiles_per_batch):
    i = pl.program_id(0)
    j = pl.program_id(1)
    tm = h_ref.shape[0]
    ts = tm // FFN_SUB

    @pl.when(j == 0)
    def _():
        xn_scr[...] = _rms(h_ref[...], g_ref[...]).astype(BF16)
        o_ref[...] = jnp.zeros_like(o_ref)

    @pl.when((i % tiles_per_batch) == 0)
    def _():
        halo_scr[j] = mh_ref[...]

    prev = halo_scr[j]
    cw = cw_ref[...]
    for s in range(FFN_SUB):
        rows = pl.ds(s * ts, ts)
        xn = xn_scr[rows, :]
        gate = jnp.dot(xn, wg_ref[...], preferred_element_type=F32)
        val = jnp.dot(xn, wv_ref[...], preferred_element_type=F32)
        conv = (cw[0:1] * _shift_rows(gate, prev, 2)
                + cw[1:2] * _shift_rows(gate, prev, 1)
                + cw[2:3] * gate + cb_ref[...])
        prev = gate[ts - SUBLANES:, :]
        act = (jax.nn.silu(conv) * val).astype(BF16)
        o_ref[rows, :] += jnp.dot(act, wdn_ref[...], preferred_element_type=F32)
    halo_scr[j] = prev

    @pl.when(j == pl.num_programs(1) - 1)
    def _():
        o_ref[...] = _rms(h_ref[...] + o_ref[...], fg_ref[...])


def _ffn(h1, meta_halo, w, tiles_per_batch, tm):
    rows, d = h1.shape
    n_ff = w["wdn"].shape[0]
    nj = n_ff // FFN_TF
    in_specs = [
        pl.BlockSpec((tm, d), lambda i, j: (i, 0)),
        _const_spec((1, d)),
        pl.BlockSpec((d, FFN_TF), lambda i, j: (0, j)),
        pl.BlockSpec((d, FFN_TF), lambda i, j: (0, j)),
        pl.BlockSpec((CONV_W, FFN_TF), lambda i, j: (0, j)),
        pl.BlockSpec((1, FFN_TF), lambda i, j: (0, j)),
        pl.BlockSpec((FFN_TF, d), lambda i, j: (j, 0)),
        pl.BlockSpec((SUBLANES, FFN_TF), lambda i, j: (0, j)),
        _const_spec((1, d)),
    ]
    tiles = 2 * (2 * 4 * tm * d + 2 * d * 2 * FFN_TF + 2 * FFN_TF * d)
    scratch = 2 * tm * d + 4 * nj * SUBLANES * FFN_TF
    temps = 4 * tm * FFN_TF * 8 + 4 * tm * d
    return pl.pallas_call(
        functools.partial(_ffn_kernel, tiles_per_batch=tiles_per_batch),
        grid=(rows // tm, nj),
        in_specs=in_specs,
        out_specs=pl.BlockSpec((tm, d), lambda i, j: (i, 0)),
        out_shape=jax.ShapeDtypeStruct((rows, d), F32),
        scratch_shapes=[pltpu.VMEM((tm, d), BF16), pltpu.VMEM((nj, SUBLANES, FFN_TF), F32)],
        compiler_params=pltpu.CompilerParams(
            dimension_semantics=("arbitrary", "arbitrary"),
            vmem_limit_bytes=_vmem_limit(tiles + scratch + temps)),
        name="ffn",
    )(h1, w["ffn_g"], w["wg"], w["wv_ff"], w["conv_w"], w["conv_b"], w["wdn"], meta_halo, w["final_g"])


def _swap_halves(w):
    half = w.shape[-1] // 2
    return jnp.concatenate([-w[..., half:], w[..., :half]], axis=-1)


def _block_diag(blocks):
    n, r, c = blocks.shape
    eye = jnp.eye(n, dtype=blocks.dtype)
    return (blocks[:, :, None, :] * eye[:, None, :, None]).reshape(n * r, n * c)


def _prepare(p):
    d_ssm = p["d_skip"].shape[-1]
    q_lora = p["q_a_norm"].shape[-1]
    kv_lora = p["kv_a_norm"].shape[-1]
    d_ff = p["w_down"].shape[0]
    w = {"d_ssm": d_ssm, "scale": 1.0 / math.sqrt(QK_NOPE + QK_ROPE)}

    row = lambda v: v.reshape(1, -1).astype(F32)
    w["mix_g"] = row(p["mix_norm"])
    w_in = p["w_in"]
    o3 = d_ssm + q_lora + kv_lora
    w["win"] = jnp.concatenate([w_in, _swap_halves(w_in[:, o3:])], axis=1).astype(BF16)
    w["qg"] = row(p["q_a_norm"])
    w["kvg"] = row(p["kv_a_norm"])
    wq = p["w_q_b"].reshape(q_lora, HEADS, QK_NOPE + QK_ROPE)
    wq_pe = wq[..., QK_NOPE:]
    w["wq"] = jnp.concatenate([wq, _swap_halves(wq_pe)], axis=-1).reshape(q_lora, HEADS * HEAD_W).astype(BF16)
    wkv = p["w_kv_b"].reshape(kv_lora, HEADS, QK_NOPE + V_HEAD)
    w["wk"] = wkv[..., :QK_NOPE].reshape(kv_lora, HEADS * QK_NOPE).astype(BF16)
    w["wv"] = wkv[..., QK_NOPE:].reshape(kv_lora, HEADS * V_HEAD).astype(BF16)

    lam_re, lam_im = p["lam_re"].astype(F32), p["lam_im"].astype(F32)
    dt = jnp.exp(p["log_dt"].astype(F32))[:, None]
    mag = jnp.exp(lam_re * dt)
    a_re, a_im = mag * jnp.cos(lam_im * dt), mag * jnp.sin(lam_im * dt)
    den = lam_re * lam_re + lam_im * lam_im
    f_re = ((a_re - 1.0) * lam_re + a_im * lam_im) / den
    f_im = (a_im * lam_re - (a_re - 1.0) * lam_im) / den
    b_re, b_im = p["b_re"].astype(F32), p["b_im"].astype(F32)
    bb_re = f_re[..., None] * b_re - f_im[..., None] * b_im
    bb_im = f_re[..., None] * b_im + f_im[..., None] * b_re
    n_g = lam_re.shape[0]
    n_gb = n_g // GROUPS_PER_MXU
    grp = lambda t: t.reshape(n_gb, GROUPS_PER_MXU, *t.shape[1:])
    bd = jax.vmap(_block_diag)
    w["bm"] = jnp.concatenate([bd(grp(jnp.swapaxes(bb_re, 1, 2))),
                               bd(grp(jnp.swapaxes(bb_im, 1, 2)))], axis=2).astype(BF16)
    c_re, c_im = p["c_re"].astype(F32), p["c_im"].astype(F32)
    w["cm"] = jnp.concatenate([bd(grp(jnp.swapaxes(c_re, 1, 2))),
                               bd(grp(jnp.swapaxes(-c_im, 1, 2)))], axis=1).astype(BF16)
    bcast = lambda a: jnp.broadcast_to(a.reshape(n_gb, 1, STATE_W), (n_gb, SUBLANES, STATE_W))
    w["are"], w["aim"] = bcast(a_re), bcast(a_im)
    w["dskip"] = row(p["d_skip"])
    w["wglu"] = p["w_glu"].astype(BF16)
    w["bglu"] = row(p["b_glu"])

    w["ga"], w["gb"] = row(p["out_norm_ssm"]), row(p["out_norm_attn"])
    w["wo_a"] = p["w_out"][:d_ssm].astype(BF16)
    w["wo_b"] = p["w_out"][d_ssm:].astype(BF16)

    n_ff = -(-d_ff // FFN_TF) * FFN_TF
    pad = n_ff - d_ff
    w["wg"] = jnp.pad(p["w_up"][:, :d_ff].astype(BF16), ((0, 0), (0, pad)))
    w["wv_ff"] = jnp.pad(p["w_up"][:, d_ff:].astype(BF16), ((0, 0), (0, pad)))
    w["wdn"] = jnp.pad(p["w_down"].astype(BF16), ((0, pad), (0, 0)))
    w["conv_w"] = jnp.pad(p["conv_w"].astype(F32), ((0, 0), (0, pad)))
    w["conv_b"] = jnp.pad(row(p["conv_b"]), ((0, 0), (0, pad)))
    w["ffn_g"] = row(p["ffn_norm"])
    w["final_g"] = row(p["final_norm"])
    return w


def _rope_table(n_pos):
    pos = jnp.arange(n_pos, dtype=F32)
    inv_freq = 1.0 / (ROPE_BASE ** (jnp.arange(0, QK_ROPE, 2, dtype=F32) / QK_ROPE))
    ang = pos[:, None] * inv_freq[None, :]
    cos, sin = jnp.cos(ang), jnp.sin(ang)
    return jnp.concatenate([cos, cos, sin, sin], axis=1)


def kernel(x, meta_tokens, mix_norm, w_in, lam_re, lam_im, log_dt, b_re, b_im, c_re, c_im, d_skip, w_glu, b_glu, q_a_norm, w_q_b, kv_a_norm, w_kv_b, out_norm_ssm, out_norm_attn, w_out, ffn_norm, w_up, conv_w, conv_b, w_down, final_norm):
    bsz, seq, d = x.shape
    assert meta_tokens.shape == (N_META, d) and bsz == SUBLANES
    p = dict(mix_norm=mix_norm[0], w_in=w_in[0], lam_re=lam_re[0], lam_im=lam_im[0], log_dt=log_dt[0],
             b_re=b_re[0], b_im=b_im[0], c_re=c_re[0], c_im=c_im[0], d_skip=d_skip[0],
             w_glu=w_glu[0], b_glu=b_glu[0], q_a_norm=q_a_norm[0], w_q_b=w_q_b[0],
             kv_a_norm=kv_a_norm[0], w_kv_b=w_kv_b[0], out_norm_ssm=out_norm_ssm[0],
             out_norm_attn=out_norm_attn[0], w_out=w_out[0], ffn_norm=ffn_norm[0], w_up=w_up[0],
             conv_w=conv_w[0], conv_b=conv_b[0], w_down=w_down[0], final_norm=final_norm)
    w = _prepare(p)
    d_ssm = w["d_ssm"]
    cs = _rope_table(N_META + seq)
    x2d = x.reshape(bsz * seq, d)
    meta = meta_tokens.astype(x.dtype)

    u_m, q_m, k_m, v_m = _inproj(meta, cs[:N_META], w, 1, 1, N_META)
    n_state = 2 * STATE_W * w["bm"].shape[0]
    ya_m8, h_meta_state = _s5(jnp.broadcast_to(u_m[None], (SUBLANES, N_META, d_ssm)),
                              jnp.zeros((SUBLANES, n_state), F32), w, N_META)
    ya_m = ya_m8[0]
    yb_m = _attention_meta(q_m, k_m, v_m)
    h1_m = _outproj(ya_m, yb_m, meta, w, 1, 1, N_META)
    meta_halo = _meta_gate(h1_m, w)[N_META - SUBLANES:]

    tm = 512
    nt = seq // tm
    u, q, k, v = _inproj(x2d, cs[N_META:], w, bsz, nt, tm)
    ya, _ = _s5(u.reshape(bsz, seq, d_ssm), h_meta_state, w, 64)
    yb = _attention(q, k, v, k_m, v_m, bsz, seq)
    h1 = _outproj(ya.reshape(bsz * seq, d_ssm), yb, x2d, w, bsz, nt, tm)
    out = _ffn(h1, meta_halo, w, nt, tm)
    return out.reshape(bsz, seq, d)
```

```python
import functools
import math

import jax
import jax.numpy as jnp
from jax import lax
from jax.experimental import pallas as pl
from jax.experimental.pallas import tpu as pltpu

F32 = jnp.float32
BF16 = jnp.bfloat16

EPS = 1e-6
CHUNK = 64
N_META = 16
SSM_GROUP = 16
SSM_STATE = 64
HEADS = 8
QK_NOPE = 128
QK_ROPE = 64
V_HEAD = 128
ROPE_BASE = 10000.0
CONV_W = 3

LANES = 128
SUBLANES = 8
MXU_DIM = 256
VMEM_BYTES = 64 * 1024 * 1024

HEAD_W = QK_NOPE + 2 * QK_ROPE
GROUPS_PER_MXU = MXU_DIM // SSM_GROUP
STATE_W = GROUPS_PER_MXU * SSM_STATE

ATT_TQ = 512
FFN_TF = 512
FFN_SUB = 2


_NT = (((1,), (1,)), ((), ()))


def _vmem_limit(nbytes):
    return int(min(nbytes + (8 << 20), VMEM_BYTES - (4 << 20)))


def _rms(x, g):
    x = x.astype(F32)
    return x * lax.rsqrt(jnp.mean(x * x, axis=-1, keepdims=True) + EPS) * g


def _const_spec(shape):
    nd = len(shape)
    return pl.BlockSpec(shape, lambda *_: (0,) * nd, pipeline_mode=pl.Buffered(1))


def _inproj_kernel(x_ref, g_ref, win_ref, qg_ref, wq_ref, kvg_ref, wk_ref, wv_ref, cs_ref,
                   u_ref, q_ref, k_ref, v_ref, *, scale):
    d_ssm = u_ref.shape[1]
    q_lora = wq_ref.shape[0]
    kv_lora = wk_ref.shape[0]
    xn = _rms(x_ref[...], g_ref[...]).astype(BF16)
    z = jnp.dot(xn, win_ref[...], preferred_element_type=F32)
    o1, o2, o3 = d_ssm, d_ssm + q_lora, d_ssm + q_lora + kv_lora
    u_ref[...] = z[:, :o1].astype(u_ref.dtype)

    cs = cs_ref[...]
    qn = _rms(z[:, o1:o2], qg_ref[...]).astype(BF16)
    q = jnp.dot(qn, wq_ref[...], preferred_element_type=F32)
    cs_q = cs * scale
    for h in range(HEADS):
        c0 = h * HEAD_W
        q_ref[:, c0:c0 + QK_NOPE] = (q[:, c0:c0 + QK_NOPE] * scale).astype(q_ref.dtype)
        q_ref[:, c0 + QK_NOPE:c0 + HEAD_W] = (q[:, c0 + QK_NOPE:c0 + HEAD_W] * cs_q).astype(q_ref.dtype)

    kvn = _rms(z[:, o2:o3], kvg_ref[...]).astype(BF16)
    kn = jnp.dot(kvn, wk_ref[...], preferred_element_type=F32)
    v_ref[...] = lax.dot_general(wv_ref[...], kvn, _NT, preferred_element_type=F32).astype(v_ref.dtype)
    t = z[:, o3:o3 + 2 * QK_ROPE] * cs
    krot = (t + pltpu.roll(t, QK_ROPE, 1)).astype(k_ref.dtype)
    for h in range(HEADS):
        c0 = h * HEAD_W
        k_ref[:, c0:c0 + QK_NOPE] = kn[:, h * QK_NOPE:(h + 1) * QK_NOPE].astype(k_ref.dtype)
        k_ref[:, c0 + QK_NOPE:c0 + HEAD_W] = krot


def _inproj(x2d, cs, w, nb, nt, tm):
    rows, d = x2d.shape
    d_ssm = w["d_ssm"]
    n_in = w["win"].shape[1]
    row_spec = lambda width: pl.BlockSpec((tm, width), lambda b, j: (b * nt + j, 0))
    in_specs = [
        row_spec(d),
        _const_spec((1, d)),
        _const_spec(w["win"].shape),
        _const_spec(w["qg"].shape),
        _const_spec(w["wq"].shape),
        _const_spec(w["kvg"].shape),
        _const_spec(w["wk"].shape),
        _const_spec(w["wv"].shape),
        pl.BlockSpec((tm, LANES), lambda b, j: (j, 0)),
    ]
    out_shape = (
        jax.ShapeDtypeStruct((rows, d_ssm), BF16),
        jax.ShapeDtypeStruct((rows, HEADS * HEAD_W), BF16),
        jax.ShapeDtypeStruct((rows, HEADS * HEAD_W), BF16),
        jax.ShapeDtypeStruct((nb, HEADS * V_HEAD, nt * tm), BF16),
    )
    out_specs = (
        row_spec(d_ssm),
        row_spec(HEADS * HEAD_W),
        row_spec(HEADS * HEAD_W),
        pl.BlockSpec((None, HEADS * V_HEAD, tm), lambda b, j: (b, 0, j)),
    )
    weights = 2 * (w["win"].size + w["wq"].size + w["wk"].size + w["wv"].size)
    tiles = 2 * tm * (4 * d + 2 * d_ssm + 4 * HEADS * HEAD_W + 2 * HEADS * V_HEAD + 4 * LANES)
    temps = 4 * tm * (d + n_in + 2 * HEADS * HEAD_W + 2 * HEADS * V_HEAD)
    return pl.pallas_call(
        functools.partial(_inproj_kernel, scale=w["scale"]),
        grid=(nb, nt),
        in_specs=in_specs,
        out_specs=out_specs,
        out_shape=out_shape,
        compiler_params=pltpu.CompilerParams(
            dimension_semantics=("arbitrary", "arbitrary"),
            vmem_limit_bytes=_vmem_limit(weights + tiles + temps)),
        name="inproj",
    )(x2d, w["mix_g"], w["win"], w["qg"], w["wq"], w["kvg"], w["wk"], w["wv"], cs)


def _s5_kernel(u_ref, h0_ref, perm_ref, bm_ref, cm_ref, are_ref, aim_ref, dskip_ref, wglu_ref, bglu_ref,
               y_ref, hT_ref, hbuf, state, *, steps):
    n_gb = bm_ref.shape[0]
    sw = are_ref.shape[2]
    cin = bm_ref.shape[1]
    nb, _, d_ssm = u_ref.shape

    @pl.when(pl.program_id(0) == 0)
    def _():
        state[...] = h0_ref[...]

    u = jnp.dot(perm_ref[0], u_ref[...].reshape(nb * steps, d_ssm),
                preferred_element_type=F32).astype(BF16)
    for gb in range(n_gb):
        hbuf[:, 2 * sw * gb:2 * sw * (gb + 1)] = jnp.dot(
            u[:, cin * gb:cin * (gb + 1)], bm_ref[gb], preferred_element_type=F32)

    for gb in range(n_gb):
        are = are_ref[gb]
        aim = aim_ref[gb]
        c_re = 2 * sw * gb
        c_im = c_re + sw

        def tstep(t, carry, c_re=c_re, c_im=c_im, are=are, aim=aim):
            hre, him = carry
            r0 = pl.multiple_of(t * SUBLANES, SUBLANES)
            nre = are * hre - aim * him + hbuf[pl.ds(r0, SUBLANES), c_re:c_re + sw]
            nim = are * him + aim * hre + hbuf[pl.ds(r0, SUBLANES), c_im:c_im + sw]
            hbuf[pl.ds(r0, SUBLANES), c_re:c_re + sw] = nre
            hbuf[pl.ds(r0, SUBLANES), c_im:c_im + sw] = nim
            return nre, nim

        hre, him = lax.fori_loop(0, steps, tstep,
                                 (state[:, c_re:c_re + sw], state[:, c_im:c_im + sw]),
                                 unroll=min(steps, 8))
        state[:, c_re:c_re + sw] = hre
        state[:, c_im:c_im + sw] = him

    ys = [jnp.dot(hbuf[:, 2 * sw * gb:2 * sw * (gb + 1)].astype(BF16), cm_ref[gb],
                  preferred_element_type=F32) for gb in range(n_gb)]
    y = jnp.concatenate(ys, axis=1) + dskip_ref[...] * u.astype(F32)
    g = jax.nn.gelu(y)
    gate = jnp.dot(g.astype(BF16), wglu_ref[...], preferred_element_type=F32) + bglu_ref[...]
    out = (g * jax.nn.sigmoid(gate)).astype(BF16)
    y_ref[...] = jnp.dot(perm_ref[1], out, preferred_element_type=F32).astype(y_ref.dtype).reshape(y_ref.shape)
    hT_ref[...] = state[...]


def _s5(u, h0, w, steps):
    nb, t_len, d_ssm = u.shape
    assert nb == SUBLANES
    r = steps * SUBLANES
    n_state = h0.shape[1]
    src = (jnp.arange(r) % SUBLANES) * steps + jnp.arange(r) // SUBLANES
    fwd = (src[:, None] == jnp.arange(r)[None, :]).astype(BF16)
    perm = jnp.stack([fwd, fwd.T])
    blk = pl.BlockSpec((nb, steps, d_ssm), lambda i: (0, i, 0))
    in_specs = [
        blk,
        _const_spec(h0.shape),
        _const_spec(perm.shape),
        _const_spec(w["bm"].shape),
        _const_spec(w["cm"].shape),
        _const_spec(w["are"].shape),
        _const_spec(w["aim"].shape),
        _const_spec((1, d_ssm)),
        _const_spec(w["wglu"].shape),
        _const_spec((1, d_ssm)),
    ]
    out_shape = (jax.ShapeDtypeStruct(u.shape, BF16),
                 jax.ShapeDtypeStruct(h0.shape, F32))
    out_specs = (blk, pl.BlockSpec(h0.shape, lambda i: (0, 0)))
    weights = (2 * (w["bm"].size + w["cm"].size + w["wglu"].size + perm.size)
               + 4 * (w["are"].size + w["aim"].size))
    tiles = 2 * r * d_ssm * 2 * 2 + 3 * 4 * h0.size
    scratch = 4 * r * n_state + 4 * h0.size
    temps = 2 * r * n_state + 6 * 4 * r * d_ssm
    return pl.pallas_call(
        functools.partial(_s5_kernel, steps=steps),
        grid=(t_len // steps,),
        in_specs=in_specs,
        out_specs=out_specs,
        out_shape=out_shape,
        scratch_shapes=[pltpu.VMEM((r, n_state), F32), pltpu.VMEM(h0.shape, F32)],
        compiler_params=pltpu.CompilerParams(
            dimension_semantics=("arbitrary",),
            vmem_limit_bytes=_vmem_limit(weights + tiles + scratch + temps)),
        name="s5",
    )(u, h0, perm, w["bm"], w["cm"], w["are"], w["aim"], w["dskip"], w["wglu"], w["bglu"])


def _attn_kernel(q_ref, k_ref, vt_ref, km_ref, vmt_ref, o_ref):
    seq = q_ref.shape[0]
    half = ATT_TQ // 2
    neg = jnp.finfo(F32).min
    kc = lax.broadcasted_iota(jnp.int32, (half, half), 0) // CHUNK
    qc = lax.broadcasted_iota(jnp.int32, (half, half), 1) // CHUNK
    visible = kc <= qc
    colmax = lambda t: jnp.max(t, axis=0, keepdims=True)
    colsum = lambda t: jnp.sum(t, axis=0, keepdims=True)
    km = km_ref[...]
    vmt = vmt_ref[...]
    for i in range(seq // ATT_TQ):
        r0 = i * ATT_TQ
        q = q_ref[r0:r0 + ATT_TQ, :]
        s_meta = lax.dot_general(km, q, _NT, preferred_element_type=F32)
        s_diag = lax.dot_general(k_ref[r0:r0 + ATT_TQ, :], q, _NT, preferred_element_type=F32)
        s00 = jnp.where(visible, s_diag[:half, :half], neg)
        s01 = s_diag[:half, half:]
        s11 = jnp.where(visible, s_diag[half:, half:], neg)
        m = jnp.concatenate([colmax(s00), jnp.maximum(colmax(s01), colmax(s11))], axis=1)
        m = jnp.maximum(m, colmax(s_meta))
        if i > 0:
            s_full = lax.dot_general(k_ref[0:r0, :], q, _NT, preferred_element_type=F32)
            m = jnp.maximum(m, colmax(s_full))
        p_meta = jnp.exp2(s_meta - m)
        p00 = jnp.exp2(s00 - m[:, :half])
        p01 = jnp.exp2(s01 - m[:, half:])
        p11 = jnp.exp2(s11 - m[:, half:])
        l = colsum(p_meta) + jnp.concatenate([colsum(p00), colsum(p01) + colsum(p11)], axis=1)
        p_diag = jnp.concatenate([jnp.concatenate([p00, p01], axis=1),
                                  jnp.concatenate([jnp.zeros_like(p11), p11], axis=1)], axis=0)
        o = jnp.dot(vmt, p_meta.astype(BF16), preferred_element_type=F32)
        o = o + jnp.dot(vt_ref[:, r0:r0 + ATT_TQ], p_diag.astype(BF16), preferred_element_type=F32)
        if i > 0:
            p_full = jnp.exp2(s_full - m)
            l = l + colsum(p_full)
            o = o + jnp.dot(vt_ref[:, 0:r0], p_full.astype(BF16), preferred_element_type=F32)
        o_ref[r0:r0 + ATT_TQ, :] = (o / l).T.astype(o_ref.dtype)


def _attention(q, k, vt, km, vmt, nb, seq):
    assert ATT_TQ % CHUNK == 0 and seq % ATT_TQ == 0 and V_HEAD == LANES
    n_meta = km.shape[0]
    in_specs = [
        pl.BlockSpec((seq, HEAD_W), lambda b, h: (b, h)),
        pl.BlockSpec((seq, HEAD_W), lambda b, h: (b, h)),
        pl.BlockSpec((None, V_HEAD, seq), lambda b, h: (b, h, 0)),
        pl.BlockSpec((n_meta, HEAD_W), lambda b, h: (0, h)),
        pl.BlockSpec((V_HEAD, n_meta), lambda b, h: (h, 0)),
    ]
    tiles = 2 * 2 * seq * (2 * HEAD_W + 2 * V_HEAD)
    temps = 4 * 4 * seq * ATT_TQ * 2
    return pl.pallas_call(
        _attn_kernel,
        grid=(nb, HEADS),
        in_specs=in_specs,
        out_specs=pl.BlockSpec((seq, V_HEAD), lambda b, h: (b, h)),
        out_shape=jax.ShapeDtypeStruct((nb * seq, HEADS * V_HEAD), BF16),
        compiler_params=pltpu.CompilerParams(
            dimension_semantics=("arbitrary", "arbitrary"),
            vmem_limit_bytes=_vmem_limit(tiles + temps)),
        name="attn",
    )(q, k, vt, km, vmt)


def _attn_meta_kernel(q_ref, k_ref, vt_ref, o_ref):
    for h in range(HEADS):
        s = lax.dot_general(q_ref[:, h * HEAD_W:(h + 1) * HEAD_W],
                            k_ref[:, h * HEAD_W:(h + 1) * HEAD_W], _NT, preferred_element_type=F32)
        p = jnp.exp2(s - jnp.max(s, axis=1, keepdims=True))
        pv = lax.dot_general(p.astype(BF16), vt_ref[h * V_HEAD:(h + 1) * V_HEAD, :], _NT,
                             preferred_element_type=F32)
        o_ref[:, h * V_HEAD:(h + 1) * V_HEAD] = (pv / jnp.sum(p, axis=1, keepdims=True)).astype(o_ref.dtype)


def _attention_meta(q, k, vt):
    return pl.pallas_call(
        _attn_meta_kernel,
        out_shape=jax.ShapeDtypeStruct((q.shape[0], vt.shape[0]), BF16),
        name="attn_meta",
    )(q, k, vt)


def _outproj_kernel(ya_ref, yb_ref, x_ref, ga_ref, gb_ref, wa_ref, wb_ref, o_ref):
    a = _rms(ya_ref[...], ga_ref[...]).astype(BF16)
    b = _rms(yb_ref[...], gb_ref[...]).astype(BF16)
    mix = jnp.dot(a, wa_ref[...], preferred_element_type=F32)
    mix = mix + jnp.dot(b, wb_ref[...], preferred_element_type=F32)
    o_ref[...] = x_ref[...].astype(F32) + mix


def _outproj(ya, yb, x2d, w, nb, nt, tm):
    rows, d = x2d.shape
    d_ssm, d_attn = w["wo_a"].shape[0], w["wo_b"].shape[0]
    in_specs = [
        pl.BlockSpec((tm, d_ssm), lambda b, j: (b * nt + j, 0)),
        pl.BlockSpec((tm, d_attn), lambda b, j: (b * nt + j, 0)),
        pl.BlockSpec((tm, d), lambda b, j: (b * nt + j, 0)),
        _const_spec((1, d_ssm)),
        _const_spec((1, d_attn)),
        _const_spec(w["wo_a"].shape),
        _const_spec(w["wo_b"].shape),
    ]
    weights = 2 * (w["wo_a"].size + w["wo_b"].size)
    tiles = 2 * tm * (2 * d_ssm + 2 * d_attn + 4 * d + 4 * d)
    temps = 4 * tm * (2 * d_ssm + 2 * d_attn + 2 * d)
    return pl.pallas_call(
        _outproj_kernel,
        grid=(nb, nt),
        in_specs=in_specs,
        out_specs=pl.BlockSpec((tm, d), lambda b, j: (b * nt + j, 0)),
        out_shape=jax.ShapeDtypeStruct((rows, d), F32),
        compiler_params=pltpu.CompilerParams(
            dimension_semantics=("arbitrary", "arbitrary"),
            vmem_limit_bytes=_vmem_limit(weights + tiles + temps)),
        name="outproj",
    )(ya, yb, x2d, w["ga"], w["gb"], w["wo_a"], w["wo_b"])


def _gate_kernel(h_ref, g_ref, wg_ref, o_ref):
    xn = _rms(h_ref[...], g_ref[...]).astype(BF16)
    o_ref[...] = jnp.dot(xn, wg_ref[...], preferred_element_type=F32)


def _meta_gate(h_meta, w):
    rows, d = h_meta.shape
    n_ff = w["wdn"].shape[0]
    return pl.pallas_call(
        _gate_kernel,
        grid=(n_ff // FFN_TF,),
        in_specs=[_const_spec((rows, d)), _const_spec((1, d)),
                  pl.BlockSpec((d, FFN_TF), lambda j: (0, j))],
        out_specs=pl.BlockSpec((rows, FFN_TF), lambda j: (0, j)),
        out_shape=jax.ShapeDtypeStruct((rows, n_ff), F32),
        compiler_params=pltpu.CompilerParams(dimension_semantics=("arbitrary",)),
        name="meta_gate",
    )(h_meta, w["ffn_g"], w["wg"])


def _shift_rows(x, prev, k):
    rolled = pltpu.roll(x, k, 0)
    head = jnp.concatenate([prev, x[:SUBLANES]], axis=0)[SUBLANES - k:2 * SUBLANES - k]
    return jnp.concatenate([head, rolled[SUBLANES:]], axis=0)


def _ffn_kernel(h_ref, g_ref, wg_ref, wv_ref, cw_ref, cb_ref, wdn_ref, mh_ref, fg_ref, o_ref,
                xn_scr, halo_scr, *, tiles_per_batch):
    i = pl.program_id(0)
    j = pl.program_id(1)
    tm = h_ref.shape[0]
    ts = tm // FFN_SUB

    @pl.when(j == 0)
    def _():
        xn_scr[...] = _rms(h_ref[...], g_ref[...]).astype(BF16)
        o_ref[...] = jnp.zeros_like(o_ref)

    @pl.when((i % tiles_per_batch) == 0)
    def _():
        halo_scr[j] = mh_ref[...]

    prev = halo_scr[j]
    cw = cw_ref[...]
    for s in range(FFN_SUB):
        rows = pl.ds(s * ts, ts)
        xn = xn_scr[rows, :]
        gate = jnp.dot(xn, wg_ref[...], preferred_element_type=F32)
        val = jnp.dot(xn, wv_ref[...], preferred_element_type=F32)
        conv = (cw[0:1] * _shift_rows(gate, prev, 2)
                + cw[1:2] * _shift_rows(gate, prev, 1)
                + cw[2:3] * gate + cb_ref[...])
        prev = gate[ts - SUBLANES:, :]
        act = (jax.nn.silu(conv) * val).astype(BF16)
        o_ref[rows, :] += jnp.dot(act, wdn_ref[...], preferred_element_type=F32)
    halo_scr[j] = prev

    @pl.when(j == pl.num_programs(1) - 1)
    def _():
        o_ref[...] = _rms(h_ref[...] + o_ref[...], fg_ref[...])


def _ffn(h1, meta_halo, w, tiles_per_batch, tm):
    rows, d = h1.shape
    n_ff = w["wdn"].shape[0]
    nj = n_ff // FFN_TF
    in_specs = [
        pl.BlockSpec((tm, d), lambda i, j: (i, 0)),
        _const_spec((1, d)),
        pl.BlockSpec((d, FFN_TF), lambda i, j: (0, j)),
        pl.BlockSpec((d, FFN_TF), lambda i, j: (0, j)),
        pl.BlockSpec((CONV_W, FFN_TF), lambda i, j: (0, j)),
        pl.BlockSpec((1, FFN_TF), lambda i, j: (0, j)),
        pl.BlockSpec((FFN_TF, d), lambda i, j: (j, 0)),
        pl.BlockSpec((SUBLANES, FFN_TF), lambda i, j: (0, j)),
        _const_spec((1, d)),
    ]
    tiles = 2 * (2 * 4 * tm * d + 2 * d * 2 * FFN_TF + 2 * FFN_TF * d)
    scratch = 2 * tm * d + 4 * nj * SUBLANES * FFN_TF
    temps = 4 * tm * FFN_TF * 8 + 4 * tm * d
    return pl.pallas_call(
        functools.partial(_ffn_kernel, tiles_per_batch=tiles_per_batch),
        grid=(rows // tm, nj),
        in_specs=in_specs,
        out_specs=pl.BlockSpec((tm, d), lambda i, j: (i, 0)),
        out_shape=jax.ShapeDtypeStruct((rows, d), F32),
        scratch_shapes=[pltpu.VMEM((tm, d), BF16), pltpu.VMEM((nj, SUBLANES, FFN_TF), F32)],
        compiler_params=pltpu.CompilerParams(
            dimension_semantics=("arbitrary", "arbitrary"),
            vmem_limit_bytes=_vmem_limit(tiles + scratch + temps)),
        name="ffn",
    )(h1, w["ffn_g"], w["wg"], w["wv_ff"], w["conv_w"], w["conv_b"], w["wdn"], meta_halo, w["final_g"])


def _swap_halves(w):
    half = w.shape[-1] // 2
    return jnp.concatenate([-w[..., half:], w[..., :half]], axis=-1)


def _block_diag(blocks):
    n, r, c = blocks.shape
    eye = jnp.eye(n, dtype=blocks.dtype)
    return (blocks[:, :, None, :] * eye[:, None, :, None]).reshape(n * r, n * c)


def _prepare(p):
    d_ssm = p["d_skip"].shape[-1]
    q_lora = p["q_a_norm"].shape[-1]
    kv_lora = p["kv_a_norm"].shape[-1]
    d_ff = p["w_down"].shape[0]
    w = {"d_ssm": d_ssm, "scale": math.log2(math.e) / math.sqrt(QK_NOPE + QK_ROPE)}

    row = lambda v: v.reshape(1, -1).astype(F32)
    w["mix_g"] = row(p["mix_norm"])
    w_in = p["w_in"]
    o3 = d_ssm + q_lora + kv_lora
    w["win"] = jnp.concatenate([w_in, _swap_halves(w_in[:, o3:])], axis=1).astype(BF16)
    w["qg"] = row(p["q_a_norm"])
    w["kvg"] = row(p["kv_a_norm"])
    wq = p["w_q_b"].reshape(q_lora, HEADS, QK_NOPE + QK_ROPE)
    wq_pe = wq[..., QK_NOPE:]
    w["wq"] = jnp.concatenate([wq, _swap_halves(wq_pe)], axis=-1).reshape(q_lora, HEADS * HEAD_W).astype(BF16)
    wkv = p["w_kv_b"].reshape(kv_lora, HEADS, QK_NOPE + V_HEAD)
    w["wk"] = wkv[..., :QK_NOPE].reshape(kv_lora, HEADS * QK_NOPE).astype(BF16)
    w["wv"] = wkv[..., QK_NOPE:].reshape(kv_lora, HEADS * V_HEAD).T.astype(BF16)

    lam_re, lam_im = p["lam_re"].astype(F32), p["lam_im"].astype(F32)
    dt = jnp.exp(p["log_dt"].astype(F32))[:, None]
    mag = jnp.exp(lam_re * dt)
    a_re, a_im = mag * jnp.cos(lam_im * dt), mag * jnp.sin(lam_im * dt)
    den = lam_re * lam_re + lam_im * lam_im
    f_re = ((a_re - 1.0) * lam_re + a_im * lam_im) / den
    f_im = (a_im * lam_re - (a_re - 1.0) * lam_im) / den
    b_re, b_im = p["b_re"].astype(F32), p["b_im"].astype(F32)
    bb_re = f_re[..., None] * b_re - f_im[..., None] * b_im
    bb_im = f_re[..., None] * b_im + f_im[..., None] * b_re
    n_g = lam_re.shape[0]
    n_gb = n_g // GROUPS_PER_MXU
    grp = lambda t: t.reshape(n_gb, GROUPS_PER_MXU, *t.shape[1:])
    bd = jax.vmap(_block_diag)
    w["bm"] = jnp.concatenate([bd(grp(jnp.swapaxes(bb_re, 1, 2))),
                               bd(grp(jnp.swapaxes(bb_im, 1, 2)))], axis=2).astype(BF16)
    c_re, c_im = p["c_re"].astype(F32), p["c_im"].astype(F32)
    w["cm"] = jnp.concatenate([bd(grp(jnp.swapaxes(c_re, 1, 2))),
                               bd(grp(jnp.swapaxes(-c_im, 1, 2)))], axis=1).astype(BF16)
    bcast = lambda a: jnp.broadcast_to(a.reshape(n_gb, 1, STATE_W), (n_gb, SUBLANES, STATE_W))
    w["are"], w["aim"] = bcast(a_re), bcast(a_im)
    w["dskip"] = row(p["d_skip"])
    w["wglu"] = p["w_glu"].astype(BF16)
    w["bglu"] = row(p["b_glu"])

    w["ga"], w["gb"] = row(p["out_norm_ssm"]), row(p["out_norm_attn"])
    w["wo_a"] = p["w_out"][:d_ssm].astype(BF16)
    w["wo_b"] = p["w_out"][d_ssm:].astype(BF16)

    n_ff = -(-d_ff // FFN_TF) * FFN_TF
    pad = n_ff - d_ff
    w["wg"] = jnp.pad(p["w_up"][:, :d_ff].astype(BF16), ((0, 0), (0, pad)))
    w["wv_ff"] = jnp.pad(p["w_up"][:, d_ff:].astype(BF16), ((0, 0), (0, pad)))
    w["wdn"] = jnp.pad(p["w_down"].astype(BF16), ((0, pad), (0, 0)))
    w["conv_w"] = jnp.pad(p["conv_w"].astype(F32), ((0, 0), (0, pad)))
    w["conv_b"] = jnp.pad(row(p["conv_b"]), ((0, 0), (0, pad)))
    w["ffn_g"] = row(p["ffn_norm"])
    w["final_g"] = row(p["final_norm"])
    return w


def _rope_table(n_pos):
    pos = jnp.arange(n_pos, dtype=F32)
    inv_freq = 1.0 / (ROPE_BASE ** (jnp.arange(0, QK_ROPE, 2, dtype=F32) / QK_ROPE))
    ang = pos[:, None] * inv_freq[None, :]
    cos, sin = jnp.cos(ang), jnp.sin(ang)
    return jnp.concatenate([cos, cos, sin, sin], axis=1)


def kernel(x, meta_tokens, mix_norm, w_in, lam_re, lam_im, log_dt, b_re, b_im, c_re, c_im, d_skip, w_glu, b_glu, q_a_norm, w_q_b, kv_a_norm, w_kv_b, out_norm_ssm, out_norm_attn, w_out, ffn_norm, w_up, conv_w, conv_b, w_down, final_norm):
    bsz, seq, d = x.shape
    assert meta_tokens.shape == (N_META, d) and bsz == SUBLANES
    p = dict(mix_norm=mix_norm[0], w_in=w_in[0], lam_re=lam_re[0], lam_im=lam_im[0], log_dt=log_dt[0],
             b_re=b_re[0], b_im=b_im[0], c_re=c_re[0], c_im=c_im[0], d_skip=d_skip[0],
             w_glu=w_glu[0], b_glu=b_glu[0], q_a_norm=q_a_norm[0], w_q_b=w_q_b[0],
             kv_a_norm=kv_a_norm[0], w_kv_b=w_kv_b[0], out_norm_ssm=out_norm_ssm[0],
             out_norm_attn=out_norm_attn[0], w_out=w_out[0], ffn_norm=ffn_norm[0], w_up=w_up[0],
             conv_w=conv_w[0], conv_b=conv_b[0], w_down=w_down[0], final_norm=final_norm)
    w = _prepare(p)
    d_ssm = w["d_ssm"]
    cs = _rope_table(N_META + seq)
    x2d = x.reshape(bsz * seq, d)
    meta = meta_tokens.astype(x.dtype)

    u_m, q_m, k_m, v_m = _inproj(meta, cs[:N_META], w, 1, 1, N_META)
    n_state = 2 * STATE_W * w["bm"].shape[0]
    ya_m8, h_meta_state = _s5(jnp.broadcast_to(u_m[None], (SUBLANES, N_META, d_ssm)),
                              jnp.zeros((SUBLANES, n_state), F32), w, N_META)
    ya_m = ya_m8[0]
    yb_m = _attention_meta(q_m, k_m, v_m[0])
    h1_m = _outproj(ya_m, yb_m, meta, w, 1, 1, N_META)
    meta_halo = _meta_gate(h1_m, w)[N_META - SUBLANES:]

    tm = 512
    nt = seq // tm
    u, q, k, v = _inproj(x2d, cs[N_META:], w, bsz, nt, tm)
    ya, _ = _s5(u.reshape(bsz, seq, d_ssm), h_meta_state, w, 64)
    yb = _attention(q, k, v, k_m, v_m[0], bsz, seq)
    h1 = _outproj(ya.reshape(bsz * seq, d_ssm), yb, x2d, w, bsz, nt, tm)
    out = _ffn(h1, meta_halo, w, nt, tm)
    return out.reshape(bsz, seq, d)
```

```python
import functools
import math

import jax
import jax.numpy as jnp
from jax import lax
from jax.experimental import pallas as pl
from jax.experimental.pallas import tpu as pltpu

F32 = jnp.float32
BF16 = jnp.bfloat16

EPS = 1e-6
CHUNK = 64
N_META = 16
SSM_GROUP = 16
SSM_STATE = 64
HEADS = 8
QK_NOPE = 128
QK_ROPE = 64
V_HEAD = 128
ROPE_BASE = 10000.0
CONV_W = 3

LANES = 128
SUBLANES = 8
MXU_DIM = 256
VMEM_BYTES = 64 * 1024 * 1024

HEAD_W = QK_NOPE + 2 * QK_ROPE
GROUPS_PER_MXU = MXU_DIM // SSM_GROUP
STATE_W = GROUPS_PER_MXU * SSM_STATE

ATT_TQ = 512
FFN_TF = 512
FFN_SUB = 2


_NT = (((1,), (1,)), ((), ()))


def _vmem_limit(nbytes):
    return int(min(nbytes + (8 << 20), VMEM_BYTES - (4 << 20)))


def _rms(x, g):
    x = x.astype(F32)
    return x * lax.rsqrt(jnp.mean(x * x, axis=-1, keepdims=True) + EPS) * g


def _const_spec(shape):
    nd = len(shape)
    return pl.BlockSpec(shape, lambda *_: (0,) * nd, pipeline_mode=pl.Buffered(1))


def _inproj_kernel(x_ref, g_ref, win_ref, qg_ref, wq_ref, kvg_ref, wk_ref, wv_ref, cs_ref,
                   u_ref, q_ref, k_ref, v_ref, *, scale):
    d_ssm = u_ref.shape[1]
    q_lora = wq_ref.shape[0]
    kv_lora = wk_ref.shape[0]
    xn = _rms(x_ref[...], g_ref[...]).astype(BF16)
    z = jnp.dot(xn, win_ref[...], preferred_element_type=F32)
    o1, o2, o3 = d_ssm, d_ssm + q_lora, d_ssm + q_lora + kv_lora
    u_ref[...] = z[:, :o1].astype(u_ref.dtype)

    cs = cs_ref[...]
    qn = _rms(z[:, o1:o2], qg_ref[...]).astype(BF16)
    q = jnp.dot(qn, wq_ref[...], preferred_element_type=F32)
    cs_q = cs * scale
    for h in range(HEADS):
        c0 = h * HEAD_W
        q_ref[:, c0:c0 + QK_NOPE] = (q[:, c0:c0 + QK_NOPE] * scale).astype(q_ref.dtype)
        q_ref[:, c0 + QK_NOPE:c0 + HEAD_W] = (q[:, c0 + QK_NOPE:c0 + HEAD_W] * cs_q).astype(q_ref.dtype)

    kvn = _rms(z[:, o2:o3], kvg_ref[...]).astype(BF16)
    kn = jnp.dot(kvn, wk_ref[...], preferred_element_type=F32)
    v_ref[...] = lax.dot_general(wv_ref[...], kvn, _NT, preferred_element_type=F32).astype(v_ref.dtype)
    t = z[:, o3:o3 + 2 * QK_ROPE] * cs
    krot = (t + pltpu.roll(t, QK_ROPE, 1)).astype(k_ref.dtype)
    for h in range(HEADS):
        c0 = h * HEAD_W
        k_ref[:, c0:c0 + QK_NOPE] = kn[:, h * QK_NOPE:(h + 1) * QK_NOPE].astype(k_ref.dtype)
        k_ref[:, c0 + QK_NOPE:c0 + HEAD_W] = krot


def _inproj(x2d, cs, w, nb, nt, tm):
    rows, d = x2d.shape
    d_ssm = w["d_ssm"]
    n_in = w["win"].shape[1]
    row_spec = lambda width: pl.BlockSpec((tm, width), lambda b, j: (b * nt + j, 0))
    in_specs = [
        row_spec(d),
        _const_spec((1, d)),
        _const_spec(w["win"].shape),
        _const_spec(w["qg"].shape),
        _const_spec(w["wq"].shape),
        _const_spec(w["kvg"].shape),
        _const_spec(w["wk"].shape),
        _const_spec(w["wv"].shape),
        pl.BlockSpec((tm, LANES), lambda b, j: (j, 0)),
    ]
    out_shape = (
        jax.ShapeDtypeStruct((rows, d_ssm), BF16),
        jax.ShapeDtypeStruct((rows, HEADS * HEAD_W), BF16),
        jax.ShapeDtypeStruct((rows, HEADS * HEAD_W), BF16),
        jax.ShapeDtypeStruct((nb, HEADS * V_HEAD, nt * tm), BF16),
    )
    out_specs = (
        row_spec(d_ssm),
        row_spec(HEADS * HEAD_W),
        row_spec(HEADS * HEAD_W),
        pl.BlockSpec((None, HEADS * V_HEAD, tm), lambda b, j: (b, 0, j)),
    )
    weights = 2 * (w["win"].size + w["wq"].size + w["wk"].size + w["wv"].size)
    tiles = 2 * tm * (4 * d + 2 * d_ssm + 4 * HEADS * HEAD_W + 2 * HEADS * V_HEAD + 4 * LANES)
    temps = 4 * tm * (d + n_in + 2 * HEADS * HEAD_W + 2 * HEADS * V_HEAD)
    return pl.pallas_call(
        functools.partial(_inproj_kernel, scale=w["scale"]),
        grid=(nb, nt),
        in_specs=in_specs,
        out_specs=out_specs,
        out_shape=out_shape,
        compiler_params=pltpu.CompilerParams(
            dimension_semantics=("arbitrary", "arbitrary"),
            vmem_limit_bytes=_vmem_limit(weights + tiles + temps)),
        name="inproj",
    )(x2d, w["mix_g"], w["win"], w["qg"], w["wq"], w["kvg"], w["wk"], w["wv"], cs)


def _s5_kernel(u_ref, h0_ref, perm_ref, bm_ref, cm_ref, are_ref, aim_ref, dskip_ref, wglu_ref, bglu_ref,
               y_ref, hT_ref, hbuf, state, *, steps):
    n_gb = bm_ref.shape[0]
    sw = are_ref.shape[2]
    cin = bm_ref.shape[1]
    nb, _, d_ssm = u_ref.shape

    @pl.when(pl.program_id(0) == 0)
    def _():
        state[...] = h0_ref[...]

    u = jnp.dot(perm_ref[0], u_ref[...].reshape(nb * steps, d_ssm),
                preferred_element_type=F32).astype(BF16)
    for gb in range(n_gb):
        hbuf[:, 2 * sw * gb:2 * sw * (gb + 1)] = jnp.dot(
            u[:, cin * gb:cin * (gb + 1)], bm_ref[gb], preferred_element_type=F32)

    for gb in range(n_gb):
        are = are_ref[gb]
        aim = aim_ref[gb]
        c_re = 2 * sw * gb
        c_im = c_re + sw

        hre, him = state[:, c_re:c_re + sw], state[:, c_im:c_im + sw]
        for t in range(steps):
            r0 = t * SUBLANES
            nre = are * hre - aim * him + hbuf[r0:r0 + SUBLANES, c_re:c_re + sw]
            nim = are * him + aim * hre + hbuf[r0:r0 + SUBLANES, c_im:c_im + sw]
            hbuf[r0:r0 + SUBLANES, c_re:c_re + sw] = nre
            hbuf[r0:r0 + SUBLANES, c_im:c_im + sw] = nim
            hre, him = nre, nim
        state[:, c_re:c_re + sw] = hre
        state[:, c_im:c_im + sw] = him

    ys = [jnp.dot(hbuf[:, 2 * sw * gb:2 * sw * (gb + 1)].astype(BF16), cm_ref[gb],
                  preferred_element_type=F32) for gb in range(n_gb)]
    y = jnp.concatenate(ys, axis=1) + dskip_ref[...] * u.astype(F32)
    g = jax.nn.gelu(y)
    gate = jnp.dot(g.astype(BF16), wglu_ref[...], preferred_element_type=F32) + bglu_ref[...]
    out = (g * jax.nn.sigmoid(gate)).astype(BF16)
    y_ref[...] = jnp.dot(perm_ref[1], out, preferred_element_type=F32).astype(y_ref.dtype).reshape(y_ref.shape)
    hT_ref[...] = state[...]


def _s5(u, h0, w, steps):
    nb, t_len, d_ssm = u.shape
    assert nb == SUBLANES
    r = steps * SUBLANES
    n_state = h0.shape[1]
    src = (jnp.arange(r) % SUBLANES) * steps + jnp.arange(r) // SUBLANES
    fwd = (src[:, None] == jnp.arange(r)[None, :]).astype(BF16)
    perm = jnp.stack([fwd, fwd.T])
    blk = pl.BlockSpec((nb, steps, d_ssm), lambda i: (0, i, 0))
    in_specs = [
        blk,
        _const_spec(h0.shape),
        _const_spec(perm.shape),
        _const_spec(w["bm"].shape),
        _const_spec(w["cm"].shape),
        _const_spec(w["are"].shape),
        _const_spec(w["aim"].shape),
        _const_spec((1, d_ssm)),
        _const_spec(w["wglu"].shape),
        _const_spec((1, d_ssm)),
    ]
    out_shape = (jax.ShapeDtypeStruct(u.shape, BF16),
                 jax.ShapeDtypeStruct(h0.shape, F32))
    out_specs = (blk, pl.BlockSpec(h0.shape, lambda i: (0, 0)))
    weights = (2 * (w["bm"].size + w["cm"].size + w["wglu"].size + perm.size)
               + 4 * (w["are"].size + w["aim"].size))
    tiles = 2 * r * d_ssm * 2 * 2 + 3 * 4 * h0.size
    scratch = 4 * r * n_state + 4 * h0.size
    temps = 2 * r * n_state + 6 * 4 * r * d_ssm
    return pl.pallas_call(
        functools.partial(_s5_kernel, steps=steps),
        grid=(t_len // steps,),
        in_specs=in_specs,
        out_specs=out_specs,
        out_shape=out_shape,
        scratch_shapes=[pltpu.VMEM((r, n_state), F32), pltpu.VMEM(h0.shape, F32)],
        compiler_params=pltpu.CompilerParams(
            dimension_semantics=("arbitrary",),
            vmem_limit_bytes=_vmem_limit(weights + tiles + scratch + temps)),
        name="s5",
    )(u, h0, perm, w["bm"], w["cm"], w["are"], w["aim"], w["dskip"], w["wglu"], w["bglu"])


def _attn_kernel(q_ref, k_ref, vt_ref, km_ref, vmt_ref, o_ref):
    seq = q_ref.shape[0]
    half = ATT_TQ // 2
    neg = jnp.finfo(F32).min
    kc = lax.broadcasted_iota(jnp.int32, (half, half), 0) // CHUNK
    qc = lax.broadcasted_iota(jnp.int32, (half, half), 1) // CHUNK
    visible = kc <= qc
    colmax = lambda t: jnp.max(t, axis=0, keepdims=True)
    colsum = lambda t: jnp.sum(t, axis=0, keepdims=True)
    km = km_ref[...]
    vmt = vmt_ref[...]
    for i in range(seq // ATT_TQ):
        r0 = i * ATT_TQ
        q = q_ref[r0:r0 + ATT_TQ, :]
        s_meta = lax.dot_general(km, q, _NT, preferred_element_type=F32)
        s_diag = lax.dot_general(k_ref[r0:r0 + ATT_TQ, :], q, _NT, preferred_element_type=F32)
        s00 = jnp.where(visible, s_diag[:half, :half], neg)
        s01 = s_diag[:half, half:]
        s11 = jnp.where(visible, s_diag[half:, half:], neg)
        m = jnp.concatenate([colmax(s00), jnp.maximum(colmax(s01), colmax(s11))], axis=1)
        m = jnp.maximum(m, colmax(s_meta))
        if i > 0:
            s_full = lax.dot_general(k_ref[0:r0, :], q, _NT, preferred_element_type=F32)
            m = jnp.maximum(m, colmax(s_full))
        p_meta = jnp.exp2(s_meta - m)
        p00 = jnp.exp2(s00 - m[:, :half])
        p01 = jnp.exp2(s01 - m[:, half:])
        p11 = jnp.exp2(s11 - m[:, half:])
        l = colsum(p_meta) + jnp.concatenate([colsum(p00), colsum(p01) + colsum(p11)], axis=1)
        p_diag = jnp.concatenate([jnp.concatenate([p00, p01], axis=1),
                                  jnp.concatenate([jnp.zeros_like(p11), p11], axis=1)], axis=0)
        o = jnp.dot(vmt, p_meta.astype(BF16), preferred_element_type=F32)
        o = o + jnp.dot(vt_ref[:, r0:r0 + ATT_TQ], p_diag.astype(BF16), preferred_element_type=F32)
        if i > 0:
            p_full = jnp.exp2(s_full - m)
            l = l + colsum(p_full)
            o = o + jnp.dot(vt_ref[:, 0:r0], p_full.astype(BF16), preferred_element_type=F32)
        o_ref[r0:r0 + ATT_TQ, :] = (o / l).T.astype(o_ref.dtype)


def _attention(q, k, vt, km, vmt, nb, seq):
    assert ATT_TQ % CHUNK == 0 and seq % ATT_TQ == 0 and V_HEAD == LANES
    n_meta = km.shape[0]
    in_specs = [
        pl.BlockSpec((seq, HEAD_W), lambda b, h: (b, h)),
        pl.BlockSpec((seq, HEAD_W), lambda b, h: (b, h)),
        pl.BlockSpec((None, V_HEAD, seq), lambda b, h: (b, h, 0)),
        pl.BlockSpec((n_meta, HEAD_W), lambda b, h: (0, h)),
        pl.BlockSpec((V_HEAD, n_meta), lambda b, h: (h, 0)),
    ]
    tiles = 2 * 2 * seq * (2 * HEAD_W + 2 * V_HEAD)
    temps = 4 * 4 * seq * ATT_TQ * 2
    return pl.pallas_call(
        _attn_kernel,
        grid=(nb, HEADS),
        in_specs=in_specs,
        out_specs=pl.BlockSpec((seq, V_HEAD), lambda b, h: (b, h)),
        out_shape=jax.ShapeDtypeStruct((nb * seq, HEADS * V_HEAD), BF16),
        compiler_params=pltpu.CompilerParams(
            dimension_semantics=("arbitrary", "arbitrary"),
            vmem_limit_bytes=_vmem_limit(tiles + temps)),
        name="attn",
    )(q, k, vt, km, vmt)


def _attn_meta_kernel(q_ref, k_ref, vt_ref, o_ref):
    for h in range(HEADS):
        s = lax.dot_general(q_ref[:, h * HEAD_W:(h + 1) * HEAD_W],
                            k_ref[:, h * HEAD_W:(h + 1) * HEAD_W], _NT, preferred_element_type=F32)
        p = jnp.exp2(s - jnp.max(s, axis=1, keepdims=True))
        pv = lax.dot_general(p.astype(BF16), vt_ref[h * V_HEAD:(h + 1) * V_HEAD, :], _NT,
                             preferred_element_type=F32)
        o_ref[:, h * V_HEAD:(h + 1) * V_HEAD] = (pv / jnp.sum(p, axis=1, keepdims=True)).astype(o_ref.dtype)


def _attention_meta(q, k, vt):
    return pl.pallas_call(
        _attn_meta_kernel,
        out_shape=jax.ShapeDtypeStruct((q.shape[0], vt.shape[0]), BF16),
        name="attn_meta",
    )(q, k, vt)


def _outproj_kernel(ya_ref, yb_ref, x_ref, ga_ref, gb_ref, wa_ref, wb_ref, o_ref):
    a = _rms(ya_ref[...], ga_ref[...]).astype(BF16)
    b = _rms(yb_ref[...], gb_ref[...]).astype(BF16)
    mix = jnp.dot(a, wa_ref[...], preferred_element_type=F32)
    mix = mix + jnp.dot(b, wb_ref[...], preferred_element_type=F32)
    o_ref[...] = x_ref[...].astype(F32) + mix


def _outproj(ya, yb, x2d, w, nb, nt, tm):
    rows, d = x2d.shape
    d_ssm, d_attn = w["wo_a"].shape[0], w["wo_b"].shape[0]
    in_specs = [
        pl.BlockSpec((tm, d_ssm), lambda b, j: (b * nt + j, 0)),
        pl.BlockSpec((tm, d_attn), lambda b, j: (b * nt + j, 0)),
        pl.BlockSpec((tm, d), lambda b, j: (b * nt + j, 0)),
        _const_spec((1, d_ssm)),
        _const_spec((1, d_attn)),
        _const_spec(w["wo_a"].shape),
        _const_spec(w["wo_b"].shape),
    ]
    weights = 2 * (w["wo_a"].size + w["wo_b"].size)
    tiles = 2 * tm * (2 * d_ssm + 2 * d_attn + 4 * d + 4 * d)
    temps = 4 * tm * (2 * d_ssm + 2 * d_attn + 2 * d)
    return pl.pallas_call(
        _outproj_kernel,
        grid=(nb, nt),
        in_specs=in_specs,
        out_specs=pl.BlockSpec((tm, d), lambda b, j: (b * nt + j, 0)),
        out_shape=jax.ShapeDtypeStruct((rows, d), F32),
        compiler_params=pltpu.CompilerParams(
            dimension_semantics=("arbitrary", "arbitrary"),
            vmem_limit_bytes=_vmem_limit(weights + tiles + temps)),
        name="outproj",
    )(ya, yb, x2d, w["ga"], w["gb"], w["wo_a"], w["wo_b"])


def _gate_kernel(h_ref, g_ref, wg_ref, o_ref):
    xn = _rms(h_ref[...], g_ref[...]).astype(BF16)
    o_ref[...] = jnp.dot(xn, wg_ref[...], preferred_element_type=F32)


def _meta_gate(h_meta, w):
    rows, d = h_meta.shape
    n_chunks = w["conv_w"].shape[0]
    gate = pl.pallas_call(
        _gate_kernel,
        grid=(n_chunks,),
        in_specs=[_const_spec((rows, d)), _const_spec((1, d)),
                  pl.BlockSpec((d, FFN_TF), lambda j: (0, j))],
        out_specs=pl.BlockSpec((rows, FFN_TF), lambda j: (0, j)),
        out_shape=jax.ShapeDtypeStruct((rows, n_chunks * FFN_TF), F32),
        compiler_params=pltpu.CompilerParams(dimension_semantics=("arbitrary",)),
        name="meta_gate",
    )(h_meta, w["ffn_g"], w["wu"])
    halo = gate[rows - SUBLANES:].reshape(SUBLANES, n_chunks, FFN_TF)
    return jnp.swapaxes(halo, 0, 1)


def _shift_rows(x, prev, k):
    rolled = pltpu.roll(x, k, 0)
    head = jnp.concatenate([prev, x[:SUBLANES]], axis=0)[SUBLANES - k:2 * SUBLANES - k]
    return jnp.concatenate([head, rolled[SUBLANES:]], axis=0)


def _ffn_kernel(h_ref, g_ref, wu_hbm, wd_hbm, cw_ref, cb_ref, mh_ref, fg_ref, o_ref,
                xn_scr, halo_scr, wg_buf, wv_buf, wd_buf, sem, *, tiles_per_batch, d_ff):
    i = pl.program_id(0)
    tm = h_ref.shape[0]
    ts = tm // FFN_SUB
    n_full = d_ff // FFN_TF
    rem = d_ff - n_full * FFN_TF
    assert 0 < rem and rem % LANES == 0 and (n_full + 1) % 2 == 1

    def chunk_copies(col0, width, slot):
        return (
            pltpu.make_async_copy(wu_hbm.at[:, pl.ds(col0, width)],
                                  wg_buf.at[slot, :, pl.ds(0, width)], sem.at[slot, 0]),
            pltpu.make_async_copy(wu_hbm.at[:, pl.ds(d_ff + col0, width)],
                                  wv_buf.at[slot, :, pl.ds(0, width)], sem.at[slot, 1]),
            pltpu.make_async_copy(wd_hbm.at[pl.ds(col0, width), :],
                                  wd_buf.at[slot, pl.ds(0, width), :], sem.at[slot, 2]),
        )

    def start(copies):
        for c in copies:
            c.start()

    def wait(copies):
        for c in copies:
            c.wait()

    ragged = lambda slot: chunk_copies(n_full * FFN_TF, rem, slot)
    full = lambda c, slot: chunk_copies(pl.multiple_of(c * FFN_TF, FFN_TF), FFN_TF, slot)

    def compute(idx, width, slot):
        wg = wg_buf[slot, :, :width]
        wv = wv_buf[slot, :, :width]
        wd = wd_buf[slot, :width, :]
        cw = cw_ref[idx][:, :width]
        cb = cb_ref[idx][:, :width]
        prev = halo_scr[idx][:, :width]
        for s in range(FFN_SUB):
            rows = pl.ds(s * ts, ts)
            xn = xn_scr[rows, :]
            gate = jnp.dot(xn, wg, preferred_element_type=F32)
            val = jnp.dot(xn, wv, preferred_element_type=F32)
            conv = (cw[0:1] * _shift_rows(gate, prev, 2)
                    + cw[1:2] * _shift_rows(gate, prev, 1)
                    + cw[2:3] * gate + cb)
            prev = gate[ts - SUBLANES:, :]
            act = (jax.nn.silu(conv) * val).astype(BF16)
            o_ref[rows, :] += jnp.dot(act, wd, preferred_element_type=F32)
        halo_scr[idx, :, pl.ds(0, width)] = prev

    slot_r = i % 2

    @pl.when(i == 0)
    def _():
        start(ragged(slot_r))

    @pl.when((i % tiles_per_batch) == 0)
    def _():
        halo_scr[...] = mh_ref[...]

    h = h_ref[...]
    xn_scr[...] = _rms(h, g_ref[...]).astype(BF16)
    o_ref[...] = h

    wait(ragged(slot_r))
    start(full(0, 1 - slot_r))
    compute(n_full, rem, slot_r)

    def body(c, carry):
        slot = (slot_r + 1 + c) % 2
        wait(full(c, slot))

        @pl.when(c + 1 < n_full)
        def _():
            start(full(c + 1, 1 - slot))

        @pl.when(jnp.logical_and(c + 1 == n_full, i + 1 < pl.num_programs(0)))
        def _():
            start(ragged(1 - slot))

        compute(c, FFN_TF, slot)
        return carry

    lax.fori_loop(0, n_full, body, 0)
    o_ref[...] = _rms(o_ref[...], fg_ref[...])


def _ffn(h1, meta_halo, w, tiles_per_batch, tm):
    rows, d = h1.shape
    d_ff = w["wd"].shape[0]
    n_chunks = w["conv_w"].shape[0]
    in_specs = [
        pl.BlockSpec((tm, d), lambda i: (i, 0)),
        _const_spec((1, d)),
        pl.BlockSpec(memory_space=pl.ANY),
        pl.BlockSpec(memory_space=pl.ANY),
        _const_spec(w["conv_w"].shape),
        _const_spec(w["conv_b"].shape),
        _const_spec(meta_halo.shape),
        _const_spec((1, d)),
    ]
    slots = 2
    tiles = 2 * 2 * 4 * tm * d + 4 * (w["conv_w"].size + w["conv_b"].size + meta_halo.size)
    scratch = 2 * tm * d + 4 * meta_halo.size + slots * 3 * 2 * d * FFN_TF
    temps = 4 * (tm // FFN_SUB) * FFN_TF * 16 + 4 * tm * d
    return pl.pallas_call(
        functools.partial(_ffn_kernel, tiles_per_batch=tiles_per_batch, d_ff=d_ff),
        grid=(rows // tm,),
        in_specs=in_specs,
        out_specs=pl.BlockSpec((tm, d), lambda i: (i, 0)),
        out_shape=jax.ShapeDtypeStruct((rows, d), F32),
        scratch_shapes=[
            pltpu.VMEM((tm, d), BF16),
            pltpu.VMEM((n_chunks, SUBLANES, FFN_TF), F32),
            pltpu.VMEM((slots, d, FFN_TF), BF16),
            pltpu.VMEM((slots, d, FFN_TF), BF16),
            pltpu.VMEM((slots, FFN_TF, d), BF16),
            pltpu.SemaphoreType.DMA((slots, 3)),
        ],
        compiler_params=pltpu.CompilerParams(
            dimension_semantics=("arbitrary",),
            vmem_limit_bytes=_vmem_limit(tiles + scratch + temps)),
        name="ffn",
    )(h1, w["ffn_g"], w["wu"], w["wd"], w["conv_w"], w["conv_b"], meta_halo, w["final_g"])


def _swap_halves(w):
    half = w.shape[-1] // 2
    return jnp.concatenate([-w[..., half:], w[..., :half]], axis=-1)


def _block_diag(blocks):
    n, r, c = blocks.shape
    eye = jnp.eye(n, dtype=blocks.dtype)
    return (blocks[:, :, None, :] * eye[:, None, :, None]).reshape(n * r, n * c)


def _prepare(p):
    d_ssm = p["d_skip"].shape[-1]
    q_lora = p["q_a_norm"].shape[-1]
    kv_lora = p["kv_a_norm"].shape[-1]
    d_ff = p["w_down"].shape[0]
    w = {"d_ssm": d_ssm, "scale": math.log2(math.e) / math.sqrt(QK_NOPE + QK_ROPE)}

    row = lambda v: v.reshape(1, -1).astype(F32)
    w["mix_g"] = row(p["mix_norm"])
    w_in = p["w_in"]
    o3 = d_ssm + q_lora + kv_lora
    w["win"] = jnp.concatenate([w_in, _swap_halves(w_in[:, o3:])], axis=1).astype(BF16)
    w["qg"] = row(p["q_a_norm"])
    w["kvg"] = row(p["kv_a_norm"])
    wq = p["w_q_b"].reshape(q_lora, HEADS, QK_NOPE + QK_ROPE)
    wq_pe = wq[..., QK_NOPE:]
    w["wq"] = jnp.concatenate([wq, _swap_halves(wq_pe)], axis=-1).reshape(q_lora, HEADS * HEAD_W).astype(BF16)
    wkv = p["w_kv_b"].reshape(kv_lora, HEADS, QK_NOPE + V_HEAD)
    w["wk"] = wkv[..., :QK_NOPE].reshape(kv_lora, HEADS * QK_NOPE).astype(BF16)
    w["wv"] = wkv[..., QK_NOPE:].reshape(kv_lora, HEADS * V_HEAD).T.astype(BF16)

    lam_re, lam_im = p["lam_re"].astype(F32), p["lam_im"].astype(F32)
    dt = jnp.exp(p["log_dt"].astype(F32))[:, None]
    mag = jnp.exp(lam_re * dt)
    a_re, a_im = mag * jnp.cos(lam_im * dt), mag * jnp.sin(lam_im * dt)
    den = lam_re * lam_re + lam_im * lam_im
    f_re = ((a_re - 1.0) * lam_re + a_im * lam_im) / den
    f_im = (a_im * lam_re - (a_re - 1.0) * lam_im) / den
    b_re, b_im = p["b_re"].astype(F32), p["b_im"].astype(F32)
    bb_re = f_re[..., None] * b_re - f_im[..., None] * b_im
    bb_im = f_re[..., None] * b_im + f_im[..., None] * b_re
    n_g = lam_re.shape[0]
    n_gb = n_g // GROUPS_PER_MXU
    grp = lambda t: t.reshape(n_gb, GROUPS_PER_MXU, *t.shape[1:])
    bd = jax.vmap(_block_diag)
    w["bm"] = jnp.concatenate([bd(grp(jnp.swapaxes(bb_re, 1, 2))),
                               bd(grp(jnp.swapaxes(bb_im, 1, 2)))], axis=2).astype(BF16)
    c_re, c_im = p["c_re"].astype(F32), p["c_im"].astype(F32)
    w["cm"] = jnp.concatenate([bd(grp(jnp.swapaxes(c_re, 1, 2))),
                               bd(grp(jnp.swapaxes(-c_im, 1, 2)))], axis=1).astype(BF16)
    bcast = lambda a: jnp.broadcast_to(a.reshape(n_gb, 1, STATE_W), (n_gb, SUBLANES, STATE_W))
    w["are"], w["aim"] = bcast(a_re), bcast(a_im)
    w["dskip"] = row(p["d_skip"])
    w["wglu"] = p["w_glu"].astype(BF16)
    w["bglu"] = row(p["b_glu"])

    w["ga"], w["gb"] = row(p["out_norm_ssm"]), row(p["out_norm_attn"])
    w["wo_a"] = p["w_out"][:d_ssm].astype(BF16)
    w["wo_b"] = p["w_out"][d_ssm:].astype(BF16)

    n_chunks = -(-d_ff // FFN_TF)
    pad = n_chunks * FFN_TF - d_ff
    w["wu"] = p["w_up"].astype(BF16)
    w["wd"] = p["w_down"].astype(BF16)
    per_chunk = lambda a: jnp.swapaxes(
        jnp.pad(a.astype(F32), ((0, 0), (0, pad))).reshape(a.shape[0], n_chunks, FFN_TF), 0, 1)
    w["conv_w"] = per_chunk(p["conv_w"])
    w["conv_b"] = per_chunk(p["conv_b"].reshape(1, -1))
    w["ffn_g"] = row(p["ffn_norm"])
    w["final_g"] = row(p["final_norm"])
    return w


def _rope_table(n_pos):
    pos = jnp.arange(n_pos, dtype=F32)
    inv_freq = 1.0 / (ROPE_BASE ** (jnp.arange(0, QK_ROPE, 2, dtype=F32) / QK_ROPE))
    ang = pos[:, None] * inv_freq[None, :]
    cos, sin = jnp.cos(ang), jnp.sin(ang)
    return jnp.concatenate([cos, cos, sin, sin], axis=1)


def kernel(x, meta_tokens, mix_norm, w_in, lam_re, lam_im, log_dt, b_re, b_im, c_re, c_im, d_skip, w_glu, b_glu, q_a_norm, w_q_b, kv_a_norm, w_kv_b, out_norm_ssm, out_norm_attn, w_out, ffn_norm, w_up, conv_w, conv_b, w_down, final_norm):
    bsz, seq, d = x.shape
    assert meta_tokens.shape == (N_META, d) and bsz == SUBLANES
    p = dict(mix_norm=mix_norm[0], w_in=w_in[0], lam_re=lam_re[0], lam_im=lam_im[0], log_dt=log_dt[0],
             b_re=b_re[0], b_im=b_im[0], c_re=c_re[0], c_im=c_im[0], d_skip=d_skip[0],
             w_glu=w_glu[0], b_glu=b_glu[0], q_a_norm=q_a_norm[0], w_q_b=w_q_b[0],
             kv_a_norm=kv_a_norm[0], w_kv_b=w_kv_b[0], out_norm_ssm=out_norm_ssm[0],
             out_norm_attn=out_norm_attn[0], w_out=w_out[0], ffn_norm=ffn_norm[0], w_up=w_up[0],
             conv_w=conv_w[0], conv_b=conv_b[0], w_down=w_down[0], final_norm=final_norm)
    w = _prepare(p)
    d_ssm = w["d_ssm"]
    cs = _rope_table(N_META + seq)
    x2d = x.reshape(bsz * seq, d)
    meta = meta_tokens.astype(x.dtype)

    u_m, q_m, k_m, v_m = _inproj(meta, cs[:N_META], w, 1, 1, N_META)
    n_state = 2 * STATE_W * w["bm"].shape[0]
    ya_m8, h_meta_state = _s5(jnp.broadcast_to(u_m[None], (SUBLANES, N_META, d_ssm)),
                              jnp.zeros((SUBLANES, n_state), F32), w, N_META)
    ya_m = ya_m8[0]
    yb_m = _attention_meta(q_m, k_m, v_m[0])
    h1_m = _outproj(ya_m, yb_m, meta, w, 1, 1, N_META)
    meta_halo = _meta_gate(h1_m, w)

    tm = 512
    nt = seq // tm
    u, q, k, v = _inproj(x2d, cs[N_META:], w, bsz, nt, tm)
    ya, _ = _s5(u.reshape(bsz, seq, d_ssm), h_meta_state, w, 64)
    yb = _attention(q, k, v, k_m, v_m[0], bsz, seq)
    h1 = _outproj(ya.reshape(bsz * seq, d_ssm), yb, x2d, w, bsz, nt, tm)
    out = _ffn(h1, meta_halo, w, nt, tm)
    return out.reshape(bsz, seq, d)
```

```python
import functools
import math

import jax
import jax.numpy as jnp
from jax import lax
from jax.experimental import pallas as pl
from jax.experimental.pallas import tpu as pltpu

F32 = jnp.float32
BF16 = jnp.bfloat16

EPS = 1e-6
CHUNK = 64
N_META = 16
SSM_GROUP = 16
SSM_STATE = 64
HEADS = 8
QK_NOPE = 128
QK_ROPE = 64
V_HEAD = 128
ROPE_BASE = 10000.0
CONV_W = 3

LANES = 128
SUBLANES = 8
MXU_DIM = 256
VMEM_BYTES = 64 * 1024 * 1024

HEAD_W = QK_NOPE + 2 * QK_ROPE
GROUPS_PER_MXU = MXU_DIM // SSM_GROUP
STATE_W = GROUPS_PER_MXU * SSM_STATE

ATT_TQ = 512
FFN_TF = 1024
FFN_SUB = 2


_NT = (((1,), (1,)), ((), ()))


def _vmem_limit(nbytes):
    return int(min(nbytes + (8 << 20), VMEM_BYTES - (4 << 20)))


def _rms(x, g):
    x = x.astype(F32)
    return x * lax.rsqrt(jnp.mean(x * x, axis=-1, keepdims=True) + EPS) * g


def _const_spec(shape):
    nd = len(shape)
    return pl.BlockSpec(shape, lambda *_: (0,) * nd, pipeline_mode=pl.Buffered(1))


def _inproj_kernel(x_ref, g_ref, win_ref, wpe_ref, qg_ref, wq_ref, kvg_ref, wk_ref, wv_ref, cs_ref,
                   u_ref, q_ref, k_ref, v_ref, *, scale):
    d_ssm = u_ref.shape[1]
    q_lora = wq_ref.shape[0]
    kv_lora = wk_ref.shape[0]
    xn = _rms(x_ref[...], g_ref[...]).astype(BF16)
    z = jnp.dot(xn, win_ref[...], preferred_element_type=F32)
    o1, o2, o3 = d_ssm, d_ssm + q_lora, d_ssm + q_lora + kv_lora
    u_ref[...] = z[:, :o1].astype(u_ref.dtype)

    cs = cs_ref[...]
    qn = _rms(z[:, o1:o2], qg_ref[...]).astype(BF16)
    q = jnp.dot(qn, wq_ref[...], preferred_element_type=F32)
    cs_q = cs * scale
    for h in range(HEADS):
        c0 = h * HEAD_W
        q_ref[:, c0:c0 + QK_NOPE] = (q[:, c0:c0 + QK_NOPE] * scale).astype(q_ref.dtype)
        q_ref[:, c0 + QK_NOPE:c0 + HEAD_W] = (q[:, c0 + QK_NOPE:c0 + HEAD_W] * cs_q).astype(q_ref.dtype)

    kvn = _rms(z[:, o2:o3], kvg_ref[...]).astype(BF16)
    kn = jnp.dot(kvn, wk_ref[...], preferred_element_type=F32)
    v_ref[...] = lax.dot_general(wv_ref[...], kvn, _NT, preferred_element_type=F32).astype(v_ref.dtype)
    t = jnp.dot(xn, wpe_ref[...], preferred_element_type=F32) * cs
    krot = (t + pltpu.roll(t, QK_ROPE, 1)).astype(k_ref.dtype)
    for h in range(HEADS):
        c0 = h * HEAD_W
        k_ref[:, c0:c0 + QK_NOPE] = kn[:, h * QK_NOPE:(h + 1) * QK_NOPE].astype(k_ref.dtype)
        k_ref[:, c0 + QK_NOPE:c0 + HEAD_W] = krot


def _inproj(x2d, cs, w, nb, nt, tm):
    rows, d = x2d.shape
    d_ssm = w["d_ssm"]
    n_in = w["win"].shape[1]
    row_spec = lambda width: pl.BlockSpec((tm, width), lambda b, j: (b * nt + j, 0))
    in_specs = [
        row_spec(d),
        _const_spec((1, d)),
        _const_spec(w["win"].shape),
        _const_spec(w["wpe"].shape),
        _const_spec(w["qg"].shape),
        _const_spec(w["wq"].shape),
        _const_spec(w["kvg"].shape),
        _const_spec(w["wk"].shape),
        _const_spec(w["wv"].shape),
        pl.BlockSpec((tm, LANES), lambda b, j: (j, 0)),
    ]
    out_shape = (
        jax.ShapeDtypeStruct((rows, d_ssm), BF16),
        jax.ShapeDtypeStruct((rows, HEADS * HEAD_W), BF16),
        jax.ShapeDtypeStruct((rows, HEADS * HEAD_W), BF16),
        jax.ShapeDtypeStruct((nb, HEADS * V_HEAD, nt * tm), BF16),
    )
    out_specs = (
        row_spec(d_ssm),
        row_spec(HEADS * HEAD_W),
        row_spec(HEADS * HEAD_W),
        pl.BlockSpec((None, HEADS * V_HEAD, tm), lambda b, j: (b, 0, j)),
    )
    weights = 2 * (w["win"].size + w["wpe"].size + w["wq"].size + w["wk"].size + w["wv"].size)
    tiles = 2 * tm * (4 * d + 2 * d_ssm + 4 * HEADS * HEAD_W + 2 * HEADS * V_HEAD + 4 * LANES)
    temps = 4 * tm * (d + n_in + 2 * HEADS * HEAD_W + 2 * HEADS * V_HEAD)
    return pl.pallas_call(
        functools.partial(_inproj_kernel, scale=w["scale"]),
        grid=(nb, nt),
        in_specs=in_specs,
        out_specs=out_specs,
        out_shape=out_shape,
        compiler_params=pltpu.CompilerParams(
            dimension_semantics=("arbitrary", "arbitrary"),
            vmem_limit_bytes=_vmem_limit(weights + tiles + temps)),
        name="inproj",
    )(x2d, w["mix_g"], w["win"], w["wpe"], w["qg"], w["wq"], w["kvg"], w["wk"], w["wv"], cs)


def _s5_kernel(u_ref, h0_ref, perm_ref, bm_ref, cm_ref, are_ref, aim_ref, dskip_ref, wglu_ref, bglu_ref,
               y_ref, hT_ref, hbuf, state, *, steps):
    n_gb = bm_ref.shape[0]
    sw = are_ref.shape[2]
    cin = bm_ref.shape[1]
    nb, _, d_ssm = u_ref.shape

    @pl.when(pl.program_id(0) == 0)
    def _():
        state[...] = h0_ref[...]

    u = jnp.dot(perm_ref[0], u_ref[...].reshape(nb * steps, d_ssm),
                preferred_element_type=F32).astype(BF16)
    for gb in range(n_gb):
        hbuf[:, 2 * sw * gb:2 * sw * (gb + 1)] = jnp.dot(
            u[:, cin * gb:cin * (gb + 1)], bm_ref[gb], preferred_element_type=F32)

    for gb in range(n_gb):
        are = are_ref[gb]
        aim = aim_ref[gb]
        c_re = 2 * sw * gb
        c_im = c_re + sw

        hre, him = state[:, c_re:c_re + sw], state[:, c_im:c_im + sw]
        for t in range(steps):
            r0 = t * SUBLANES
            nre = are * hre - aim * him + hbuf[r0:r0 + SUBLANES, c_re:c_re + sw]
            nim = are * him + aim * hre + hbuf[r0:r0 + SUBLANES, c_im:c_im + sw]
            hbuf[r0:r0 + SUBLANES, c_re:c_re + sw] = nre
            hbuf[r0:r0 + SUBLANES, c_im:c_im + sw] = nim
            hre, him = nre, nim
        state[:, c_re:c_re + sw] = hre
        state[:, c_im:c_im + sw] = him

    ys = [jnp.dot(hbuf[:, 2 * sw * gb:2 * sw * (gb + 1)].astype(BF16), cm_ref[gb],
                  preferred_element_type=F32) for gb in range(n_gb)]
    y = jnp.concatenate(ys, axis=1) + dskip_ref[...] * u.astype(F32)
    g = jax.nn.gelu(y)
    gate = jnp.dot(g.astype(BF16), wglu_ref[...], preferred_element_type=F32) + bglu_ref[...]
    out = (g * jax.nn.sigmoid(gate)).astype(BF16)
    y_ref[...] = jnp.dot(perm_ref[1], out, preferred_element_type=F32).astype(y_ref.dtype).reshape(y_ref.shape)
    hT_ref[...] = state[...]


def _s5(u, h0, w, steps):
    nb, t_len, d_ssm = u.shape
    assert nb == SUBLANES
    r = steps * SUBLANES
    n_state = h0.shape[1]
    src = (jnp.arange(r) % SUBLANES) * steps + jnp.arange(r) // SUBLANES
    fwd = (src[:, None] == jnp.arange(r)[None, :]).astype(BF16)
    perm = jnp.stack([fwd, fwd.T])
    blk = pl.BlockSpec((nb, steps, d_ssm), lambda i: (0, i, 0))
    in_specs = [
        blk,
        _const_spec(h0.shape),
        _const_spec(perm.shape),
        _const_spec(w["bm"].shape),
        _const_spec(w["cm"].shape),
        _const_spec(w["are"].shape),
        _const_spec(w["aim"].shape),
        _const_spec((1, d_ssm)),
        _const_spec(w["wglu"].shape),
        _const_spec((1, d_ssm)),
    ]
    out_shape = (jax.ShapeDtypeStruct(u.shape, BF16),
                 jax.ShapeDtypeStruct(h0.shape, F32))
    out_specs = (blk, pl.BlockSpec(h0.shape, lambda i: (0, 0)))
    weights = (2 * (w["bm"].size + w["cm"].size + w["wglu"].size + perm.size)
               + 4 * (w["are"].size + w["aim"].size))
    tiles = 2 * r * d_ssm * 2 * 2 + 3 * 4 * h0.size
    scratch = 4 * r * n_state + 4 * h0.size
    temps = 2 * r * n_state + 6 * 4 * r * d_ssm
    return pl.pallas_call(
        functools.partial(_s5_kernel, steps=steps),
        grid=(t_len // steps,),
        in_specs=in_specs,
        out_specs=out_specs,
        out_shape=out_shape,
        scratch_shapes=[pltpu.VMEM((r, n_state), F32), pltpu.VMEM(h0.shape, F32)],
        compiler_params=pltpu.CompilerParams(
            dimension_semantics=("arbitrary",),
            vmem_limit_bytes=_vmem_limit(weights + tiles + scratch + temps)),
        name="s5",
    )(u, h0, perm, w["bm"], w["cm"], w["are"], w["aim"], w["dskip"], w["wglu"], w["bglu"])


def _attn_kernel(q_ref, k_ref, vt_ref, km_ref, vmt_ref, o_ref):
    seq = q_ref.shape[0]
    half = ATT_TQ // 2
    neg = jnp.finfo(F32).min
    kc = lax.broadcasted_iota(jnp.int32, (half, half), 0) // CHUNK
    qc = lax.broadcasted_iota(jnp.int32, (half, half), 1) // CHUNK
    visible = kc <= qc
    colmax = lambda t: jnp.max(t, axis=0, keepdims=True)
    colsum = lambda t: jnp.sum(t, axis=0, keepdims=True)
    km = km_ref[...]
    vmt = vmt_ref[...]
    for i in range(seq // ATT_TQ):
        r0 = i * ATT_TQ
        q = q_ref[r0:r0 + ATT_TQ, :]
        s_meta = lax.dot_general(km, q, _NT, preferred_element_type=F32)
        s_diag = lax.dot_general(k_ref[r0:r0 + ATT_TQ, :], q, _NT, preferred_element_type=F32)
        s00 = jnp.where(visible, s_diag[:half, :half], neg)
        s01 = s_diag[:half, half:]
        s11 = jnp.where(visible, s_diag[half:, half:], neg)
        m = jnp.concatenate([colmax(s00), jnp.maximum(colmax(s01), colmax(s11))], axis=1)
        m = jnp.maximum(m, colmax(s_meta))
        if i > 0:
            s_full = lax.dot_general(k_ref[0:r0, :], q, _NT, preferred_element_type=F32)
            m = jnp.maximum(m, colmax(s_full))
        p_meta = jnp.exp2(s_meta - m)
        p00 = jnp.exp2(s00 - m[:, :half])
        p01 = jnp.exp2(s01 - m[:, half:])
        p11 = jnp.exp2(s11 - m[:, half:])
        l = colsum(p_meta) + jnp.concatenate([colsum(p00), colsum(p01) + colsum(p11)], axis=1)
        p_diag = jnp.concatenate([jnp.concatenate([p00, p01], axis=1),
                                  jnp.concatenate([jnp.zeros_like(p11), p11], axis=1)], axis=0)
        o = jnp.dot(vmt, p_meta.astype(BF16), preferred_element_type=F32)
        o = o + jnp.dot(vt_ref[:, r0:r0 + ATT_TQ], p_diag.astype(BF16), preferred_element_type=F32)
        if i > 0:
            p_full = jnp.exp2(s_full - m)
            l = l + colsum(p_full)
            o = o + jnp.dot(vt_ref[:, 0:r0], p_full.astype(BF16), preferred_element_type=F32)
        o_ref[r0:r0 + ATT_TQ, :] = (o / l).T.astype(o_ref.dtype)


def _attention(q, k, vt, km, vmt, nb, seq):
    assert ATT_TQ % CHUNK == 0 and seq % ATT_TQ == 0 and V_HEAD == LANES
    n_meta = km.shape[0]
    in_specs = [
        pl.BlockSpec((seq, HEAD_W), lambda b, h: (b, h)),
        pl.BlockSpec((seq, HEAD_W), lambda b, h: (b, h)),
        pl.BlockSpec((None, V_HEAD, seq), lambda b, h: (b, h, 0)),
        pl.BlockSpec((n_meta, HEAD_W), lambda b, h: (0, h)),
        pl.BlockSpec((V_HEAD, n_meta), lambda b, h: (h, 0)),
    ]
    tiles = 2 * 2 * seq * (2 * HEAD_W + 2 * V_HEAD)
    temps = 4 * 4 * seq * ATT_TQ * 2
    return pl.pallas_call(
        _attn_kernel,
        grid=(nb, HEADS),
        in_specs=in_specs,
        out_specs=pl.BlockSpec((seq, V_HEAD), lambda b, h: (b, h)),
        out_shape=jax.ShapeDtypeStruct((nb * seq, HEADS * V_HEAD), BF16),
        compiler_params=pltpu.CompilerParams(
            dimension_semantics=("arbitrary", "arbitrary"),
            vmem_limit_bytes=_vmem_limit(tiles + temps)),
        name="attn",
    )(q, k, vt, km, vmt)


def _attn_meta_kernel(q_ref, k_ref, vt_ref, o_ref):
    for h in range(HEADS):
        s = lax.dot_general(q_ref[:, h * HEAD_W:(h + 1) * HEAD_W],
                            k_ref[:, h * HEAD_W:(h + 1) * HEAD_W], _NT, preferred_element_type=F32)
        p = jnp.exp2(s - jnp.max(s, axis=1, keepdims=True))
        pv = lax.dot_general(p.astype(BF16), vt_ref[h * V_HEAD:(h + 1) * V_HEAD, :], _NT,
                             preferred_element_type=F32)
        o_ref[:, h * V_HEAD:(h + 1) * V_HEAD] = (pv / jnp.sum(p, axis=1, keepdims=True)).astype(o_ref.dtype)


def _attention_meta(q, k, vt):
    return pl.pallas_call(
        _attn_meta_kernel,
        out_shape=jax.ShapeDtypeStruct((q.shape[0], vt.shape[0]), BF16),
        name="attn_meta",
    )(q, k, vt)


def _outproj_kernel(ya_ref, yb_ref, x_ref, ga_ref, gb_ref, wa_ref, wb_ref, o_ref):
    a = _rms(ya_ref[...], ga_ref[...]).astype(BF16)
    b = _rms(yb_ref[...], gb_ref[...]).astype(BF16)
    mix = jnp.dot(a, wa_ref[...], preferred_element_type=F32)
    mix = mix + jnp.dot(b, wb_ref[...], preferred_element_type=F32)
    o_ref[...] = x_ref[...].astype(F32) + mix


def _outproj(ya, yb, x2d, w, nb, nt, tm):
    rows, d = x2d.shape
    d_ssm, d_attn = w["wo_a"].shape[0], w["wo_b"].shape[0]
    in_specs = [
        pl.BlockSpec((tm, d_ssm), lambda b, j: (b * nt + j, 0)),
        pl.BlockSpec((tm, d_attn), lambda b, j: (b * nt + j, 0)),
        pl.BlockSpec((tm, d), lambda b, j: (b * nt + j, 0)),
        _const_spec((1, d_ssm)),
        _const_spec((1, d_attn)),
        _const_spec(w["wo_a"].shape),
        _const_spec(w["wo_b"].shape),
    ]
    weights = 2 * (w["wo_a"].size + w["wo_b"].size)
    tiles = 2 * tm * (2 * d_ssm + 2 * d_attn + 4 * d + 4 * d)
    temps = 4 * tm * (2 * d_ssm + 2 * d_attn + 2 * d)
    return pl.pallas_call(
        _outproj_kernel,
        grid=(nb, nt),
        in_specs=in_specs,
        out_specs=pl.BlockSpec((tm, d), lambda b, j: (b * nt + j, 0)),
        out_shape=jax.ShapeDtypeStruct((rows, d), F32),
        compiler_params=pltpu.CompilerParams(
            dimension_semantics=("arbitrary", "arbitrary"),
            vmem_limit_bytes=_vmem_limit(weights + tiles + temps)),
        name="outproj",
    )(ya, yb, x2d, w["ga"], w["gb"], w["wo_a"], w["wo_b"])


def _gate_kernel(h_ref, g_ref, wg_ref, o_ref):
    xn = _rms(h_ref[...], g_ref[...]).astype(BF16)
    o_ref[...] = jnp.dot(xn, wg_ref[...], preferred_element_type=F32)


def _meta_gate(h_meta, w):
    rows, d = h_meta.shape
    n_chunks = w["conv_w"].shape[0]
    gate = pl.pallas_call(
        _gate_kernel,
        grid=(n_chunks,),
        in_specs=[_const_spec((rows, d)), _const_spec((1, d)),
                  pl.BlockSpec((d, FFN_TF), lambda j: (0, j))],
        out_specs=pl.BlockSpec((rows, FFN_TF), lambda j: (0, j)),
        out_shape=jax.ShapeDtypeStruct((rows, n_chunks * FFN_TF), F32),
        compiler_params=pltpu.CompilerParams(dimension_semantics=("arbitrary",)),
        name="meta_gate",
    )(h_meta, w["ffn_g"], w["wu"])
    halo = gate[rows - SUBLANES:].reshape(SUBLANES, n_chunks, FFN_TF)
    return jnp.swapaxes(halo, 0, 1)


def _shift_rows(x, prev, k):
    rolled = pltpu.roll(x, k, 0)
    head = jnp.concatenate([prev, x[:SUBLANES]], axis=0)[SUBLANES - k:2 * SUBLANES - k]
    return jnp.concatenate([head, rolled[SUBLANES:]], axis=0)


def _ffn_kernel(h_ref, g_ref, wu_hbm, wd_hbm, cw_ref, cb_ref, mh_ref, fg_ref, o_ref,
                xn_scr, halo_scr, wg_buf, wv_buf, wd_buf, sem, *, tiles_per_batch, d_ff):
    i = pl.program_id(0)
    tm = h_ref.shape[0]
    ts = tm // FFN_SUB
    n_full = d_ff // FFN_TF
    rem = d_ff - n_full * FFN_TF
    assert 0 < rem and rem % LANES == 0

    def chunk_copies(col0, width, slot):
        return (
            pltpu.make_async_copy(wu_hbm.at[:, pl.ds(col0, width)],
                                  wg_buf.at[slot, :, pl.ds(0, width)], sem.at[slot, 0]),
            pltpu.make_async_copy(wu_hbm.at[:, pl.ds(d_ff + col0, width)],
                                  wv_buf.at[slot, :, pl.ds(0, width)], sem.at[slot, 1]),
            pltpu.make_async_copy(wd_hbm.at[pl.ds(col0, width), :],
                                  wd_buf.at[slot, pl.ds(0, width), :], sem.at[slot, 2]),
        )

    def start(copies):
        for c in copies:
            c.start()

    def wait(copies):
        for c in copies:
            c.wait()

    ragged = lambda slot: chunk_copies(n_full * FFN_TF, rem, slot)
    full = lambda c, slot: chunk_copies(pl.multiple_of(c * FFN_TF, FFN_TF), FFN_TF, slot)

    def compute(idx, width, slot):
        wg = wg_buf[slot, :, :width]
        wv = wv_buf[slot, :, :width]
        wd = wd_buf[slot, :width, :]
        cw = cw_ref[idx][:, :width]
        cb = cb_ref[idx][:, :width]
        prev = halo_scr[idx][:, :width]
        for s in range(FFN_SUB):
            rows = pl.ds(s * ts, ts)
            xn = xn_scr[rows, :]
            gate = jnp.dot(xn, wg, preferred_element_type=F32)
            val = jnp.dot(xn, wv, preferred_element_type=F32)
            conv = (cw[0:1] * _shift_rows(gate, prev, 2)
                    + cw[1:2] * _shift_rows(gate, prev, 1)
                    + cw[2:3] * gate + cb)
            prev = gate[ts - SUBLANES:, :]
            act = (jax.nn.silu(conv) * val).astype(BF16)
            o_ref[rows, :] += jnp.dot(act, wd, preferred_element_type=F32)
        halo_scr[idx, :, pl.ds(0, width)] = prev

    slot_r = (i * (n_full + 1)) % 2

    @pl.when(i == 0)
    def _():
        start(ragged(slot_r))

    @pl.when((i % tiles_per_batch) == 0)
    def _():
        halo_scr[...] = mh_ref[...]

    wait(ragged(slot_r))
    start(full(0, 1 - slot_r))
    h = h_ref[...]
    xn_scr[...] = _rms(h, g_ref[...]).astype(BF16)
    o_ref[...] = h
    compute(n_full, rem, slot_r)

    def body(c, carry):
        slot = (slot_r + 1 + c) % 2
        wait(full(c, slot))

        @pl.when(c + 1 < n_full)
        def _():
            start(full(c + 1, 1 - slot))

        @pl.when(jnp.logical_and(c + 1 == n_full, i + 1 < pl.num_programs(0)))
        def _():
            start(ragged(1 - slot))

        compute(c, FFN_TF, slot)
        return carry

    lax.fori_loop(0, n_full, body, 0)
    o_ref[...] = _rms(o_ref[...], fg_ref[...])


def _ffn(h1, meta_halo, w, tiles_per_batch, tm):
    rows, d = h1.shape
    d_ff = w["wd"].shape[0]
    n_chunks = w["conv_w"].shape[0]
    in_specs = [
        pl.BlockSpec((tm, d), lambda i: (i, 0)),
        _const_spec((1, d)),
        pl.BlockSpec(memory_space=pl.ANY),
        pl.BlockSpec(memory_space=pl.ANY),
        _const_spec(w["conv_w"].shape),
        _const_spec(w["conv_b"].shape),
        _const_spec(meta_halo.shape),
        _const_spec((1, d)),
    ]
    slots = 2
    tiles = 2 * 2 * 4 * tm * d + 4 * (w["conv_w"].size + w["conv_b"].size + meta_halo.size)
    scratch = 2 * tm * d + 4 * meta_halo.size + slots * 3 * 2 * d * FFN_TF
    temps = 4 * (tm // FFN_SUB) * FFN_TF * 16 + 4 * tm * d
    return pl.pallas_call(
        functools.partial(_ffn_kernel, tiles_per_batch=tiles_per_batch, d_ff=d_ff),
        grid=(rows // tm,),
        in_specs=in_specs,
        out_specs=pl.BlockSpec((tm, d), lambda i: (i, 0)),
        out_shape=jax.ShapeDtypeStruct((rows, d), F32),
        scratch_shapes=[
            pltpu.VMEM((tm, d), BF16),
            pltpu.VMEM((n_chunks, SUBLANES, FFN_TF), F32),
            pltpu.VMEM((slots, d, FFN_TF), BF16),
            pltpu.VMEM((slots, d, FFN_TF), BF16),
            pltpu.VMEM((slots, FFN_TF, d), BF16),
            pltpu.SemaphoreType.DMA((slots, 3)),
        ],
        compiler_params=pltpu.CompilerParams(
            dimension_semantics=("arbitrary",),
            vmem_limit_bytes=_vmem_limit(tiles + scratch + temps)),
        name="ffn",
    )(h1, w["ffn_g"], w["wu"], w["wd"], w["conv_w"], w["conv_b"], meta_halo, w["final_g"])


def _swap_halves(w):
    half = w.shape[-1] // 2
    return jnp.concatenate([-w[..., half:], w[..., :half]], axis=-1)


def _block_diag(blocks, n):
    rows, c = blocks.shape[-2:]
    r = rows // n
    rep = (jnp.arange(c)[:, None] == jnp.arange(n * c)[None, :] % c).astype(blocks.dtype)
    tiled = jnp.einsum("...rc,cq->...rq", blocks, rep, precision=lax.Precision.HIGHEST)
    keep = (jnp.arange(rows)[:, None] // r) == (jnp.arange(n * c)[None, :] // c)
    return jnp.where(keep, tiled, 0.0)


def _prepare(p):
    d_ssm = p["d_skip"].shape[-1]
    q_lora = p["q_a_norm"].shape[-1]
    kv_lora = p["kv_a_norm"].shape[-1]
    d_ff = p["w_down"].shape[0]
    w = {"d_ssm": d_ssm, "scale": math.log2(math.e) / math.sqrt(QK_NOPE + QK_ROPE)}

    row = lambda v: v.reshape(1, -1).astype(F32)
    w["mix_g"] = row(p["mix_norm"])
    w_in = p["w_in"]
    o3 = d_ssm + q_lora + kv_lora
    w["win"] = w_in[:, :o3].astype(BF16)
    w["wpe"] = jnp.concatenate([w_in[:, o3:], _swap_halves(w_in[:, o3:])], axis=1).astype(BF16)
    w["qg"] = row(p["q_a_norm"])
    w["kvg"] = row(p["kv_a_norm"])
    wq = p["w_q_b"].reshape(q_lora, HEADS, QK_NOPE + QK_ROPE)
    wq_pe = wq[..., QK_NOPE:]
    w["wq"] = jnp.concatenate([wq, _swap_halves(wq_pe)], axis=-1).reshape(q_lora, HEADS * HEAD_W).astype(BF16)
    wkv = p["w_kv_b"].reshape(kv_lora, HEADS, QK_NOPE + V_HEAD)
    w["wk"] = wkv[..., :QK_NOPE].reshape(kv_lora, HEADS * QK_NOPE).astype(BF16)
    w["wv"] = wkv[..., QK_NOPE:].reshape(kv_lora, HEADS * V_HEAD).T.astype(BF16)

    lam_re, lam_im = p["lam_re"].astype(F32), p["lam_im"].astype(F32)
    dt = jnp.exp(p["log_dt"].astype(F32))[:, None]
    mag = jnp.exp(lam_re * dt)
    a_re, a_im = mag * jnp.cos(lam_im * dt), mag * jnp.sin(lam_im * dt)
    den = lam_re * lam_re + lam_im * lam_im
    f_re = ((a_re - 1.0) * lam_re + a_im * lam_im) / den
    f_im = (a_im * lam_re - (a_re - 1.0) * lam_im) / den
    b_re, b_im = p["b_re"].astype(F32), p["b_im"].astype(F32)
    bb_re = f_re[..., None] * b_re - f_im[..., None] * b_im
    bb_im = f_re[..., None] * b_im + f_im[..., None] * b_re
    n_g = lam_re.shape[0]
    n_gb = n_g // GROUPS_PER_MXU
    bd = lambda t: _block_diag(t.reshape(n_gb, GROUPS_PER_MXU * t.shape[1], t.shape[2]), GROUPS_PER_MXU)
    w["bm"] = jnp.concatenate([bd(jnp.swapaxes(bb_re, 1, 2)),
                               bd(jnp.swapaxes(bb_im, 1, 2))], axis=2).astype(BF16)
    c_re, c_im = p["c_re"].astype(F32), p["c_im"].astype(F32)
    w["cm"] = jnp.concatenate([bd(jnp.swapaxes(c_re, 1, 2)),
                               bd(jnp.swapaxes(-c_im, 1, 2))], axis=1).astype(BF16)
    bcast = lambda a: jnp.broadcast_to(a.reshape(n_gb, 1, STATE_W), (n_gb, SUBLANES, STATE_W))
    w["are"], w["aim"] = bcast(a_re), bcast(a_im)
    w["dskip"] = row(p["d_skip"])
    w["wglu"] = p["w_glu"].astype(BF16)
    w["bglu"] = row(p["b_glu"])

    w["ga"], w["gb"] = row(p["out_norm_ssm"]), row(p["out_norm_attn"])
    w["wo_a"] = p["w_out"][:d_ssm].astype(BF16)
    w["wo_b"] = p["w_out"][d_ssm:].astype(BF16)

    n_chunks = -(-d_ff // FFN_TF)
    pad = n_chunks * FFN_TF - d_ff
    w["wu"] = p["w_up"].astype(BF16)
    w["wd"] = p["w_down"].astype(BF16)
    per_chunk = lambda a: jnp.swapaxes(
        jnp.pad(a.astype(F32), ((0, 0), (0, pad))).reshape(a.shape[0], n_chunks, FFN_TF), 0, 1)
    w["conv_w"] = per_chunk(p["conv_w"])
    w["conv_b"] = per_chunk(p["conv_b"].reshape(1, -1))
    w["ffn_g"] = row(p["ffn_norm"])
    w["final_g"] = row(p["final_norm"])
    return w


def _rope_table(n_pos):
    pos = jnp.arange(n_pos, dtype=F32)
    inv_freq = 1.0 / (ROPE_BASE ** (jnp.arange(0, QK_ROPE, 2, dtype=F32) / QK_ROPE))
    ang = pos[:, None] * inv_freq[None, :]
    cos, sin = jnp.cos(ang), jnp.sin(ang)
    return jnp.concatenate([cos, cos, sin, sin], axis=1)


def kernel(x, meta_tokens, mix_norm, w_in, lam_re, lam_im, log_dt, b_re, b_im, c_re, c_im, d_skip, w_glu, b_glu, q_a_norm, w_q_b, kv_a_norm, w_kv_b, out_norm_ssm, out_norm_attn, w_out, ffn_norm, w_up, conv_w, conv_b, w_down, final_norm):
    bsz, seq, d = x.shape
    assert meta_tokens.shape == (N_META, d) and bsz == SUBLANES
    p = dict(mix_norm=mix_norm[0], w_in=w_in[0], lam_re=lam_re[0], lam_im=lam_im[0], log_dt=log_dt[0],
             b_re=b_re[0], b_im=b_im[0], c_re=c_re[0], c_im=c_im[0], d_skip=d_skip[0],
             w_glu=w_glu[0], b_glu=b_glu[0], q_a_norm=q_a_norm[0], w_q_b=w_q_b[0],
             kv_a_norm=kv_a_norm[0], w_kv_b=w_kv_b[0], out_norm_ssm=out_norm_ssm[0],
             out_norm_attn=out_norm_attn[0], w_out=w_out[0], ffn_norm=ffn_norm[0], w_up=w_up[0],
             conv_w=conv_w[0], conv_b=conv_b[0], w_down=w_down[0], final_norm=final_norm)
    w = _prepare(p)
    d_ssm = w["d_ssm"]
    cs = _rope_table(N_META + seq)
    x2d = x.reshape(bsz * seq, d)
    meta = meta_tokens.astype(x.dtype)

    u_m, q_m, k_m, v_m = _inproj(meta, cs[:N_META], w, 1, 1, N_META)
    n_state = 2 * STATE_W * w["bm"].shape[0]
    ya_m8, h_meta_state = _s5(jnp.broadcast_to(u_m[None], (SUBLANES, N_META, d_ssm)),
                              jnp.zeros((SUBLANES, n_state), F32), w, N_META)
    ya_m = ya_m8[0]
    yb_m = _attention_meta(q_m, k_m, v_m[0])
    h1_m = _outproj(ya_m, yb_m, meta, w, 1, 1, N_META)
    meta_halo = _meta_gate(h1_m, w)

    tm = 512
    nt = seq // tm
    u, q, k, v = _inproj(x2d, cs[N_META:], w, bsz, nt, tm)
    ya, _ = _s5(u.reshape(bsz, seq, d_ssm), h_meta_state, w, 64)
    yb = _attention(q, k, v, k_m, v_m[0], bsz, seq)
    h1 = _outproj(ya.reshape(bsz * seq, d_ssm), yb, x2d, w, bsz, nt, tm)
    out = _ffn(h1, meta_halo, w, nt, tm)
    return out.reshape(bsz, seq, d)
```

```python
import functools
import math

import jax
import jax.numpy as jnp
from jax import lax
from jax.experimental import pallas as pl
from jax.experimental.pallas import tpu as pltpu

F32 = jnp.float32
BF16 = jnp.bfloat16

EPS = 1e-6
CHUNK = 64
N_META = 16
SSM_GROUP = 16
SSM_STATE = 64
HEADS = 8
QK_NOPE = 128
QK_ROPE = 64
V_HEAD = 128
ROPE_BASE = 10000.0
CONV_W = 3

LANES = 128
SUBLANES = 8
MXU_DIM = 256
VMEM_BYTES = 64 * 1024 * 1024

HEAD_W = QK_NOPE + 2 * QK_ROPE
GROUPS_PER_MXU = MXU_DIM // SSM_GROUP
STATE_W = GROUPS_PER_MXU * SSM_STATE

ATT_TQ = 512
VT_ROWS = V_HEAD + SUBLANES
FFN_TF = 1024
FFN_SUB = 2


_NT = (((1,), (1,)), ((), ()))


def _vmem_limit(nbytes):
    return int(min(nbytes + (4 << 20), VMEM_BYTES - (4 << 20)))


def _rms(x, g):
    x = x.astype(F32)
    return x * lax.rsqrt(jnp.mean(x * x, axis=-1, keepdims=True) + EPS) * g


def _const_spec(shape):
    nd = len(shape)
    return pl.BlockSpec(shape, lambda *_: (0,) * nd, pipeline_mode=pl.Buffered(1))


def _inproj_kernel(x_ref, g_ref, win_ref, wpe_ref, qg_ref, wq_ref, kvg_ref, wk_ref, wv_ref, vone_ref,
                   cs_ref, u_ref, q_ref, k_ref, v_ref, *, scale):
    d_ssm = u_ref.shape[1]
    q_lora = wq_ref.shape[0]
    kv_lora = wk_ref.shape[0]
    xn = _rms(x_ref[...], g_ref[...]).astype(BF16)
    z = jnp.dot(xn, win_ref[...], preferred_element_type=F32)
    o1, o2, o3 = d_ssm, d_ssm + q_lora, d_ssm + q_lora + kv_lora
    u_ref[...] = z[:, :o1].astype(u_ref.dtype)

    cs = cs_ref[...]
    qn = _rms(z[:, o1:o2], qg_ref[...]).astype(BF16)
    q = jnp.dot(qn, wq_ref[...], preferred_element_type=F32)
    cs_q = cs * scale
    for h in range(HEADS):
        c0 = h * HEAD_W
        q_ref[:, c0:c0 + QK_NOPE] = (q[:, c0:c0 + QK_NOPE] * scale).astype(q_ref.dtype)
        q_ref[:, c0 + QK_NOPE:c0 + HEAD_W] = (q[:, c0 + QK_NOPE:c0 + HEAD_W] * cs_q).astype(q_ref.dtype)

    kvn = _rms(z[:, o2:o3], kvg_ref[...]).astype(BF16)
    kn = jnp.dot(kvn, wk_ref[...], preferred_element_type=F32)
    vt = lax.dot_general(wv_ref[...], kvn, _NT, preferred_element_type=F32)
    v_ref[...] = (vt + vone_ref[...]).astype(v_ref.dtype)
    t = jnp.dot(xn, wpe_ref[...], preferred_element_type=F32) * cs
    krot = (t + pltpu.roll(t, QK_ROPE, 1)).astype(k_ref.dtype)
    for h in range(HEADS):
        c0 = h * HEAD_W
        k_ref[:, c0:c0 + QK_NOPE] = kn[:, h * QK_NOPE:(h + 1) * QK_NOPE].astype(k_ref.dtype)
        k_ref[:, c0 + QK_NOPE:c0 + HEAD_W] = krot


def _inproj(x2d, cs, w, nb, nt, tm):
    rows, d = x2d.shape
    d_ssm = w["d_ssm"]
    n_in = w["win"].shape[1]
    row_spec = lambda width: pl.BlockSpec((tm, width), lambda b, j: (b * nt + j, 0))
    in_specs = [
        row_spec(d),
        _const_spec((1, d)),
        _const_spec(w["win"].shape),
        _const_spec(w["wpe"].shape),
        _const_spec(w["qg"].shape),
        _const_spec(w["wq"].shape),
        _const_spec(w["kvg"].shape),
        _const_spec(w["wk"].shape),
        _const_spec(w["wv"].shape),
        _const_spec(w["vone"].shape),
        pl.BlockSpec((tm, LANES), lambda b, j: (j, 0)),
    ]
    out_shape = (
        jax.ShapeDtypeStruct((rows, d_ssm), BF16),
        jax.ShapeDtypeStruct((rows, HEADS * HEAD_W), BF16),
        jax.ShapeDtypeStruct((rows, HEADS * HEAD_W), BF16),
        jax.ShapeDtypeStruct((nb, HEADS * VT_ROWS, nt * tm), BF16),
    )
    out_specs = (
        row_spec(d_ssm),
        row_spec(HEADS * HEAD_W),
        row_spec(HEADS * HEAD_W),
        pl.BlockSpec((None, HEADS * VT_ROWS, tm), lambda b, j: (b, 0, j)),
    )
    weights = 2 * (w["win"].size + w["wpe"].size + w["wq"].size + w["wk"].size + w["wv"].size)
    tiles = 2 * tm * (4 * d + 2 * d_ssm + 4 * HEADS * HEAD_W + 2 * HEADS * V_HEAD + 4 * LANES)
    temps = 4 * tm * (d + n_in)
    return pl.pallas_call(
        functools.partial(_inproj_kernel, scale=w["scale"]),
        grid=(nb, nt),
        in_specs=in_specs,
        out_specs=out_specs,
        out_shape=out_shape,
        compiler_params=pltpu.CompilerParams(
            dimension_semantics=("arbitrary", "arbitrary"),
            vmem_limit_bytes=_vmem_limit(weights + tiles + temps)),
        name="inproj",
    )(x2d, w["mix_g"], w["win"], w["wpe"], w["qg"], w["wq"], w["kvg"], w["wk"], w["wv"], w["vone"], cs)


def _s5_kernel(u_ref, h0_ref, perm_ref, bm_ref, cm_ref, are_ref, aim_ref, dskip_ref, wglu_ref, bglu_ref,
               y_ref, hT_ref, hbuf, state, *, steps):
    n_gb = bm_ref.shape[0]
    sw = are_ref.shape[2]
    cin = bm_ref.shape[1]
    nb, _, d_ssm = u_ref.shape

    @pl.when(pl.program_id(0) == 0)
    def _():
        state[...] = h0_ref[...]

    u = jnp.dot(perm_ref[0], u_ref[...].reshape(nb * steps, d_ssm),
                preferred_element_type=F32).astype(BF16)
    for gb in range(n_gb):
        hbuf[:, 2 * sw * gb:2 * sw * (gb + 1)] = jnp.dot(
            u[:, cin * gb:cin * (gb + 1)], bm_ref[gb], preferred_element_type=F32)

    for gb in range(n_gb):
        are = are_ref[gb]
        aim = aim_ref[gb]
        c_re = 2 * sw * gb
        c_im = c_re + sw

        hre, him = state[:, c_re:c_re + sw], state[:, c_im:c_im + sw]
        for t in range(steps):
            r0 = t * SUBLANES
            nre = are * hre - aim * him + hbuf[r0:r0 + SUBLANES, c_re:c_re + sw]
            nim = are * him + aim * hre + hbuf[r0:r0 + SUBLANES, c_im:c_im + sw]
            hbuf[r0:r0 + SUBLANES, c_re:c_re + sw] = nre
            hbuf[r0:r0 + SUBLANES, c_im:c_im + sw] = nim
            hre, him = nre, nim
        state[:, c_re:c_re + sw] = hre
        state[:, c_im:c_im + sw] = him

    ys = [jnp.dot(hbuf[:, 2 * sw * gb:2 * sw * (gb + 1)].astype(BF16), cm_ref[gb],
                  preferred_element_type=F32) for gb in range(n_gb)]
    y = jnp.concatenate(ys, axis=1) + dskip_ref[...] * u.astype(F32)
    g = jax.nn.gelu(y)
    gate = jnp.dot(g.astype(BF16), wglu_ref[...], preferred_element_type=F32) + bglu_ref[...]
    out = (g * jax.nn.sigmoid(gate)).astype(BF16)
    y_ref[...] = jnp.dot(perm_ref[1], out, preferred_element_type=F32).astype(y_ref.dtype).reshape(y_ref.shape)
    hT_ref[...] = state[...]


def _s5(u, h0, w, steps):
    nb, t_len, d_ssm = u.shape
    assert nb == SUBLANES
    r = steps * SUBLANES
    n_state = h0.shape[1]
    src = (jnp.arange(r) % SUBLANES) * steps + jnp.arange(r) // SUBLANES
    fwd = (src[:, None] == jnp.arange(r)[None, :]).astype(BF16)
    perm = jnp.stack([fwd, fwd.T])
    blk = pl.BlockSpec((nb, steps, d_ssm), lambda i: (0, i, 0))
    in_specs = [
        blk,
        _const_spec(h0.shape),
        _const_spec(perm.shape),
        _const_spec(w["bm"].shape),
        _const_spec(w["cm"].shape),
        _const_spec(w["are"].shape),
        _const_spec(w["aim"].shape),
        _const_spec((1, d_ssm)),
        _const_spec(w["wglu"].shape),
        _const_spec((1, d_ssm)),
    ]
    out_shape = (jax.ShapeDtypeStruct(u.shape, BF16),
                 jax.ShapeDtypeStruct(h0.shape, F32))
    out_specs = (blk, pl.BlockSpec(h0.shape, lambda i: (0, 0)))
    weights = (2 * (w["bm"].size + w["cm"].size + w["wglu"].size + perm.size)
               + 4 * (w["are"].size + w["aim"].size))
    tiles = 2 * r * d_ssm * 2 * 2 + 3 * 4 * h0.size
    scratch = 4 * r * n_state + 4 * h0.size
    temps = 2 * 4 * r * d_ssm
    return pl.pallas_call(
        functools.partial(_s5_kernel, steps=steps),
        grid=(t_len // steps,),
        in_specs=in_specs,
        out_specs=out_specs,
        out_shape=out_shape,
        scratch_shapes=[pltpu.VMEM((r, n_state), F32), pltpu.VMEM(h0.shape, F32)],
        compiler_params=pltpu.CompilerParams(
            dimension_semantics=("arbitrary",),
            vmem_limit_bytes=_vmem_limit(weights + tiles + scratch + temps)),
        name="s5",
    )(u, h0, perm, w["bm"], w["cm"], w["are"], w["aim"], w["dskip"], w["wglu"], w["bglu"])


def _attn_kernel(q_ref, k_ref, vt_ref, km_ref, vmt_ref, o_ref):
    seq = q_ref.shape[0]
    half = ATT_TQ // 2
    neg = jnp.finfo(F32).min
    kc = lax.broadcasted_iota(jnp.int32, (half, half), 0) // CHUNK
    qc = lax.broadcasted_iota(jnp.int32, (half, half), 1) // CHUNK
    visible = kc <= qc
    colmax = lambda t: jnp.max(t, axis=0, keepdims=True)
    prob = lambda s, m: jnp.exp2(s - m).astype(BF16)
    km = km_ref[...]
    vmt = vmt_ref[...]
    order = range(seq // ATT_TQ - 1, -1, -1)
    scores = {}
    for i in order:
        r0 = i * ATT_TQ
        q = q_ref[r0:r0 + ATT_TQ, :]
        s_meta = lax.dot_general(km, q, _NT, preferred_element_type=F32)
        s_diag = lax.dot_general(k_ref[r0:r0 + ATT_TQ, :], q, _NT, preferred_element_type=F32)
        s_full = (lax.dot_general(k_ref[0:r0, :], q, _NT, preferred_element_type=F32)
                  if i > 0 else None)
        scores[i] = (s_meta, s_diag, s_full)
    for i in order:
        r0 = i * ATT_TQ
        s_meta, s_diag, s_full = scores[i]
        s00 = jnp.where(visible, s_diag[:half, :half], neg)
        s01 = s_diag[:half, half:]
        s11 = jnp.where(visible, s_diag[half:, half:], neg)
        m = jnp.concatenate([colmax(s00), jnp.maximum(colmax(s01), colmax(s11))], axis=1)
        m = jnp.maximum(m, colmax(s_meta))
        if i > 0:
            m = jnp.maximum(m, colmax(s_full))
        p_meta = prob(s_meta, m)
        p00 = prob(s00, m[:, :half])
        p01 = prob(s01, m[:, half:])
        p11 = prob(s11, m[:, half:])
        p_diag = jnp.concatenate([jnp.concatenate([p00, p01], axis=1),
                                  jnp.concatenate([jnp.zeros_like(p11), p11], axis=1)], axis=0)
        o = jnp.dot(vmt, p_meta, preferred_element_type=F32)
        o = o + jnp.dot(vt_ref[:, r0:r0 + ATT_TQ], p_diag, preferred_element_type=F32)
        if i > 0:
            o = o + jnp.dot(vt_ref[:, 0:r0], prob(s_full, m), preferred_element_type=F32)
        out = o[:V_HEAD] / o[V_HEAD:V_HEAD + 1]
        o_ref[r0:r0 + ATT_TQ, :] = out.T.astype(o_ref.dtype)


def _attention(q, k, vt, km, vmt, nb, seq):
    assert ATT_TQ % CHUNK == 0 and seq % ATT_TQ == 0 and V_HEAD == LANES
    n_meta = km.shape[0]
    in_specs = [
        pl.BlockSpec((seq, HEAD_W), lambda b, h: (b, h)),
        pl.BlockSpec((seq, HEAD_W), lambda b, h: (b, h)),
        pl.BlockSpec((None, VT_ROWS, seq), lambda b, h: (b, h, 0)),
        pl.BlockSpec((n_meta, HEAD_W), lambda b, h: (0, h)),
        pl.BlockSpec((VT_ROWS, n_meta), lambda b, h: (h, 0)),
    ]
    tiles = 2 * 2 * seq * (2 * HEAD_W + 2 * V_HEAD)
    temps = 4 * 4 * seq * ATT_TQ
    return pl.pallas_call(
        _attn_kernel,
        grid=(nb, HEADS),
        in_specs=in_specs,
        out_specs=pl.BlockSpec((seq, V_HEAD), lambda b, h: (b, h)),
        out_shape=jax.ShapeDtypeStruct((nb * seq, HEADS * V_HEAD), BF16),
        compiler_params=pltpu.CompilerParams(
            dimension_semantics=("arbitrary", "arbitrary"),
            vmem_limit_bytes=_vmem_limit(tiles + temps)),
        name="attn",
    )(q, k, vt, km, vmt)


def _attn_meta_kernel(q_ref, k_ref, vt_ref, o_ref):
    for h in range(HEADS):
        s = lax.dot_general(q_ref[:, h * HEAD_W:(h + 1) * HEAD_W],
                            k_ref[:, h * HEAD_W:(h + 1) * HEAD_W], _NT, preferred_element_type=F32)
        p = jnp.exp2(s - jnp.max(s, axis=1, keepdims=True))
        pv = lax.dot_general(p.astype(BF16), vt_ref[h * VT_ROWS:h * VT_ROWS + V_HEAD, :], _NT,
                             preferred_element_type=F32)
        o_ref[:, h * V_HEAD:(h + 1) * V_HEAD] = (pv / jnp.sum(p, axis=1, keepdims=True)).astype(o_ref.dtype)


def _attention_meta(q, k, vt):
    return pl.pallas_call(
        _attn_meta_kernel,
        out_shape=jax.ShapeDtypeStruct((q.shape[0], HEADS * V_HEAD), BF16),
        name="attn_meta",
    )(q, k, vt)


def _outproj_kernel(ya_ref, yb_ref, x_ref, ga_ref, gb_ref, wa_ref, wb_ref, o_ref):
    a = _rms(ya_ref[...], ga_ref[...]).astype(BF16)
    b = _rms(yb_ref[...], gb_ref[...]).astype(BF16)
    mix = jnp.dot(a, wa_ref[...], preferred_element_type=F32)
    mix = mix + jnp.dot(b, wb_ref[...], preferred_element_type=F32)
    o_ref[...] = x_ref[...].astype(F32) + mix


def _outproj(ya, yb, x2d, w, nb, nt, tm):
    rows, d = x2d.shape
    d_ssm, d_attn = w["wo_a"].shape[0], w["wo_b"].shape[0]
    in_specs = [
        pl.BlockSpec((tm, d_ssm), lambda b, j: (b * nt + j, 0)),
        pl.BlockSpec((tm, d_attn), lambda b, j: (b * nt + j, 0)),
        pl.BlockSpec((tm, d), lambda b, j: (b * nt + j, 0)),
        _const_spec((1, d_ssm)),
        _const_spec((1, d_attn)),
        _const_spec(w["wo_a"].shape),
        _const_spec(w["wo_b"].shape),
    ]
    weights = 2 * (w["wo_a"].size + w["wo_b"].size)
    tiles = 2 * tm * (2 * d_ssm + 2 * d_attn + 4 * d + 4 * d)
    temps = 4 * tm * d
    return pl.pallas_call(
        _outproj_kernel,
        grid=(nb, nt),
        in_specs=in_specs,
        out_specs=pl.BlockSpec((tm, d), lambda b, j: (b * nt + j, 0)),
        out_shape=jax.ShapeDtypeStruct((rows, d), F32),
        compiler_params=pltpu.CompilerParams(
            dimension_semantics=("arbitrary", "arbitrary"),
            vmem_limit_bytes=_vmem_limit(weights + tiles + temps)),
        name="outproj",
    )(ya, yb, x2d, w["ga"], w["gb"], w["wo_a"], w["wo_b"])


def _gate_kernel(h_ref, g_ref, wg_ref, o_ref):
    xn = _rms(h_ref[...], g_ref[...]).astype(BF16)
    o_ref[...] = jnp.dot(xn, wg_ref[...], preferred_element_type=F32)


def _meta_gate(h_meta, w):
    rows, d = h_meta.shape
    n_chunks = w["conv_w"].shape[0]
    gate = pl.pallas_call(
        _gate_kernel,
        grid=(n_chunks,),
        in_specs=[_const_spec((rows, d)), _const_spec((1, d)),
                  pl.BlockSpec((d, FFN_TF), lambda j: (0, j))],
        out_specs=pl.BlockSpec((rows, FFN_TF), lambda j: (0, j)),
        out_shape=jax.ShapeDtypeStruct((rows, n_chunks * FFN_TF), F32),
        compiler_params=pltpu.CompilerParams(dimension_semantics=("arbitrary",)),
        name="meta_gate",
    )(h_meta, w["ffn_g"], w["wu"])
    halo = gate[rows - SUBLANES:].reshape(SUBLANES, n_chunks, FFN_TF)
    return jnp.swapaxes(halo, 0, 1)


def _shift_rows(x, prev, k):
    rolled = pltpu.roll(x, k, 0)
    head = jnp.concatenate([prev, x[:SUBLANES]], axis=0)[SUBLANES - k:2 * SUBLANES - k]
    return jnp.concatenate([head, rolled[SUBLANES:]], axis=0)


def _ffn_kernel(h_ref, g_ref, wu_hbm, wd_hbm, cw_ref, cb_ref, mh_ref, fg_ref, o_ref,
                xn_scr, halo_scr, wg_buf, wv_buf, wd_buf, sem, *, tiles_per_batch, d_ff):
    i = pl.program_id(0)
    tm = h_ref.shape[0]
    ts = tm // FFN_SUB
    n_full = d_ff // FFN_TF
    rem = d_ff - n_full * FFN_TF
    assert 0 < rem and rem % LANES == 0

    def chunk_copies(col0, width, slot):
        return (
            pltpu.make_async_copy(wu_hbm.at[:, pl.ds(col0, width)],
                                  wg_buf.at[slot, :, pl.ds(0, width)], sem.at[slot, 0]),
            pltpu.make_async_copy(wu_hbm.at[:, pl.ds(d_ff + col0, width)],
                                  wv_buf.at[slot, :, pl.ds(0, width)], sem.at[slot, 1]),
            pltpu.make_async_copy(wd_hbm.at[pl.ds(col0, width), :],
                                  wd_buf.at[slot, pl.ds(0, width), :], sem.at[slot, 2]),
        )

    def start(copies):
        for c in copies:
            c.start()

    def wait(copies):
        for c in copies:
            c.wait()

    ragged = lambda slot: chunk_copies(n_full * FFN_TF, rem, slot)
    full = lambda c, slot: chunk_copies(pl.multiple_of(c * FFN_TF, FFN_TF), FFN_TF, slot)

    def compute(idx, width, slot):
        wg = wg_buf[slot, :, :width]
        wv = wv_buf[slot, :, :width]
        wd = wd_buf[slot, :width, :]
        cw = cw_ref[idx][:, :width]
        cb = cb_ref[idx][:, :width]
        prev = halo_scr[idx][:, :width]
        for s in range(FFN_SUB):
            rows = pl.ds(s * ts, ts)
            xn = xn_scr[rows, :]
            gate = jnp.dot(xn, wg, preferred_element_type=F32)
            val = jnp.dot(xn, wv, preferred_element_type=F32)
            conv = (cw[0:1] * _shift_rows(gate, prev, 2)
                    + cw[1:2] * _shift_rows(gate, prev, 1)
                    + cw[2:3] * gate + cb)
            prev = gate[ts - SUBLANES:, :]
            act = (jax.nn.silu(conv) * val).astype(BF16)
            o_ref[rows, :] += jnp.dot(act, wd, preferred_element_type=F32)
        halo_scr[idx, :, pl.ds(0, width)] = prev

    slot_r = (i * (n_full + 1)) % 2

    @pl.when(i == 0)
    def _():
        start(ragged(slot_r))

    @pl.when((i % tiles_per_batch) == 0)
    def _():
        halo_scr[...] = mh_ref[...]

    wait(ragged(slot_r))
    start(full(0, 1 - slot_r))
    h = h_ref[...]
    xn_scr[...] = _rms(h, g_ref[...]).astype(BF16)
    o_ref[...] = h
    compute(n_full, rem, slot_r)

    def body(c, carry):
        slot = (slot_r + 1 + c) % 2
        wait(full(c, slot))

        @pl.when(c + 1 < n_full)
        def _():
            start(full(c + 1, 1 - slot))

        @pl.when(jnp.logical_and(c + 1 == n_full, i + 1 < pl.num_programs(0)))
        def _():
            start(ragged(1 - slot))

        compute(c, FFN_TF, slot)
        return carry

    lax.fori_loop(0, n_full, body, 0)
    o_ref[...] = _rms(o_ref[...], fg_ref[...])


def _ffn(h1, meta_halo, w, tiles_per_batch, tm):
    rows, d = h1.shape
    d_ff = w["wd"].shape[0]
    n_chunks = w["conv_w"].shape[0]
    in_specs = [
        pl.BlockSpec((tm, d), lambda i: (i, 0)),
        _const_spec((1, d)),
        pl.BlockSpec(memory_space=pl.ANY),
        pl.BlockSpec(memory_space=pl.ANY),
        _const_spec(w["conv_w"].shape),
        _const_spec(w["conv_b"].shape),
        _const_spec(meta_halo.shape),
        _const_spec((1, d)),
    ]
    slots = 2
    tiles = 2 * 2 * 4 * tm * d + 4 * (w["conv_w"].size + w["conv_b"].size + meta_halo.size)
    scratch = 2 * tm * d + 4 * meta_halo.size + slots * 3 * 2 * d * FFN_TF
    temps = 4 * (tm // FFN_SUB) * FFN_TF * 16
    return pl.pallas_call(
        functools.partial(_ffn_kernel, tiles_per_batch=tiles_per_batch, d_ff=d_ff),
        grid=(rows // tm,),
        in_specs=in_specs,
        out_specs=pl.BlockSpec((tm, d), lambda i: (i, 0)),
        out_shape=jax.ShapeDtypeStruct((rows, d), F32),
        scratch_shapes=[
            pltpu.VMEM((tm, d), BF16),
            pltpu.VMEM((n_chunks, SUBLANES, FFN_TF), F32),
            pltpu.VMEM((slots, d, FFN_TF), BF16),
            pltpu.VMEM((slots, d, FFN_TF), BF16),
            pltpu.VMEM((slots, FFN_TF, d), BF16),
            pltpu.SemaphoreType.DMA((slots, 3)),
        ],
        compiler_params=pltpu.CompilerParams(
            dimension_semantics=("arbitrary",),
            vmem_limit_bytes=_vmem_limit(tiles + scratch + temps)),
        name="ffn",
    )(h1, w["ffn_g"], w["wu"], w["wd"], w["conv_w"], w["conv_b"], meta_halo, w["final_g"])


def _swap_halves(w):
    half = w.shape[-1] // 2
    return jnp.concatenate([-w[..., half:], w[..., :half]], axis=-1)


def _block_diag(blocks, n):
    rows, c = blocks.shape[-2:]
    r = rows // n
    rep = (jnp.arange(c)[:, None] == jnp.arange(n * c)[None, :] % c).astype(blocks.dtype)
    tiled = jnp.einsum("...rc,cq->...rq", blocks, rep, precision=lax.Precision.HIGHEST)
    keep = (jnp.arange(rows)[:, None] // r) == (jnp.arange(n * c)[None, :] // c)
    return jnp.where(keep, tiled, 0.0)


def _prepare(p):
    d_ssm = p["d_skip"].shape[-1]
    q_lora = p["q_a_norm"].shape[-1]
    kv_lora = p["kv_a_norm"].shape[-1]
    d_ff = p["w_down"].shape[0]
    w = {"d_ssm": d_ssm, "scale": math.log2(math.e) / math.sqrt(QK_NOPE + QK_ROPE)}

    row = lambda v: v.reshape(1, -1).astype(F32)
    w["mix_g"] = row(p["mix_norm"])
    w_in = p["w_in"]
    o3 = d_ssm + q_lora + kv_lora
    w["win"] = w_in[:, :o3].astype(BF16)
    w["wpe"] = jnp.concatenate([w_in[:, o3:], _swap_halves(w_in[:, o3:])], axis=1).astype(BF16)
    w["qg"] = row(p["q_a_norm"])
    w["kvg"] = row(p["kv_a_norm"])
    wq = p["w_q_b"].reshape(q_lora, HEADS, QK_NOPE + QK_ROPE)
    wq_pe = wq[..., QK_NOPE:]
    w["wq"] = jnp.concatenate([wq, _swap_halves(wq_pe)], axis=-1).reshape(q_lora, HEADS * HEAD_W).astype(BF16)
    wkv = p["w_kv_b"].reshape(kv_lora, HEADS, QK_NOPE + V_HEAD)
    w["wk"] = wkv[..., :QK_NOPE].reshape(kv_lora, HEADS * QK_NOPE).astype(BF16)
    wv = jnp.pad(jnp.moveaxis(wkv[..., QK_NOPE:], 0, 2), ((0, 0), (0, VT_ROWS - V_HEAD), (0, 0)))
    w["wv"] = wv.reshape(HEADS * VT_ROWS, kv_lora).astype(BF16)
    w["vone"] = jnp.tile((jnp.arange(VT_ROWS) == V_HEAD).astype(F32), HEADS).reshape(-1, 1)

    lam_re, lam_im = p["lam_re"].astype(F32), p["lam_im"].astype(F32)
    dt = jnp.exp(p["log_dt"].astype(F32))[:, None]
    mag = jnp.exp(lam_re * dt)
    a_re, a_im = mag * jnp.cos(lam_im * dt), mag * jnp.sin(lam_im * dt)
    den = lam_re * lam_re + lam_im * lam_im
    f_re = ((a_re - 1.0) * lam_re + a_im * lam_im) / den
    f_im = (a_im * lam_re - (a_re - 1.0) * lam_im) / den
    b_re, b_im = p["b_re"].astype(F32), p["b_im"].astype(F32)
    bb_re = f_re[..., None] * b_re - f_im[..., None] * b_im
    bb_im = f_re[..., None] * b_im + f_im[..., None] * b_re
    n_g = lam_re.shape[0]
    n_gb = n_g // GROUPS_PER_MXU
    bd = lambda t: _block_diag(t.reshape(n_gb, GROUPS_PER_MXU * t.shape[1], t.shape[2]), GROUPS_PER_MXU)
    w["bm"] = jnp.concatenate([bd(jnp.swapaxes(bb_re, 1, 2)),
                               bd(jnp.swapaxes(bb_im, 1, 2))], axis=2).astype(BF16)
    c_re, c_im = p["c_re"].astype(F32), p["c_im"].astype(F32)
    w["cm"] = jnp.concatenate([bd(jnp.swapaxes(c_re, 1, 2)),
                               bd(jnp.swapaxes(-c_im, 1, 2))], axis=1).astype(BF16)
    bcast = lambda a: jnp.broadcast_to(a.reshape(n_gb, 1, STATE_W), (n_gb, SUBLANES, STATE_W))
    w["are"], w["aim"] = bcast(a_re), bcast(a_im)
    w["dskip"] = row(p["d_skip"])
    w["wglu"] = p["w_glu"].astype(BF16)
    w["bglu"] = row(p["b_glu"])

    w["ga"], w["gb"] = row(p["out_norm_ssm"]), row(p["out_norm_attn"])
    w["wo_a"] = p["w_out"][:d_ssm].astype(BF16)
    w["wo_b"] = p["w_out"][d_ssm:].astype(BF16)

    n_chunks = -(-d_ff // FFN_TF)
    pad = n_chunks * FFN_TF - d_ff
    w["wu"] = p["w_up"].astype(BF16)
    w["wd"] = p["w_down"].astype(BF16)
    per_chunk = lambda a: jnp.swapaxes(
        jnp.pad(a.astype(F32), ((0, 0), (0, pad))).reshape(a.shape[0], n_chunks, FFN_TF), 0, 1)
    w["conv_w"] = per_chunk(p["conv_w"])
    w["conv_b"] = per_chunk(p["conv_b"].reshape(1, -1))
    w["ffn_g"] = row(p["ffn_norm"])
    w["final_g"] = row(p["final_norm"])
    return w


def _rope_table(n_pos):
    pos = jnp.arange(n_pos, dtype=F32)
    inv_freq = 1.0 / (ROPE_BASE ** (jnp.arange(0, QK_ROPE, 2, dtype=F32) / QK_ROPE))
    ang = pos[:, None] * inv_freq[None, :]
    cos, sin = jnp.cos(ang), jnp.sin(ang)
    return jnp.concatenate([cos, cos, sin, sin], axis=1)


def kernel(x, meta_tokens, mix_norm, w_in, lam_re, lam_im, log_dt, b_re, b_im, c_re, c_im, d_skip, w_glu, b_glu, q_a_norm, w_q_b, kv_a_norm, w_kv_b, out_norm_ssm, out_norm_attn, w_out, ffn_norm, w_up, conv_w, conv_b, w_down, final_norm):
    bsz, seq, d = x.shape
    assert meta_tokens.shape == (N_META, d) and bsz == SUBLANES
    p = dict(mix_norm=mix_norm[0], w_in=w_in[0], lam_re=lam_re[0], lam_im=lam_im[0], log_dt=log_dt[0],
             b_re=b_re[0], b_im=b_im[0], c_re=c_re[0], c_im=c_im[0], d_skip=d_skip[0],
             w_glu=w_glu[0], b_glu=b_glu[0], q_a_norm=q_a_norm[0], w_q_b=w_q_b[0],
             kv_a_norm=kv_a_norm[0], w_kv_b=w_kv_b[0], out_norm_ssm=out_norm_ssm[0],
             out_norm_attn=out_norm_attn[0], w_out=w_out[0], ffn_norm=ffn_norm[0], w_up=w_up[0],
             conv_w=conv_w[0], conv_b=conv_b[0], w_down=w_down[0], final_norm=final_norm)
    w = _prepare(p)
    d_ssm = w["d_ssm"]
    cs = _rope_table(N_META + seq)
    x2d = x.reshape(bsz * seq, d)
    meta = meta_tokens.astype(x.dtype)

    u_m, q_m, k_m, v_m = _inproj(meta, cs[:N_META], w, 1, 1, N_META)
    n_state = 2 * STATE_W * w["bm"].shape[0]
    ya_m8, h_meta_state = _s5(jnp.broadcast_to(u_m[None], (SUBLANES, N_META, d_ssm)),
                              jnp.zeros((SUBLANES, n_state), F32), w, N_META)
    ya_m = ya_m8[0]
    yb_m = _attention_meta(q_m, k_m, v_m[0])
    h1_m = _outproj(ya_m, yb_m, meta, w, 1, 1, N_META)
    meta_halo = _meta_gate(h1_m, w)

    tm = 512
    nt = seq // tm
    u, q, k, v = _inproj(x2d, cs[N_META:], w, bsz, nt, tm)
    ya, _ = _s5(u.reshape(bsz, seq, d_ssm), h_meta_state, w, 64)
    yb = _attention(q, k, v, k_m, v_m[0], bsz, seq)
    h1 = _outproj(ya.reshape(bsz * seq, d_ssm), yb, x2d, w, bsz, nt, tm)
    out = _ffn(h1, meta_halo, w, nt, tm)
    return out.reshape(bsz, seq, d)
```

```python
import functools
import math

import jax
import jax.numpy as jnp
from jax import lax
from jax.experimental import pallas as pl
from jax.experimental.pallas import tpu as pltpu

F32 = jnp.float32
BF16 = jnp.bfloat16

EPS = 1e-6
CHUNK = 64
N_META = 16
SSM_GROUP = 16
SSM_STATE = 64
HEADS = 8
QK_NOPE = 128
QK_ROPE = 64
V_HEAD = 128
ROPE_BASE = 10000.0
CONV_W = 3

LANES = 128
SUBLANES = 8
MXU_DIM = 256
VMEM_BYTES = 64 * 1024 * 1024

HEAD_W = QK_NOPE + 2 * QK_ROPE
GROUPS_PER_MXU = MXU_DIM // SSM_GROUP
STATE_W = GROUPS_PER_MXU * SSM_STATE

ATT_TQ = 512
VT_ROWS = V_HEAD + SUBLANES
FFN_TF = 1024
FFN_SUB = 2
S5_SUB = 2
S5_SCAN_W = 512


_NT = (((1,), (1,)), ((), ()))


def _vmem_limit(nbytes):
    return int(min(nbytes + (4 << 20), VMEM_BYTES - (4 << 20)))


def _rms(x, g):
    x = x.astype(F32)
    return x * lax.rsqrt(jnp.mean(x * x, axis=-1, keepdims=True) + EPS) * g


def _const_spec(shape):
    nd = len(shape)
    return pl.BlockSpec(shape, lambda *_: (0,) * nd, pipeline_mode=pl.Buffered(1))


def _inproj_kernel(x_ref, g_ref, win_ref, qg_ref, wq_ref, kvg_ref, wk_ref, wv_ref, vone_ref,
                   cs_ref, u_ref, q_ref, k_ref, v_ref, *, scale):
    d_ssm = u_ref.shape[1]
    q_lora = wq_ref.shape[0]
    kv_lora = wk_ref.shape[0]
    xn = _rms(x_ref[...], g_ref[...]).astype(BF16)
    z = jnp.dot(xn, win_ref[...], preferred_element_type=F32)
    o1, o2, o3 = d_ssm, d_ssm + q_lora, d_ssm + q_lora + kv_lora
    u_ref[...] = z[:, :o1].astype(u_ref.dtype)

    cs = cs_ref[...]
    qn = _rms(z[:, o1:o2], qg_ref[...]).astype(BF16)
    q = jnp.dot(qn, wq_ref[...], preferred_element_type=F32)
    cs_q = cs * scale
    for h in range(HEADS):
        c0 = h * HEAD_W
        q_ref[:, c0:c0 + QK_NOPE] = (q[:, c0:c0 + QK_NOPE] * scale).astype(q_ref.dtype)
        q_ref[:, c0 + QK_NOPE:c0 + HEAD_W] = (q[:, c0 + QK_NOPE:c0 + HEAD_W] * cs_q).astype(q_ref.dtype)

    kvn = _rms(z[:, o2:o3], kvg_ref[...]).astype(BF16)
    kn = jnp.dot(kvn, wk_ref[...], preferred_element_type=F32)
    vt = lax.dot_general(wv_ref[...], kvn, _NT, preferred_element_type=F32)
    v_ref[...] = (vt + vone_ref[...]).astype(v_ref.dtype)
    kpe = z[:, o3:o3 + QK_ROPE]
    x1, x2 = kpe[:, :QK_ROPE // 2], kpe[:, QK_ROPE // 2:]
    t = jnp.concatenate([kpe, -x2, x1], axis=1) * cs
    krot = (t + pltpu.roll(t, QK_ROPE, 1)).astype(k_ref.dtype)
    for h in range(HEADS):
        c0 = h * HEAD_W
        k_ref[:, c0:c0 + QK_NOPE] = kn[:, h * QK_NOPE:(h + 1) * QK_NOPE].astype(k_ref.dtype)
        k_ref[:, c0 + QK_NOPE:c0 + HEAD_W] = krot


def _inproj(x2d, cs, w, nb, nt, tm):
    rows, d = x2d.shape
    d_ssm = w["d_ssm"]
    n_in = w["win"].shape[1]
    row_spec = lambda width: pl.BlockSpec((tm, width), lambda b, j: (b * nt + j, 0))
    in_specs = [
        row_spec(d),
        _const_spec((1, d)),
        _const_spec(w["win"].shape),
        _const_spec(w["qg"].shape),
        _const_spec(w["wq"].shape),
        _const_spec(w["kvg"].shape),
        _const_spec(w["wk"].shape),
        _const_spec(w["wv"].shape),
        _const_spec(w["vone"].shape),
        pl.BlockSpec((tm, LANES), lambda b, j: (j, 0)),
    ]
    out_shape = (
        jax.ShapeDtypeStruct((rows, d_ssm), BF16),
        jax.ShapeDtypeStruct((rows, HEADS * HEAD_W), BF16),
        jax.ShapeDtypeStruct((rows, HEADS * HEAD_W), BF16),
        jax.ShapeDtypeStruct((nb, HEADS * VT_ROWS, nt * tm), BF16),
    )
    out_specs = (
        row_spec(d_ssm),
        row_spec(HEADS * HEAD_W),
        row_spec(HEADS * HEAD_W),
        pl.BlockSpec((None, HEADS * VT_ROWS, tm), lambda b, j: (b, 0, j)),
    )
    weights = 2 * (w["win"].size + w["wq"].size + w["wk"].size + w["wv"].size)
    tiles = 2 * tm * (4 * d + 2 * d_ssm + 4 * HEADS * HEAD_W + 2 * HEADS * V_HEAD + 4 * LANES)
    temps = 4 * tm * (d + n_in)
    return pl.pallas_call(
        functools.partial(_inproj_kernel, scale=w["scale"]),
        grid=(nb, nt),
        in_specs=in_specs,
        out_specs=out_specs,
        out_shape=out_shape,
        compiler_params=pltpu.CompilerParams(
            dimension_semantics=("arbitrary", "arbitrary"),
            vmem_limit_bytes=_vmem_limit(weights + tiles + temps)),
        name="inproj",
    )(x2d, w["mix_g"], w["win"], w["qg"], w["wq"], w["kvg"], w["wk"], w["wv"], w["vone"], cs)


def _s5_kernel(u_ref, h0_ref, perm_ref, bm_ref, cm_ref, are_ref, aim_ref, dskip_ref, wglu_ref, bglu_ref,
               y_ref, hT_ref, hbuf, state, *, steps):
    n_gb = bm_ref.shape[0]
    sw = are_ref.shape[2]
    cin = bm_ref.shape[1]
    nb, _, d_ssm = u_ref.shape

    @pl.when(pl.program_id(0) == 0)
    def _():
        state[...] = h0_ref[...]

    u = jnp.dot(perm_ref[0], u_ref[...].reshape(nb * steps, d_ssm),
                preferred_element_type=F32).astype(BF16)

    sub = steps // S5_SUB
    span = lambda k: slice(k * sub * SUBLANES, (k + 1) * sub * SUBLANES)
    for k in range(S5_SUB):
        for gb in range(n_gb):
            hbuf[span(k), 2 * sw * gb:2 * sw * (gb + 1)] = jnp.dot(
                u[span(k), cin * gb:cin * (gb + 1)], bm_ref[gb], preferred_element_type=F32)

    blocks = [(2 * sw * gb + o, 2 * sw * gb + sw + o, gb, o)
              for gb in range(n_gb) for o in range(0, sw, S5_SCAN_W)]
    carry = [(state[:, c_re:c_re + S5_SCAN_W], state[:, c_im:c_im + S5_SCAN_W])
             for c_re, c_im, _, _ in blocks]
    for k in range(S5_SUB):
        for j, (c_re, c_im, gb, o) in enumerate(blocks):
            are = are_ref[gb, :, o:o + S5_SCAN_W]
            aim = aim_ref[gb, :, o:o + S5_SCAN_W]
            hre, him = carry[j]
            for t in range(k * sub, (k + 1) * sub):
                r0 = t * SUBLANES
                nre = are * hre - aim * him + hbuf[r0:r0 + SUBLANES, c_re:c_re + S5_SCAN_W]
                nim = are * him + aim * hre + hbuf[r0:r0 + SUBLANES, c_im:c_im + S5_SCAN_W]
                hbuf[r0:r0 + SUBLANES, c_re:c_re + S5_SCAN_W] = nre
                hbuf[r0:r0 + SUBLANES, c_im:c_im + S5_SCAN_W] = nim
                hre, him = nre, nim
            carry[j] = (hre, him)
    for (c_re, c_im, _, _), (hre, him) in zip(blocks, carry):
        state[:, c_re:c_re + S5_SCAN_W] = hre
        state[:, c_im:c_im + S5_SCAN_W] = him

    outs = []
    for k in range(S5_SUB):
        ys = [jnp.dot(hbuf[span(k), 2 * sw * gb:2 * sw * (gb + 1)].astype(BF16), cm_ref[gb],
                      preferred_element_type=F32) for gb in range(n_gb)]
        y = jnp.concatenate(ys, axis=1) + dskip_ref[...] * u[span(k)].astype(F32)
        g = jax.nn.gelu(y)
        gate = jnp.dot(g.astype(BF16), wglu_ref[...], preferred_element_type=F32) + bglu_ref[...]
        outs.append((g * jax.nn.sigmoid(gate)).astype(BF16))
    out = jnp.concatenate(outs, axis=0)
    y_ref[...] = jnp.dot(perm_ref[1], out, preferred_element_type=F32).astype(y_ref.dtype).reshape(y_ref.shape)
    hT_ref[...] = state[...]


def _s5(u, h0, w, steps):
    nb, t_len, d_ssm = u.shape
    assert nb == SUBLANES
    r = steps * SUBLANES
    n_state = h0.shape[1]
    src = (jnp.arange(r) % SUBLANES) * steps + jnp.arange(r) // SUBLANES
    fwd = (src[:, None] == jnp.arange(r)[None, :]).astype(BF16)
    perm = jnp.stack([fwd, fwd.T])
    blk = pl.BlockSpec((nb, steps, d_ssm), lambda i: (0, i, 0))
    in_specs = [
        blk,
        _const_spec(h0.shape),
        _const_spec(perm.shape),
        _const_spec(w["bm"].shape),
        _const_spec(w["cm"].shape),
        _const_spec(w["are"].shape),
        _const_spec(w["aim"].shape),
        _const_spec((1, d_ssm)),
        _const_spec(w["wglu"].shape),
        _const_spec((1, d_ssm)),
    ]
    out_shape = (jax.ShapeDtypeStruct(u.shape, BF16),
                 jax.ShapeDtypeStruct(h0.shape, F32))
    out_specs = (blk, pl.BlockSpec(h0.shape, lambda i: (0, 0)))
    weights = (2 * (w["bm"].size + w["cm"].size + w["wglu"].size + perm.size)
               + 4 * (w["are"].size + w["aim"].size))
    tiles = 2 * r * d_ssm * 2 * 2 + 3 * 4 * h0.size
    scratch = 4 * r * n_state + 4 * h0.size
    temps = 2 * 4 * r * d_ssm
    return pl.pallas_call(
        functools.partial(_s5_kernel, steps=steps),
        grid=(t_len // steps,),
        in_specs=in_specs,
        out_specs=out_specs,
        out_shape=out_shape,
        scratch_shapes=[pltpu.VMEM((r, n_state), F32), pltpu.VMEM(h0.shape, F32)],
        compiler_params=pltpu.CompilerParams(
            dimension_semantics=("arbitrary",),
            vmem_limit_bytes=_vmem_limit(weights + tiles + scratch + temps)),
        name="s5",
    )(u, h0, perm, w["bm"], w["cm"], w["are"], w["aim"], w["dskip"], w["wglu"], w["bglu"])


def _attn_kernel(q_ref, k_ref, vt_ref, km_ref, vmt_ref, o_ref):
    seq = q_ref.shape[0]
    half = ATT_TQ // 2
    neg = jnp.finfo(F32).min
    kc = lax.broadcasted_iota(jnp.int32, (half, half), 0) // CHUNK
    qc = lax.broadcasted_iota(jnp.int32, (half, half), 1) // CHUNK
    visible = kc <= qc
    colmax = lambda t: jnp.max(t, axis=0, keepdims=True)
    prob = lambda s, m: jnp.exp2(s - m).astype(BF16)
    km = km_ref[...]
    vmt = vmt_ref[...]
    order = range(seq // ATT_TQ - 1, -1, -1)
    scores = {}
    for i in order:
        r0 = i * ATT_TQ
        q = q_ref[r0:r0 + ATT_TQ, :]
        s_meta = lax.dot_general(km, q, _NT, preferred_element_type=F32)
        s_diag = lax.dot_general(k_ref[r0:r0 + ATT_TQ, :], q, _NT, preferred_element_type=F32)
        s_full = (lax.dot_general(k_ref[0:r0, :], q, _NT, preferred_element_type=F32)
                  if i > 0 else None)
        scores[i] = (s_meta, s_diag, s_full)
    for i in order:
        r0 = i * ATT_TQ
        s_meta, s_diag, s_full = scores[i]
        s00 = jnp.where(visible, s_diag[:half, :half], neg)
        s01 = s_diag[:half, half:]
        s11 = jnp.where(visible, s_diag[half:, half:], neg)
        m = jnp.concatenate([colmax(s00), jnp.maximum(colmax(s01), colmax(s11))], axis=1)
        m = jnp.maximum(m, colmax(s_meta))
        if i > 0:
            m = jnp.maximum(m, colmax(s_full))
        p_meta = prob(s_meta, m)
        p00 = prob(s00, m[:, :half])
        p01 = prob(s01, m[:, half:])
        p11 = prob(s11, m[:, half:])
        p_diag = jnp.concatenate([jnp.concatenate([p00, p01], axis=1),
                                  jnp.concatenate([jnp.zeros_like(p11), p11], axis=1)], axis=0)
        o = jnp.dot(vmt, p_meta, preferred_element_type=F32)
        o = o + jnp.dot(vt_ref[:, r0:r0 + ATT_TQ], p_diag, preferred_element_type=F32)
        if i > 0:
            o = o + jnp.dot(vt_ref[:, 0:r0], prob(s_full, m), preferred_element_type=F32)
        out = o[:V_HEAD] / o[V_HEAD:V_HEAD + 1]
        o_ref[r0:r0 + ATT_TQ, :] = out.T.astype(o_ref.dtype)


def _attention(q, k, vt, km, vmt, nb, seq):
    assert ATT_TQ % CHUNK == 0 and seq % ATT_TQ == 0 and V_HEAD == LANES
    n_meta = km.shape[0]
    in_specs = [
        pl.BlockSpec((seq, HEAD_W), lambda b, h: (b, h)),
        pl.BlockSpec((seq, HEAD_W), lambda b, h: (b, h)),
        pl.BlockSpec((None, VT_ROWS, seq), lambda b, h: (b, h, 0)),
        pl.BlockSpec((n_meta, HEAD_W), lambda b, h: (0, h)),
        pl.BlockSpec((VT_ROWS, n_meta), lambda b, h: (h, 0)),
    ]
    tiles = 2 * 2 * seq * (2 * HEAD_W + 2 * V_HEAD)
    temps = 4 * 4 * seq * ATT_TQ
    return pl.pallas_call(
        _attn_kernel,
        grid=(nb, HEADS),
        in_specs=in_specs,
        out_specs=pl.BlockSpec((seq, V_HEAD), lambda b, h: (b, h)),
        out_shape=jax.ShapeDtypeStruct((nb * seq, HEADS * V_HEAD), BF16),
        compiler_params=pltpu.CompilerParams(
            dimension_semantics=("arbitrary", "arbitrary"),
            vmem_limit_bytes=_vmem_limit(tiles + temps)),
        name="attn",
    )(q, k, vt, km, vmt)


def _attn_meta_kernel(q_ref, k_ref, vt_ref, o_ref):
    for h in range(HEADS):
        s = lax.dot_general(q_ref[:, h * HEAD_W:(h + 1) * HEAD_W],
                            k_ref[:, h * HEAD_W:(h + 1) * HEAD_W], _NT, preferred_element_type=F32)
        p = jnp.exp2(s - jnp.max(s, axis=1, keepdims=True))
        pv = lax.dot_general(p.astype(BF16), vt_ref[h * VT_ROWS:h * VT_ROWS + V_HEAD, :], _NT,
                             preferred_element_type=F32)
        o_ref[:, h * V_HEAD:(h + 1) * V_HEAD] = (pv / jnp.sum(p, axis=1, keepdims=True)).astype(o_ref.dtype)


def _attention_meta(q, k, vt):
    return pl.pallas_call(
        _attn_meta_kernel,
        out_shape=jax.ShapeDtypeStruct((q.shape[0], HEADS * V_HEAD), BF16),
        name="attn_meta",
    )(q, k, vt)


def _outproj_kernel(ya_ref, yb_ref, x_ref, ga_ref, gb_ref, wa_ref, wb_ref, o_ref):
    a = _rms(ya_ref[...], ga_ref[...]).astype(BF16)
    b = _rms(yb_ref[...], gb_ref[...]).astype(BF16)
    mix = jnp.dot(a, wa_ref[...], preferred_element_type=F32)
    mix = mix + jnp.dot(b, wb_ref[...], preferred_element_type=F32)
    o_ref[...] = x_ref[...].astype(F32) + mix


def _outproj(ya, yb, x2d, w, nb, nt, tm):
    rows, d = x2d.shape
    d_ssm, d_attn = w["wo_a"].shape[0], w["wo_b"].shape[0]
    in_specs = [
        pl.BlockSpec((tm, d_ssm), lambda b, j: (b * nt + j, 0)),
        pl.BlockSpec((tm, d_attn), lambda b, j: (b * nt + j, 0)),
        pl.BlockSpec((tm, d), lambda b, j: (b * nt + j, 0)),
        _const_spec((1, d_ssm)),
        _const_spec((1, d_attn)),
        _const_spec(w["wo_a"].shape),
        _const_spec(w["wo_b"].shape),
    ]
    weights = 2 * (w["wo_a"].size + w["wo_b"].size)
    tiles = 2 * tm * (2 * d_ssm + 2 * d_attn + 4 * d + 4 * d)
    temps = 4 * tm * d
    return pl.pallas_call(
        _outproj_kernel,
        grid=(nb, nt),
        in_specs=in_specs,
        out_specs=pl.BlockSpec((tm, d), lambda b, j: (b * nt + j, 0)),
        out_shape=jax.ShapeDtypeStruct((rows, d), F32),
        compiler_params=pltpu.CompilerParams(
            dimension_semantics=("arbitrary", "arbitrary"),
            vmem_limit_bytes=_vmem_limit(weights + tiles + temps)),
        name="outproj",
    )(ya, yb, x2d, w["ga"], w["gb"], w["wo_a"], w["wo_b"])


def _gate_kernel(h_ref, g_ref, wg_ref, o_ref):
    xn = _rms(h_ref[...], g_ref[...]).astype(BF16)
    o_ref[...] = jnp.dot(xn, wg_ref[...], preferred_element_type=F32)


def _meta_gate(h_meta, w):
    rows, d = h_meta.shape
    n_chunks = w["conv_w"].shape[0]
    gate = pl.pallas_call(
        _gate_kernel,
        grid=(n_chunks,),
        in_specs=[_const_spec((rows, d)), _const_spec((1, d)),
                  pl.BlockSpec((d, FFN_TF), lambda j: (0, j))],
        out_specs=pl.BlockSpec((rows, FFN_TF), lambda j: (0, j)),
        out_shape=jax.ShapeDtypeStruct((rows, n_chunks * FFN_TF), F32),
        compiler_params=pltpu.CompilerParams(dimension_semantics=("arbitrary",)),
        name="meta_gate",
    )(h_meta, w["ffn_g"], w["wu"])
    halo = gate[rows - SUBLANES:].reshape(SUBLANES, n_chunks, FFN_TF)
    return jnp.swapaxes(halo, 0, 1)


def _shift_rows(x, prev, k):
    rolled = pltpu.roll(x, k, 0)
    head = jnp.concatenate([prev, x[:SUBLANES]], axis=0)[SUBLANES - k:2 * SUBLANES - k]
    return jnp.concatenate([head, rolled[SUBLANES:]], axis=0)


def _ffn_kernel(h_ref, g_ref, wu_hbm, wd_hbm, cw_ref, cb_ref, mh_ref, fg_ref, o_ref,
                xn_scr, halo_scr, wg_buf, wv_buf, wd_buf, sem, *, tiles_per_batch, d_ff):
    i = pl.program_id(0)
    tm = h_ref.shape[0]
    ts = tm // FFN_SUB
    n_full = d_ff // FFN_TF
    rem = d_ff - n_full * FFN_TF
    assert 0 < rem and rem % LANES == 0

    def chunk_copies(col0, width, slot):
        return (
            pltpu.make_async_copy(wu_hbm.at[:, pl.ds(col0, width)],
                                  wg_buf.at[slot, :, pl.ds(0, width)], sem.at[slot, 0]),
            pltpu.make_async_copy(wu_hbm.at[:, pl.ds(d_ff + col0, width)],
                                  wv_buf.at[slot, :, pl.ds(0, width)], sem.at[slot, 1]),
            pltpu.make_async_copy(wd_hbm.at[pl.ds(col0, width), :],
                                  wd_buf.at[slot, pl.ds(0, width), :], sem.at[slot, 2]),
        )

    def start(copies):
        for c in copies:
            c.start()

    def wait(copies):
        for c in copies:
            c.wait()

    ragged = lambda slot: chunk_copies(n_full * FFN_TF, rem, slot)
    full = lambda c, slot: chunk_copies(pl.multiple_of(c * FFN_TF, FFN_TF), FFN_TF, slot)

    def compute(idx, width, slot):
        wg = wg_buf[slot, :, :width]
        wv = wv_buf[slot, :, :width]
        wd = wd_buf[slot, :width, :]
        cw = cw_ref[idx][:, :width]
        cb = cb_ref[idx][:, :width]
        prev = halo_scr[idx][:, :width]
        for s in range(FFN_SUB):
            rows = pl.ds(s * ts, ts)
            xn = xn_scr[rows, :]
            gate = jnp.dot(xn, wg, preferred_element_type=F32)
            val = jnp.dot(xn, wv, preferred_element_type=F32)
            conv = (cw[0:1] * _shift_rows(gate, prev, 2)
                    + cw[1:2] * _shift_rows(gate, prev, 1)
                    + cw[2:3] * gate + cb)
            prev = gate[ts - SUBLANES:, :]
            act = (jax.nn.silu(conv) * val).astype(BF16)
            o_ref[rows, :] += jnp.dot(act, wd, preferred_element_type=F32)
        halo_scr[idx, :, pl.ds(0, width)] = prev

    assert n_full >= 2
    slot0 = (i * (n_full + 1)) % 2

    @pl.when(i == 0)
    def _():
        start(full(0, slot0))

    @pl.when((i % tiles_per_batch) == 0)
    def _():
        halo_scr[...] = mh_ref[...]

    h = h_ref[...]
    xn_scr[...] = _rms(h, g_ref[...]).astype(BF16)
    o_ref[...] = h

    def body(c, carry):
        slot = (slot0 + c) % 2
        wait(full(c, slot))

        @pl.when(c + 1 < n_full)
        def _():
            start(full(c + 1, 1 - slot))

        @pl.when(c + 1 == n_full)
        def _():
            start(ragged(1 - slot))

        compute(c, FFN_TF, slot)
        return carry

    lax.fori_loop(0, n_full, body, 0)

    slot_r = (slot0 + n_full) % 2
    wait(ragged(slot_r))

    @pl.when(i + 1 < pl.num_programs(0))
    def _():
        start(full(0, 1 - slot_r))

    compute(n_full, rem, slot_r)
    o_ref[...] = _rms(o_ref[...], fg_ref[...])


def _ffn(h1, meta_halo, w, tiles_per_batch, tm):
    rows, d = h1.shape
    d_ff = w["wd"].shape[0]
    n_chunks = w["conv_w"].shape[0]
    in_specs = [
        pl.BlockSpec((tm, d), lambda i: (i, 0)),
        _const_spec((1, d)),
        pl.BlockSpec(memory_space=pl.ANY),
        pl.BlockSpec(memory_space=pl.ANY),
        _const_spec(w["conv_w"].shape),
        _const_spec(w["conv_b"].shape),
        _const_spec(meta_halo.shape),
        _const_spec((1, d)),
    ]
    slots = 2
    tiles = 2 * 2 * 4 * tm * d + 4 * (w["conv_w"].size + w["conv_b"].size + meta_halo.size)
    scratch = 2 * tm * d + 4 * meta_halo.size + slots * 3 * 2 * d * FFN_TF
    temps = 4 * (tm // FFN_SUB) * FFN_TF * 16
    return pl.pallas_call(
        functools.partial(_ffn_kernel, tiles_per_batch=tiles_per_batch, d_ff=d_ff),
        grid=(rows // tm,),
        in_specs=in_specs,
        out_specs=pl.BlockSpec((tm, d), lambda i: (i, 0)),
        out_shape=jax.ShapeDtypeStruct((rows, d), F32),
        scratch_shapes=[
            pltpu.VMEM((tm, d), BF16),
            pltpu.VMEM((n_chunks, SUBLANES, FFN_TF), F32),
            pltpu.VMEM((slots, d, FFN_TF), BF16),
            pltpu.VMEM((slots, d, FFN_TF), BF16),
            pltpu.VMEM((slots, FFN_TF, d), BF16),
            pltpu.SemaphoreType.DMA((slots, 3)),
        ],
        compiler_params=pltpu.CompilerParams(
            dimension_semantics=("arbitrary",),
            vmem_limit_bytes=_vmem_limit(tiles + scratch + temps)),
        name="ffn",
    )(h1, w["ffn_g"], w["wu"], w["wd"], w["conv_w"], w["conv_b"], meta_halo, w["final_g"])


def _swap_halves(w):
    half = w.shape[-1] // 2
    return jnp.concatenate([-w[..., half:], w[..., :half]], axis=-1)


def _block_diag(blocks, n):
    rows, c = blocks.shape[-2:]
    r = rows // n
    rep = (jnp.arange(c)[:, None] == jnp.arange(n * c)[None, :] % c).astype(blocks.dtype)
    tiled = jnp.einsum("...rc,cq->...rq", blocks, rep, precision=lax.Precision.HIGHEST)
    keep = (jnp.arange(rows)[:, None] // r) == (jnp.arange(n * c)[None, :] // c)
    return jnp.where(keep, tiled, 0.0)


def _prepare(p):
    d_ssm = p["d_skip"].shape[-1]
    q_lora = p["q_a_norm"].shape[-1]
    kv_lora = p["kv_a_norm"].shape[-1]
    d_ff = p["w_down"].shape[0]
    w = {"d_ssm": d_ssm, "scale": math.log2(math.e) / math.sqrt(QK_NOPE + QK_ROPE)}

    row = lambda v: v.reshape(1, -1).astype(F32)
    w["mix_g"] = row(p["mix_norm"])
    w["win"] = p["w_in"].astype(BF16)
    w["qg"] = row(p["q_a_norm"])
    w["kvg"] = row(p["kv_a_norm"])
    wq = p["w_q_b"].reshape(q_lora, HEADS, QK_NOPE + QK_ROPE)
    wq_pe = wq[..., QK_NOPE:]
    w["wq"] = jnp.concatenate([wq, _swap_halves(wq_pe)], axis=-1).reshape(q_lora, HEADS * HEAD_W).astype(BF16)
    wkv = p["w_kv_b"].reshape(kv_lora, HEADS, QK_NOPE + V_HEAD)
    w["wk"] = wkv[..., :QK_NOPE].reshape(kv_lora, HEADS * QK_NOPE).astype(BF16)
    wv = jnp.pad(jnp.moveaxis(wkv[..., QK_NOPE:], 0, 2), ((0, 0), (0, VT_ROWS - V_HEAD), (0, 0)))
    w["wv"] = wv.reshape(HEADS * VT_ROWS, kv_lora).astype(BF16)
    w["vone"] = jnp.tile((jnp.arange(VT_ROWS) == V_HEAD).astype(F32), HEADS).reshape(-1, 1)

    lam_re, lam_im = p["lam_re"].astype(F32), p["lam_im"].astype(F32)
    dt = jnp.exp(p["log_dt"].astype(F32))[:, None]
    mag = jnp.exp(lam_re * dt)
    a_re, a_im = mag * jnp.cos(lam_im * dt), mag * jnp.sin(lam_im * dt)
    den = lam_re * lam_re + lam_im * lam_im
    f_re = ((a_re - 1.0) * lam_re + a_im * lam_im) / den
    f_im = (a_im * lam_re - (a_re - 1.0) * lam_im) / den
    b_re, b_im = p["b_re"].astype(F32), p["b_im"].astype(F32)
    bb_re = f_re[..., None] * b_re - f_im[..., None] * b_im
    bb_im = f_re[..., None] * b_im + f_im[..., None] * b_re
    n_g = lam_re.shape[0]
    n_gb = n_g // GROUPS_PER_MXU
    bd = lambda t: _block_diag(t.reshape(n_gb, GROUPS_PER_MXU * t.shape[1], t.shape[2]), GROUPS_PER_MXU)
    w["bm"] = jnp.concatenate([bd(jnp.swapaxes(bb_re, 1, 2)),
                               bd(jnp.swapaxes(bb_im, 1, 2))], axis=2).astype(BF16)
    c_re, c_im = p["c_re"].astype(F32), p["c_im"].astype(F32)
    w["cm"] = jnp.concatenate([bd(jnp.swapaxes(c_re, 1, 2)),
                               bd(jnp.swapaxes(-c_im, 1, 2))], axis=1).astype(BF16)
    bcast = lambda a: jnp.broadcast_to(a.reshape(n_gb, 1, STATE_W), (n_gb, SUBLANES, STATE_W))
    w["are"], w["aim"] = bcast(a_re), bcast(a_im)
    w["dskip"] = row(p["d_skip"])
    w["wglu"] = p["w_glu"].astype(BF16)
    w["bglu"] = row(p["b_glu"])

    w["ga"], w["gb"] = row(p["out_norm_ssm"]), row(p["out_norm_attn"])
    w["wo_a"] = p["w_out"][:d_ssm].astype(BF16)
    w["wo_b"] = p["w_out"][d_ssm:].astype(BF16)

    n_chunks = -(-d_ff // FFN_TF)
    pad = n_chunks * FFN_TF - d_ff
    w["wu"] = p["w_up"].astype(BF16)
    w["wd"] = p["w_down"].astype(BF16)
    per_chunk = lambda a: jnp.swapaxes(
        jnp.pad(a.astype(F32), ((0, 0), (0, pad))).reshape(a.shape[0], n_chunks, FFN_TF), 0, 1)
    w["conv_w"] = per_chunk(p["conv_w"])
    w["conv_b"] = per_chunk(p["conv_b"].reshape(1, -1))
    w["ffn_g"] = row(p["ffn_norm"])
    w["final_g"] = row(p["final_norm"])
    return w


def _rope_table(n_pos):
    pos = jnp.arange(n_pos, dtype=F32)
    inv_freq = 1.0 / (ROPE_BASE ** (jnp.arange(0, QK_ROPE, 2, dtype=F32) / QK_ROPE))
    ang = pos[:, None] * inv_freq[None, :]
    cos, sin = jnp.cos(ang), jnp.sin(ang)
    return jnp.concatenate([cos, cos, sin, sin], axis=1)


def kernel(x, meta_tokens, mix_norm, w_in, lam_re, lam_im, log_dt, b_re, b_im, c_re, c_im, d_skip, w_glu, b_glu, q_a_norm, w_q_b, kv_a_norm, w_kv_b, out_norm_ssm, out_norm_attn, w_out, ffn_norm, w_up, conv_w, conv_b, w_down, final_norm):
    bsz, seq, d = x.shape
    assert meta_tokens.shape == (N_META, d) and bsz == SUBLANES
    p = dict(mix_norm=mix_norm[0], w_in=w_in[0], lam_re=lam_re[0], lam_im=lam_im[0], log_dt=log_dt[0],
             b_re=b_re[0], b_im=b_im[0], c_re=c_re[0], c_im=c_im[0], d_skip=d_skip[0],
             w_glu=w_glu[0], b_glu=b_glu[0], q_a_norm=q_a_norm[0], w_q_b=w_q_b[0],
             kv_a_norm=kv_a_norm[0], w_kv_b=w_kv_b[0], out_norm_ssm=out_norm_ssm[0],
             out_norm_attn=out_norm_attn[0], w_out=w_out[0], ffn_norm=ffn_norm[0], w_up=w_up[0],
             conv_w=conv_w[0], conv_b=conv_b[0], w_down=w_down[0], final_norm=final_norm)
    w = _prepare(p)
    d_ssm = w["d_ssm"]
    cs = _rope_table(N_META + seq)
    x2d = x.reshape(bsz * seq, d)
    meta = meta_tokens.astype(x.dtype)

    u_m, q_m, k_m, v_m = _inproj(meta, cs[:N_META], w, 1, 1, N_META)
    n_state = 2 * STATE_W * w["bm"].shape[0]
    ya_m8, h_meta_state = _s5(jnp.broadcast_to(u_m[None], (SUBLANES, N_META, d_ssm)),
                              jnp.zeros((SUBLANES, n_state), F32), w, N_META)
    ya_m = ya_m8[0]
    yb_m = _attention_meta(q_m, k_m, v_m[0])
    h1_m = _outproj(ya_m, yb_m, meta, w, 1, 1, N_META)
    meta_halo = _meta_gate(h1_m, w)

    tm = 512
    nt = seq // tm
    u, q, k, v = _inproj(x2d, cs[N_META:], w, bsz, nt, tm)
    ya, _ = _s5(u.reshape(bsz, seq, d_ssm), h_meta_state, w, 64)
    yb = _attention(q, k, v, k_m, v_m[0], bsz, seq)
    h1 = _outproj(ya.reshape(bsz * seq, d_ssm), yb, x2d, w, bsz, nt, tm)
    out = _ffn(h1, meta_halo, w, nt, tm)
    return out.reshape(bsz, seq, d)
```

```python
import functools
import math

import jax
import jax.numpy as jnp
from jax import lax
from jax.experimental import pallas as pl
from jax.experimental.pallas import tpu as pltpu

F32 = jnp.float32
BF16 = jnp.bfloat16

EPS = 1e-6
CHUNK = 64
N_META = 16
SSM_GROUP = 16
SSM_STATE = 64
HEADS = 8
QK_NOPE = 128
QK_ROPE = 64
V_HEAD = 128
ROPE_BASE = 10000.0
CONV_W = 3

LANES = 128
SUBLANES = 8
MXU_DIM = 256
VMEM_BYTES = 64 * 1024 * 1024

HEAD_W = QK_NOPE + 2 * QK_ROPE
GROUPS_PER_MXU = MXU_DIM // SSM_GROUP
STATE_W = GROUPS_PER_MXU * SSM_STATE

ATT_TQ = 512
ATT_HEADS = 2
VT_ROWS = V_HEAD + SUBLANES
FFN_TF = 1024
FFN_SUB = 2
S5_SUB = 2
S5_SCAN_W = 512


_NT = (((1,), (1,)), ((), ()))


def _vmem_limit(nbytes):
    return int(min(nbytes + (4 << 20), VMEM_BYTES - (4 << 20)))


def _rms(x, g):
    x = x.astype(F32)
    return x * lax.rsqrt(jnp.mean(x * x, axis=-1, keepdims=True) + EPS) * g


def _const_spec(shape):
    nd = len(shape)
    return pl.BlockSpec(shape, lambda *_: (0,) * nd, pipeline_mode=pl.Buffered(1))


def _inproj_kernel(x_ref, g_ref, win_ref, qg_ref, wq_ref, kvg_ref, wk_ref, wv_ref, vone_ref,
                   cs_ref, u_ref, q_ref, k_ref, v_ref, *, scale):
    d_ssm = u_ref.shape[1]
    q_lora = wq_ref.shape[0]
    kv_lora = wk_ref.shape[0]
    xn = _rms(x_ref[...], g_ref[...]).astype(BF16)
    z = jnp.dot(xn, win_ref[...], preferred_element_type=F32)
    o1, o2, o3 = d_ssm, d_ssm + q_lora, d_ssm + q_lora + kv_lora
    u_ref[...] = z[:, :o1].astype(u_ref.dtype)

    cs = cs_ref[...]
    qn = _rms(z[:, o1:o2], qg_ref[...]).astype(BF16)
    q = jnp.dot(qn, wq_ref[...], preferred_element_type=F32)
    cs_q = cs * scale
    for h in range(HEADS):
        c0 = h * HEAD_W
        q_ref[:, c0:c0 + QK_NOPE] = (q[:, c0:c0 + QK_NOPE] * scale).astype(q_ref.dtype)
        q_ref[:, c0 + QK_NOPE:c0 + HEAD_W] = (q[:, c0 + QK_NOPE:c0 + HEAD_W] * cs_q).astype(q_ref.dtype)

    kvn = _rms(z[:, o2:o3], kvg_ref[...]).astype(BF16)
    kn = jnp.dot(kvn, wk_ref[...], preferred_element_type=F32)
    vt = lax.dot_general(wv_ref[...], kvn, _NT, preferred_element_type=F32)
    v_ref[...] = (vt + vone_ref[...]).astype(v_ref.dtype)
    kpe = z[:, o3:o3 + QK_ROPE]
    x1, x2 = kpe[:, :QK_ROPE // 2], kpe[:, QK_ROPE // 2:]
    t = jnp.concatenate([kpe, -x2, x1], axis=1) * cs
    krot = (t + pltpu.roll(t, QK_ROPE, 1)).astype(k_ref.dtype)
    for h in range(HEADS):
        c0 = h * HEAD_W
        k_ref[:, c0:c0 + QK_NOPE] = kn[:, h * QK_NOPE:(h + 1) * QK_NOPE].astype(k_ref.dtype)
        k_ref[:, c0 + QK_NOPE:c0 + HEAD_W] = krot


def _inproj(x2d, cs, w, nb, nt, tm):
    rows, d = x2d.shape
    d_ssm = w["d_ssm"]
    n_in = w["win"].shape[1]
    row_spec = lambda width: pl.BlockSpec((tm, width), lambda b, j: (b * nt + j, 0))
    in_specs = [
        row_spec(d),
        _const_spec((1, d)),
        _const_spec(w["win"].shape),
        _const_spec(w["qg"].shape),
        _const_spec(w["wq"].shape),
        _const_spec(w["kvg"].shape),
        _const_spec(w["wk"].shape),
        _const_spec(w["wv"].shape),
        _const_spec(w["vone"].shape),
        pl.BlockSpec((tm, LANES), lambda b, j: (j, 0)),
    ]
    out_shape = (
        jax.ShapeDtypeStruct((rows, d_ssm), BF16),
        jax.ShapeDtypeStruct((rows, HEADS * HEAD_W), BF16),
        jax.ShapeDtypeStruct((rows, HEADS * HEAD_W), BF16),
        jax.ShapeDtypeStruct((nb, HEADS * VT_ROWS, nt * tm), BF16),
    )
    out_specs = (
        row_spec(d_ssm),
        row_spec(HEADS * HEAD_W),
        row_spec(HEADS * HEAD_W),
        pl.BlockSpec((None, HEADS * VT_ROWS, tm), lambda b, j: (b, 0, j)),
    )
    weights = 2 * (w["win"].size + w["wq"].size + w["wk"].size + w["wv"].size)
    tiles = 2 * tm * (4 * d + 2 * d_ssm + 4 * HEADS * HEAD_W + 2 * HEADS * V_HEAD + 4 * LANES)
    temps = 4 * tm * (d + n_in)
    return pl.pallas_call(
        functools.partial(_inproj_kernel, scale=w["scale"]),
        grid=(nb, nt),
        in_specs=in_specs,
        out_specs=out_specs,
        out_shape=out_shape,
        compiler_params=pltpu.CompilerParams(
            dimension_semantics=("arbitrary", "arbitrary"),
            vmem_limit_bytes=_vmem_limit(weights + tiles + temps)),
        name="inproj",
    )(x2d, w["mix_g"], w["win"], w["qg"], w["wq"], w["kvg"], w["wk"], w["wv"], w["vone"], cs)


def _s5_kernel(u_ref, h0_ref, perm_ref, bm_ref, cm_ref, are_ref, aim_ref, dskip_ref, wglu_ref, bglu_ref,
               y_ref, hT_ref, hbuf, state, *, steps):
    n_gb = bm_ref.shape[0]
    sw = are_ref.shape[2]
    cin = bm_ref.shape[1]
    nb, _, d_ssm = u_ref.shape

    @pl.when(pl.program_id(0) == 0)
    def _():
        state[...] = h0_ref[...]

    u = jnp.dot(perm_ref[0], u_ref[...].reshape(nb * steps, d_ssm),
                preferred_element_type=F32).astype(BF16)

    sub = steps // S5_SUB
    span = lambda k: slice(k * sub * SUBLANES, (k + 1) * sub * SUBLANES)
    for k in range(S5_SUB):
        for gb in range(n_gb):
            hbuf[span(k), 2 * sw * gb:2 * sw * (gb + 1)] = jnp.dot(
                u[span(k), cin * gb:cin * (gb + 1)], bm_ref[gb], preferred_element_type=F32)

    blocks = [(2 * sw * gb + o, 2 * sw * gb + sw + o, gb, o)
              for gb in range(n_gb) for o in range(0, sw, S5_SCAN_W)]
    carry = [(state[:, c_re:c_re + S5_SCAN_W], state[:, c_im:c_im + S5_SCAN_W])
             for c_re, c_im, _, _ in blocks]
    for k in range(S5_SUB):
        for j, (c_re, c_im, gb, o) in enumerate(blocks):
            are = are_ref[gb, :, o:o + S5_SCAN_W]
            aim = aim_ref[gb, :, o:o + S5_SCAN_W]
            hre, him = carry[j]
            for t in range(k * sub, (k + 1) * sub):
                r0 = t * SUBLANES
                nre = are * hre - aim * him + hbuf[r0:r0 + SUBLANES, c_re:c_re + S5_SCAN_W]
                nim = are * him + aim * hre + hbuf[r0:r0 + SUBLANES, c_im:c_im + S5_SCAN_W]
                hbuf[r0:r0 + SUBLANES, c_re:c_re + S5_SCAN_W] = nre
                hbuf[r0:r0 + SUBLANES, c_im:c_im + S5_SCAN_W] = nim
                hre, him = nre, nim
            carry[j] = (hre, him)
    for (c_re, c_im, _, _), (hre, him) in zip(blocks, carry):
        state[:, c_re:c_re + S5_SCAN_W] = hre
        state[:, c_im:c_im + S5_SCAN_W] = him

    outs = []
    for k in range(S5_SUB):
        ys = [jnp.dot(hbuf[span(k), 2 * sw * gb:2 * sw * (gb + 1)].astype(BF16), cm_ref[gb],
                      preferred_element_type=F32) for gb in range(n_gb)]
        y = jnp.concatenate(ys, axis=1) + dskip_ref[...] * u[span(k)].astype(F32)
        g = jax.nn.gelu(y)
        gate = jnp.dot(g.astype(BF16), wglu_ref[...], preferred_element_type=F32) + bglu_ref[...]
        outs.append((g * jax.nn.sigmoid(gate)).astype(BF16))
    out = jnp.concatenate(outs, axis=0)
    y_ref[...] = jnp.dot(perm_ref[1], out, preferred_element_type=F32).astype(y_ref.dtype).reshape(y_ref.shape)
    hT_ref[...] = state[...]


def _s5(u, h0, w, steps):
    nb, t_len, d_ssm = u.shape
    assert nb == SUBLANES
    r = steps * SUBLANES
    n_state = h0.shape[1]
    src = (jnp.arange(r) % SUBLANES) * steps + jnp.arange(r) // SUBLANES
    fwd = (src[:, None] == jnp.arange(r)[None, :]).astype(BF16)
    perm = jnp.stack([fwd, fwd.T])
    blk = pl.BlockSpec((nb, steps, d_ssm), lambda i: (0, i, 0))
    in_specs = [
        blk,
        _const_spec(h0.shape),
        _const_spec(perm.shape),
        _const_spec(w["bm"].shape),
        _const_spec(w["cm"].shape),
        _const_spec(w["are"].shape),
        _const_spec(w["aim"].shape),
        _const_spec((1, d_ssm)),
        _const_spec(w["wglu"].shape),
        _const_spec((1, d_ssm)),
    ]
    out_shape = (jax.ShapeDtypeStruct(u.shape, BF16),
                 jax.ShapeDtypeStruct(h0.shape, F32))
    out_specs = (blk, pl.BlockSpec(h0.shape, lambda i: (0, 0)))
    weights = (2 * (w["bm"].size + w["cm"].size + w["wglu"].size + perm.size)
               + 4 * (w["are"].size + w["aim"].size))
    tiles = 2 * r * d_ssm * 2 * 2 + 3 * 4 * h0.size
    scratch = 4 * r * n_state + 4 * h0.size
    temps = 2 * 4 * r * d_ssm
    return pl.pallas_call(
        functools.partial(_s5_kernel, steps=steps),
        grid=(t_len // steps,),
        in_specs=in_specs,
        out_specs=out_specs,
        out_shape=out_shape,
        scratch_shapes=[pltpu.VMEM((r, n_state), F32), pltpu.VMEM(h0.shape, F32)],
        compiler_params=pltpu.CompilerParams(
            dimension_semantics=("arbitrary",),
            vmem_limit_bytes=_vmem_limit(weights + tiles + scratch + temps)),
        name="s5",
    )(u, h0, perm, w["bm"], w["cm"], w["are"], w["aim"], w["dskip"], w["wglu"], w["bglu"])


def _attn_kernel(q_ref, k_ref, vt_ref, km_ref, vmt_ref, o_ref):
    seq = q_ref.shape[0]
    half = ATT_TQ // 2
    neg = jnp.finfo(F32).min
    kc = lax.broadcasted_iota(jnp.int32, (half, half), 0) // CHUNK
    qc = lax.broadcasted_iota(jnp.int32, (half, half), 1) // CHUNK
    visible = kc <= qc
    colmax = lambda t: jnp.max(t, axis=0, keepdims=True)
    prob = lambda s, m: jnp.exp2(s - m).astype(BF16)
    order = [(hh, i) for i in range(seq // ATT_TQ - 1, -1, -1) for hh in range(ATT_HEADS)]
    scores = {}
    for hh, i in order:
        r0 = i * ATT_TQ
        qs = slice(hh * HEAD_W, (hh + 1) * HEAD_W)
        q = q_ref[r0:r0 + ATT_TQ, qs]
        s_meta = lax.dot_general(km_ref[:, qs], q, _NT, preferred_element_type=F32)
        s_diag = lax.dot_general(k_ref[r0:r0 + ATT_TQ, qs], q, _NT, preferred_element_type=F32)
        s_full = (lax.dot_general(k_ref[0:r0, qs], q, _NT, preferred_element_type=F32)
                  if i > 0 else None)
        scores[hh, i] = (s_meta, s_diag, s_full)
    for hh, i in order:
        r0 = i * ATT_TQ
        vs = slice(hh * VT_ROWS, (hh + 1) * VT_ROWS)
        vmt = vmt_ref[vs, :]
        s_meta, s_diag, s_full = scores[hh, i]
        s00 = jnp.where(visible, s_diag[:half, :half], neg)
        s01 = s_diag[:half, half:]
        s11 = jnp.where(visible, s_diag[half:, half:], neg)
        m = jnp.concatenate([colmax(s00), jnp.maximum(colmax(s01), colmax(s11))], axis=1)
        m = jnp.maximum(m, colmax(s_meta))
        if i > 0:
            m = jnp.maximum(m, colmax(s_full))
        p_meta = prob(s_meta, m)
        p00 = prob(s00, m[:, :half])
        p01 = prob(s01, m[:, half:])
        p11 = prob(s11, m[:, half:])
        p_diag = jnp.concatenate([jnp.concatenate([p00, p01], axis=1),
                                  jnp.concatenate([jnp.zeros_like(p11), p11], axis=1)], axis=0)
        o = jnp.dot(vmt, p_meta, preferred_element_type=F32)
        o = o + jnp.dot(vt_ref[vs, r0:r0 + ATT_TQ], p_diag, preferred_element_type=F32)
        if i > 0:
            o = o + jnp.dot(vt_ref[vs, 0:r0], prob(s_full, m), preferred_element_type=F32)
        out = o[:V_HEAD] / o[V_HEAD:V_HEAD + 1]
        o_ref[r0:r0 + ATT_TQ, hh * V_HEAD:(hh + 1) * V_HEAD] = out.T.astype(o_ref.dtype)


def _attention(q, k, vt, km, vmt, nb, seq):
    assert ATT_TQ % CHUNK == 0 and seq % ATT_TQ == 0 and V_HEAD == LANES and HEADS % ATT_HEADS == 0
    n_meta = km.shape[0]
    hw, vr = ATT_HEADS * HEAD_W, ATT_HEADS * VT_ROWS
    in_specs = [
        pl.BlockSpec((seq, hw), lambda b, h: (b, h)),
        pl.BlockSpec((seq, hw), lambda b, h: (b, h)),
        pl.BlockSpec((None, vr, seq), lambda b, h: (b, h, 0)),
        pl.BlockSpec((n_meta, hw), lambda b, h: (0, h)),
        pl.BlockSpec((vr, n_meta), lambda b, h: (h, 0)),
    ]
    tiles = 2 * 2 * seq * (2 * hw + vr + ATT_HEADS * V_HEAD)
    temps = 4 * 4 * seq * ATT_TQ * ATT_HEADS
    return pl.pallas_call(
        _attn_kernel,
        grid=(nb, HEADS // ATT_HEADS),
        in_specs=in_specs,
        out_specs=pl.BlockSpec((seq, ATT_HEADS * V_HEAD), lambda b, h: (b, h)),
        out_shape=jax.ShapeDtypeStruct((nb * seq, HEADS * V_HEAD), BF16),
        compiler_params=pltpu.CompilerParams(
            dimension_semantics=("arbitrary", "arbitrary"),
            vmem_limit_bytes=_vmem_limit(tiles + temps)),
        name="attn",
    )(q, k, vt, km, vmt)


def _attn_meta_kernel(q_ref, k_ref, vt_ref, o_ref):
    for h in range(HEADS):
        s = lax.dot_general(q_ref[:, h * HEAD_W:(h + 1) * HEAD_W],
                            k_ref[:, h * HEAD_W:(h + 1) * HEAD_W], _NT, preferred_element_type=F32)
        p = jnp.exp2(s - jnp.max(s, axis=1, keepdims=True))
        pv = lax.dot_general(p.astype(BF16), vt_ref[h * VT_ROWS:h * VT_ROWS + V_HEAD, :], _NT,
                             preferred_element_type=F32)
        o_ref[:, h * V_HEAD:(h + 1) * V_HEAD] = (pv / jnp.sum(p, axis=1, keepdims=True)).astype(o_ref.dtype)


def _attention_meta(q, k, vt):
    return pl.pallas_call(
        _attn_meta_kernel,
        out_shape=jax.ShapeDtypeStruct((q.shape[0], HEADS * V_HEAD), BF16),
        name="attn_meta",
    )(q, k, vt)


def _outproj_kernel(ya_ref, yb_ref, x_ref, ga_ref, gb_ref, wa_ref, wb_ref, o_ref):
    a = _rms(ya_ref[...], ga_ref[...]).astype(BF16)
    b = _rms(yb_ref[...], gb_ref[...]).astype(BF16)
    mix = jnp.dot(a, wa_ref[...], preferred_element_type=F32)
    mix = mix + jnp.dot(b, wb_ref[...], preferred_element_type=F32)
    o_ref[...] = x_ref[...].astype(F32) + mix


def _outproj(ya, yb, x2d, w, nb, nt, tm):
    rows, d = x2d.shape
    d_ssm, d_attn = w["wo_a"].shape[0], w["wo_b"].shape[0]
    in_specs = [
        pl.BlockSpec((tm, d_ssm), lambda b, j: (b * nt + j, 0)),
        pl.BlockSpec((tm, d_attn), lambda b, j: (b * nt + j, 0)),
        pl.BlockSpec((tm, d), lambda b, j: (b * nt + j, 0)),
        _const_spec((1, d_ssm)),
        _const_spec((1, d_attn)),
        _const_spec(w["wo_a"].shape),
        _const_spec(w["wo_b"].shape),
    ]
    weights = 2 * (w["wo_a"].size + w["wo_b"].size)
    tiles = 2 * tm * (2 * d_ssm + 2 * d_attn + 4 * d + 4 * d)
    temps = 4 * tm * d
    return pl.pallas_call(
        _outproj_kernel,
        grid=(nb, nt),
        in_specs=in_specs,
        out_specs=pl.BlockSpec((tm, d), lambda b, j: (b * nt + j, 0)),
        out_shape=jax.ShapeDtypeStruct((rows, d), F32),
        compiler_params=pltpu.CompilerParams(
            dimension_semantics=("arbitrary", "arbitrary"),
            vmem_limit_bytes=_vmem_limit(weights + tiles + temps)),
        name="outproj",
    )(ya, yb, x2d, w["ga"], w["gb"], w["wo_a"], w["wo_b"])


def _gate_kernel(h_ref, g_ref, wg_ref, o_ref):
    xn = _rms(h_ref[...], g_ref[...]).astype(BF16)
    o_ref[...] = jnp.dot(xn, wg_ref[...], preferred_element_type=F32)


def _meta_gate(h_meta, w):
    rows, d = h_meta.shape
    n_chunks = w["conv_w"].shape[0]
    gate = pl.pallas_call(
        _gate_kernel,
        grid=(n_chunks,),
        in_specs=[_const_spec((rows, d)), _const_spec((1, d)),
                  pl.BlockSpec((d, FFN_TF), lambda j: (0, j))],
        out_specs=pl.BlockSpec((rows, FFN_TF), lambda j: (0, j)),
        out_shape=jax.ShapeDtypeStruct((rows, n_chunks * FFN_TF), F32),
        compiler_params=pltpu.CompilerParams(dimension_semantics=("arbitrary",)),
        name="meta_gate",
    )(h_meta, w["ffn_g"], w["wu"])
    halo = gate[rows - SUBLANES:].reshape(SUBLANES, n_chunks, FFN_TF)
    return jnp.swapaxes(halo, 0, 1)


def _shift_rows(x, prev, k):
    rolled = pltpu.roll(x, k, 0)
    head = jnp.concatenate([prev, x[:SUBLANES]], axis=0)[SUBLANES - k:2 * SUBLANES - k]
    return jnp.concatenate([head, rolled[SUBLANES:]], axis=0)


def _ffn_kernel(h_ref, g_ref, wu_hbm, wd_hbm, cw_ref, cb_ref, mh_ref, fg_ref, o_ref,
                xn_scr, halo_scr, wg_buf, wv_buf, wd_buf, sem, *, tiles_per_batch, d_ff):
    i = pl.program_id(0)
    tm = h_ref.shape[0]
    ts = tm // FFN_SUB
    n_full = d_ff // FFN_TF
    rem = d_ff - n_full * FFN_TF
    assert 0 < rem and rem % LANES == 0

    def chunk_copies(col0, width, slot):
        return (
            pltpu.make_async_copy(wu_hbm.at[:, pl.ds(col0, width)],
                                  wg_buf.at[slot, :, pl.ds(0, width)], sem.at[slot, 0]),
            pltpu.make_async_copy(wu_hbm.at[:, pl.ds(d_ff + col0, width)],
                                  wv_buf.at[slot, :, pl.ds(0, width)], sem.at[slot, 1]),
            pltpu.make_async_copy(wd_hbm.at[pl.ds(col0, width), :],
                                  wd_buf.at[slot, pl.ds(0, width), :], sem.at[slot, 2]),
        )

    def start(copies):
        for c in copies:
            c.start()

    def wait(copies):
        for c in copies:
            c.wait()

    ragged = lambda slot: chunk_copies(n_full * FFN_TF, rem, slot)
    full = lambda c, slot: chunk_copies(pl.multiple_of(c * FFN_TF, FFN_TF), FFN_TF, slot)

    def compute(idx, width, slot):
        wg = wg_buf[slot, :, :width]
        wv = wv_buf[slot, :, :width]
        wd = wd_buf[slot, :width, :]
        cw = cw_ref[idx][:, :width]
        cb = cb_ref[idx][:, :width]
        prev = halo_scr[idx][:, :width]
        for s in range(FFN_SUB):
            rows = pl.ds(s * ts, ts)
            xn = xn_scr[rows, :]
            gate = jnp.dot(xn, wg, preferred_element_type=F32)
            val = jnp.dot(xn, wv, preferred_element_type=F32)
            conv = (cw[0:1] * _shift_rows(gate, prev, 2)
                    + cw[1:2] * _shift_rows(gate, prev, 1)
                    + cw[2:3] * gate + cb)
            prev = gate[ts - SUBLANES:, :]
            act = (jax.nn.silu(conv) * val).astype(BF16)
            o_ref[rows, :] += jnp.dot(act, wd, preferred_element_type=F32)
        halo_scr[idx, :, pl.ds(0, width)] = prev

    assert n_full >= 2
    slot0 = (i * (n_full + 1)) % 2

    @pl.when(i == 0)
    def _():
        start(full(0, slot0))

    @pl.when((i % tiles_per_batch) == 0)
    def _():
        halo_scr[...] = mh_ref[...]

    h = h_ref[...]
    xn_scr[...] = _rms(h, g_ref[...]).astype(BF16)
    o_ref[...] = h

    def body(c, carry):
        slot = (slot0 + c) % 2
        wait(full(c, slot))

        @pl.when(c + 1 < n_full)
        def _():
            start(full(c + 1, 1 - slot))

        @pl.when(c + 1 == n_full)
        def _():
            start(ragged(1 - slot))

        compute(c, FFN_TF, slot)
        return carry

    lax.fori_loop(0, n_full, body, 0)

    slot_r = (slot0 + n_full) % 2
    wait(ragged(slot_r))

    @pl.when(i + 1 < pl.num_programs(0))
    def _():
        start(full(0, 1 - slot_r))

    compute(n_full, rem, slot_r)
    o_ref[...] = _rms(o_ref[...], fg_ref[...])


def _ffn(h1, meta_halo, w, tiles_per_batch, tm):
    rows, d = h1.shape
    d_ff = w["wd"].shape[0]
    n_chunks = w["conv_w"].shape[0]
    in_specs = [
        pl.BlockSpec((tm, d), lambda i: (i, 0)),
        _const_spec((1, d)),
        pl.BlockSpec(memory_space=pl.ANY),
        pl.BlockSpec(memory_space=pl.ANY),
        _const_spec(w["conv_w"].shape),
        _const_spec(w["conv_b"].shape),
        _const_spec(meta_halo.shape),
        _const_spec((1, d)),
    ]
    slots = 2
    tiles = 2 * 2 * 4 * tm * d + 4 * (w["conv_w"].size + w["conv_b"].size + meta_halo.size)
    scratch = 2 * tm * d + 4 * meta_halo.size + slots * 3 * 2 * d * FFN_TF
    temps = 4 * (tm // FFN_SUB) * FFN_TF * 16
    return pl.pallas_call(
        functools.partial(_ffn_kernel, tiles_per_batch=tiles_per_batch, d_ff=d_ff),
        grid=(rows // tm,),
        in_specs=in_specs,
        out_specs=pl.BlockSpec((tm, d), lambda i: (i, 0)),
        out_shape=jax.ShapeDtypeStruct((rows, d), F32),
        scratch_shapes=[
            pltpu.VMEM((tm, d), BF16),
            pltpu.VMEM((n_chunks, SUBLANES, FFN_TF), F32),
            pltpu.VMEM((slots, d, FFN_TF), BF16),
            pltpu.VMEM((slots, d, FFN_TF), BF16),
            pltpu.VMEM((slots, FFN_TF, d), BF16),
            pltpu.SemaphoreType.DMA((slots, 3)),
        ],
        compiler_params=pltpu.CompilerParams(
            dimension_semantics=("arbitrary",),
            vmem_limit_bytes=_vmem_limit(tiles + scratch + temps)),
        name="ffn",
    )(h1, w["ffn_g"], w["wu"], w["wd"], w["conv_w"], w["conv_b"], meta_halo, w["final_g"])


def _swap_halves(w):
    half = w.shape[-1] // 2
    return jnp.concatenate([-w[..., half:], w[..., :half]], axis=-1)


def _block_diag(blocks, n):
    rows, c = blocks.shape[-2:]
    r = rows // n
    rep = (jnp.arange(c)[:, None] == jnp.arange(n * c)[None, :] % c).astype(blocks.dtype)
    tiled = jnp.einsum("...rc,cq->...rq", blocks, rep, precision=lax.Precision.HIGHEST)
    keep = (jnp.arange(rows)[:, None] // r) == (jnp.arange(n * c)[None, :] // c)
    return jnp.where(keep, tiled, 0.0)


def _prepare(p):
    d_ssm = p["d_skip"].shape[-1]
    q_lora = p["q_a_norm"].shape[-1]
    kv_lora = p["kv_a_norm"].shape[-1]
    d_ff = p["w_down"].shape[0]
    w = {"d_ssm": d_ssm, "scale": math.log2(math.e) / math.sqrt(QK_NOPE + QK_ROPE)}

    row = lambda v: v.reshape(1, -1).astype(F32)
    w["mix_g"] = row(p["mix_norm"])
    w["win"] = p["w_in"].astype(BF16)
    w["qg"] = row(p["q_a_norm"])
    w["kvg"] = row(p["kv_a_norm"])
    wq = p["w_q_b"].reshape(q_lora, HEADS, QK_NOPE + QK_ROPE)
    wq_pe = wq[..., QK_NOPE:]
    w["wq"] = jnp.concatenate([wq, _swap_halves(wq_pe)], axis=-1).reshape(q_lora, HEADS * HEAD_W).astype(BF16)
    wkv = p["w_kv_b"].reshape(kv_lora, HEADS, QK_NOPE + V_HEAD)
    w["wk"] = wkv[..., :QK_NOPE].reshape(kv_lora, HEADS * QK_NOPE).astype(BF16)
    wv = jnp.pad(jnp.moveaxis(wkv[..., QK_NOPE:], 0, 2), ((0, 0), (0, VT_ROWS - V_HEAD), (0, 0)))
    w["wv"] = wv.reshape(HEADS * VT_ROWS, kv_lora).astype(BF16)
    w["vone"] = jnp.tile((jnp.arange(VT_ROWS) == V_HEAD).astype(F32), HEADS).reshape(-1, 1)

    lam_re, lam_im = p["lam_re"].astype(F32), p["lam_im"].astype(F32)
    dt = jnp.exp(p["log_dt"].astype(F32))[:, None]
    mag = jnp.exp(lam_re * dt)
    a_re, a_im = mag * jnp.cos(lam_im * dt), mag * jnp.sin(lam_im * dt)
    den = lam_re * lam_re + lam_im * lam_im
    f_re = ((a_re - 1.0) * lam_re + a_im * lam_im) / den
    f_im = (a_im * lam_re - (a_re - 1.0) * lam_im) / den
    b_re, b_im = p["b_re"].astype(F32), p["b_im"].astype(F32)
    bb_re = f_re[..., None] * b_re - f_im[..., None] * b_im
    bb_im = f_re[..., None] * b_im + f_im[..., None] * b_re
    n_g = lam_re.shape[0]
    n_gb = n_g // GROUPS_PER_MXU
    bd = lambda t: _block_diag(t.reshape(n_gb, GROUPS_PER_MXU * t.shape[1], t.shape[2]), GROUPS_PER_MXU)
    w["bm"] = jnp.concatenate([bd(jnp.swapaxes(bb_re, 1, 2)),
                               bd(jnp.swapaxes(bb_im, 1, 2))], axis=2).astype(BF16)
    c_re, c_im = p["c_re"].astype(F32), p["c_im"].astype(F32)
    w["cm"] = jnp.concatenate([bd(jnp.swapaxes(c_re, 1, 2)),
                               bd(jnp.swapaxes(-c_im, 1, 2))], axis=1).astype(BF16)
    bcast = lambda a: jnp.broadcast_to(a.reshape(n_gb, 1, STATE_W), (n_gb, SUBLANES, STATE_W))
    w["are"], w["aim"] = bcast(a_re), bcast(a_im)
    w["dskip"] = row(p["d_skip"])
    w["wglu"] = p["w_glu"].astype(BF16)
    w["bglu"] = row(p["b_glu"])

    w["ga"], w["gb"] = row(p["out_norm_ssm"]), row(p["out_norm_attn"])
    w["wo_a"] = p["w_out"][:d_ssm].astype(BF16)
    w["wo_b"] = p["w_out"][d_ssm:].astype(BF16)

    n_chunks = -(-d_ff // FFN_TF)
    pad = n_chunks * FFN_TF - d_ff
    w["wu"] = p["w_up"].astype(BF16)
    w["wd"] = p["w_down"].astype(BF16)
    per_chunk = lambda a: jnp.swapaxes(
        jnp.pad(a.astype(F32), ((0, 0), (0, pad))).reshape(a.shape[0], n_chunks, FFN_TF), 0, 1)
    w["conv_w"] = per_chunk(p["conv_w"])
    w["conv_b"] = per_chunk(p["conv_b"].reshape(1, -1))
    w["ffn_g"] = row(p["ffn_norm"])
    w["final_g"] = row(p["final_norm"])
    return w


def _rope_table(n_pos):
    pos = jnp.arange(n_pos, dtype=F32)
    inv_freq = 1.0 / (ROPE_BASE ** (jnp.arange(0, QK_ROPE, 2, dtype=F32) / QK_ROPE))
    ang = pos[:, None] * inv_freq[None, :]
    cos, sin = jnp.cos(ang), jnp.sin(ang)
    return jnp.concatenate([cos, cos, sin, sin], axis=1)


def kernel(x, meta_tokens, mix_norm, w_in, lam_re, lam_im, log_dt, b_re, b_im, c_re, c_im, d_skip, w_glu, b_glu, q_a_norm, w_q_b, kv_a_norm, w_kv_b, out_norm_ssm, out_norm_attn, w_out, ffn_norm, w_up, conv_w, conv_b, w_down, final_norm):
    bsz, seq, d = x.shape
    assert meta_tokens.shape == (N_META, d) and bsz == SUBLANES
    p = dict(mix_norm=mix_norm[0], w_in=w_in[0], lam_re=lam_re[0], lam_im=lam_im[0], log_dt=log_dt[0],
             b_re=b_re[0], b_im=b_im[0], c_re=c_re[0], c_im=c_im[0], d_skip=d_skip[0],
             w_glu=w_glu[0], b_glu=b_glu[0], q_a_norm=q_a_norm[0], w_q_b=w_q_b[0],
             kv_a_norm=kv_a_norm[0], w_kv_b=w_kv_b[0], out_norm_ssm=out_norm_ssm[0],
             out_norm_attn=out_norm_attn[0], w_out=w_out[0], ffn_norm=ffn_norm[0], w_up=w_up[0],
             conv_w=conv_w[0], conv_b=conv_b[0], w_down=w_down[0], final_norm=final_norm)
    w = _prepare(p)
    d_ssm = w["d_ssm"]
    cs = _rope_table(N_META + seq)
    x2d = x.reshape(bsz * seq, d)
    meta = meta_tokens.astype(x.dtype)

    u_m, q_m, k_m, v_m = _inproj(meta, cs[:N_META], w, 1, 1, N_META)
    n_state = 2 * STATE_W * w["bm"].shape[0]
    ya_m8, h_meta_state = _s5(jnp.broadcast_to(u_m[None], (SUBLANES, N_META, d_ssm)),
                              jnp.zeros((SUBLANES, n_state), F32), w, N_META)
    ya_m = ya_m8[0]
    yb_m = _attention_meta(q_m, k_m, v_m[0])
    h1_m = _outproj(ya_m, yb_m, meta, w, 1, 1, N_META)
    meta_halo = _meta_gate(h1_m, w)

    tm = 512
    nt = seq // tm
    u, q, k, v = _inproj(x2d, cs[N_META:], w, bsz, nt, tm)
    ya, _ = _s5(u.reshape(bsz, seq, d_ssm), h_meta_state, w, 64)
    yb = _attention(q, k, v, k_m, v_m[0], bsz, seq)
    h1 = _outproj(ya.reshape(bsz * seq, d_ssm), yb, x2d, w, bsz, nt, tm)
    out = _ffn(h1, meta_halo, w, nt, tm)
    return out.reshape(bsz, seq, d)
```

```python
import functools
import math

import jax
import jax.numpy as jnp
from jax import lax
from jax.experimental import pallas as pl
from jax.experimental.pallas import tpu as pltpu

F32 = jnp.float32
BF16 = jnp.bfloat16

EPS = 1e-6
CHUNK = 64
N_META = 16
SSM_GROUP = 16
SSM_STATE = 64
HEADS = 8
QK_NOPE = 128
QK_ROPE = 64
V_HEAD = 128
ROPE_BASE = 10000.0
CONV_W = 3

LANES = 128
SUBLANES = 8
MXU_DIM = 256
VMEM_BYTES = 64 * 1024 * 1024

HEAD_W = QK_NOPE + 2 * QK_ROPE
GROUPS_PER_MXU = MXU_DIM // SSM_GROUP
STATE_W = GROUPS_PER_MXU * SSM_STATE

ATT_TQ = 512
ATT_HEADS = 2
VT_ROWS = V_HEAD + SUBLANES
FFN_TF = 1024
FFN_SUB = 2
S5_SUB = 2
S5_SCAN_W = 512


_NT = (((1,), (1,)), ((), ()))


def _vmem_limit(nbytes):
    return int(min(nbytes + (4 << 20), VMEM_BYTES - (4 << 20)))


def _rms(x, g):
    x = x.astype(F32)
    return x * lax.rsqrt(jnp.mean(x * x, axis=-1, keepdims=True) + EPS) * g


def _const_spec(shape):
    nd = len(shape)
    return pl.BlockSpec(shape, lambda *_: (0,) * nd, pipeline_mode=pl.Buffered(1))


def _inproj_kernel(x_ref, g_ref, win_ref, qg_ref, wq_ref, kvg_ref, wk_ref, wv_ref, vone_ref,
                   cs_ref, u_ref, q_ref, k_ref, v_ref, *, scale):
    d_ssm = u_ref.shape[1]
    q_lora = wq_ref.shape[0]
    kv_lora = wk_ref.shape[0]
    xn = _rms(x_ref[...], g_ref[...]).astype(BF16)
    z = jnp.dot(xn, win_ref[...], preferred_element_type=F32)
    o1, o2, o3 = d_ssm, d_ssm + q_lora, d_ssm + q_lora + kv_lora
    u_ref[...] = z[:, :o1].astype(u_ref.dtype)

    cs = cs_ref[...]
    qn = _rms(z[:, o1:o2], qg_ref[...]).astype(BF16)
    q = jnp.dot(qn, wq_ref[...], preferred_element_type=F32)
    cs_q = cs * scale
    for h in range(HEADS):
        c0 = h * HEAD_W
        q_ref[:, c0:c0 + QK_NOPE] = (q[:, c0:c0 + QK_NOPE] * scale).astype(q_ref.dtype)
        q_ref[:, c0 + QK_NOPE:c0 + HEAD_W] = (q[:, c0 + QK_NOPE:c0 + HEAD_W] * cs_q).astype(q_ref.dtype)

    kvn = _rms(z[:, o2:o3], kvg_ref[...]).astype(BF16)
    kn = jnp.dot(kvn, wk_ref[...], preferred_element_type=F32)
    vt = lax.dot_general(wv_ref[...], kvn, _NT, preferred_element_type=F32)
    v_ref[...] = (vt + vone_ref[...]).astype(v_ref.dtype)
    kpe = z[:, o3:o3 + QK_ROPE]
    x1, x2 = kpe[:, :QK_ROPE // 2], kpe[:, QK_ROPE // 2:]
    t = jnp.concatenate([kpe, -x2, x1], axis=1) * cs
    krot = (t + pltpu.roll(t, QK_ROPE, 1)).astype(k_ref.dtype)
    for h in range(HEADS):
        c0 = h * HEAD_W
        k_ref[:, c0:c0 + QK_NOPE] = kn[:, h * QK_NOPE:(h + 1) * QK_NOPE].astype(k_ref.dtype)
        k_ref[:, c0 + QK_NOPE:c0 + HEAD_W] = krot


def _inproj(x2d, cs, w, nb, nt, tm):
    rows, d = x2d.shape
    d_ssm = w["d_ssm"]
    n_in = w["win"].shape[1]
    row_spec = lambda width: pl.BlockSpec((tm, width), lambda b, j: (b * nt + j, 0))
    in_specs = [
        row_spec(d),
        _const_spec((1, d)),
        _const_spec(w["win"].shape),
        _const_spec(w["qg"].shape),
        _const_spec(w["wq"].shape),
        _const_spec(w["kvg"].shape),
        _const_spec(w["wk"].shape),
        _const_spec(w["wv"].shape),
        _const_spec(w["vone"].shape),
        pl.BlockSpec((tm, LANES), lambda b, j: (j, 0)),
    ]
    out_shape = (
        jax.ShapeDtypeStruct((rows, d_ssm), BF16),
        jax.ShapeDtypeStruct((rows, HEADS * HEAD_W), BF16),
        jax.ShapeDtypeStruct((rows, HEADS * HEAD_W), BF16),
        jax.ShapeDtypeStruct((nb, HEADS * VT_ROWS, nt * tm), BF16),
    )
    out_specs = (
        row_spec(d_ssm),
        row_spec(HEADS * HEAD_W),
        row_spec(HEADS * HEAD_W),
        pl.BlockSpec((None, HEADS * VT_ROWS, tm), lambda b, j: (b, 0, j)),
    )
    weights = 2 * (w["win"].size + w["wq"].size + w["wk"].size + w["wv"].size)
    tiles = 2 * tm * (4 * d + 2 * d_ssm + 4 * HEADS * HEAD_W + 2 * HEADS * V_HEAD + 4 * LANES)
    temps = 4 * tm * (d + n_in)
    return pl.pallas_call(
        functools.partial(_inproj_kernel, scale=w["scale"]),
        grid=(nb, nt),
        in_specs=in_specs,
        out_specs=out_specs,
        out_shape=out_shape,
        compiler_params=pltpu.CompilerParams(
            dimension_semantics=("arbitrary", "arbitrary"),
            vmem_limit_bytes=_vmem_limit(weights + tiles + temps)),
        name="inproj",
    )(x2d, w["mix_g"], w["win"], w["qg"], w["wq"], w["kvg"], w["wk"], w["wv"], w["vone"], cs)


def _s5_kernel(u_ref, h0_ref, perm_ref, bm_ref, cm_ref, are_ref, aim_ref, dskip_ref, wglu_ref, bglu_ref,
               y_ref, hT_ref, hbuf, state, *, steps):
    n_gb = bm_ref.shape[0]
    sw = are_ref.shape[2]
    cin = bm_ref.shape[1]
    nb, _, d_ssm = u_ref.shape

    @pl.when(pl.program_id(0) == 0)
    def _():
        state[...] = h0_ref[...]

    u = jnp.dot(perm_ref[0], u_ref[...].reshape(nb * steps, d_ssm),
                preferred_element_type=F32).astype(BF16)

    sub = steps // S5_SUB
    span = lambda k: slice(k * sub * SUBLANES, (k + 1) * sub * SUBLANES)
    for k in range(S5_SUB):
        for gb in range(n_gb):
            hbuf[span(k), 2 * sw * gb:2 * sw * (gb + 1)] = jnp.dot(
                u[span(k), cin * gb:cin * (gb + 1)], bm_ref[gb], preferred_element_type=F32)

    blocks = [(2 * sw * gb + o, 2 * sw * gb + sw + o, gb, o)
              for gb in range(n_gb) for o in range(0, sw, S5_SCAN_W)]
    carry = [(state[:, c_re:c_re + S5_SCAN_W], state[:, c_im:c_im + S5_SCAN_W])
             for c_re, c_im, _, _ in blocks]
    for k in range(S5_SUB):
        for j, (c_re, c_im, gb, o) in enumerate(blocks):
            are = are_ref[gb, :, o:o + S5_SCAN_W]
            aim = aim_ref[gb, :, o:o + S5_SCAN_W]
            hre, him = carry[j]
            for t in range(k * sub, (k + 1) * sub):
                r0 = t * SUBLANES
                nre = are * hre - aim * him + hbuf[r0:r0 + SUBLANES, c_re:c_re + S5_SCAN_W]
                nim = are * him + aim * hre + hbuf[r0:r0 + SUBLANES, c_im:c_im + S5_SCAN_W]
                hbuf[r0:r0 + SUBLANES, c_re:c_re + S5_SCAN_W] = nre
                hbuf[r0:r0 + SUBLANES, c_im:c_im + S5_SCAN_W] = nim
                hre, him = nre, nim
            carry[j] = (hre, him)
    for (c_re, c_im, _, _), (hre, him) in zip(blocks, carry):
        state[:, c_re:c_re + S5_SCAN_W] = hre
        state[:, c_im:c_im + S5_SCAN_W] = him

    outs = []
    for k in range(S5_SUB):
        ys = [jnp.dot(hbuf[span(k), 2 * sw * gb:2 * sw * (gb + 1)].astype(BF16), cm_ref[gb],
                      preferred_element_type=F32) for gb in range(n_gb)]
        y = jnp.concatenate(ys, axis=1) + dskip_ref[...] * u[span(k)].astype(F32)
        g = jax.nn.gelu(y)
        gate = jnp.dot(g.astype(BF16), wglu_ref[...], preferred_element_type=F32) + bglu_ref[...]
        outs.append((g * jax.nn.sigmoid(gate)).astype(BF16))
    out = jnp.concatenate(outs, axis=0)
    y_ref[...] = jnp.dot(perm_ref[1], out, preferred_element_type=F32).astype(y_ref.dtype).reshape(y_ref.shape)
    hT_ref[...] = state[...]


def _s5(u, h0, w, steps):
    nb, t_len, d_ssm = u.shape
    assert nb == SUBLANES
    r = steps * SUBLANES
    n_state = h0.shape[1]
    src = (jnp.arange(r) % SUBLANES) * steps + jnp.arange(r) // SUBLANES
    fwd = (src[:, None] == jnp.arange(r)[None, :]).astype(BF16)
    perm = jnp.stack([fwd, fwd.T])
    blk = pl.BlockSpec((nb, steps, d_ssm), lambda i: (0, i, 0))
    in_specs = [
        blk,
        _const_spec(h0.shape),
        _const_spec(perm.shape),
        _const_spec(w["bm"].shape),
        _const_spec(w["cm"].shape),
        _const_spec(w["are"].shape),
        _const_spec(w["aim"].shape),
        _const_spec((1, d_ssm)),
        _const_spec(w["wglu"].shape),
        _const_spec((1, d_ssm)),
    ]
    out_shape = (jax.ShapeDtypeStruct(u.shape, BF16),
                 jax.ShapeDtypeStruct(h0.shape, F32))
    out_specs = (blk, pl.BlockSpec(h0.shape, lambda i: (0, 0)))
    weights = (2 * (w["bm"].size + w["cm"].size + w["wglu"].size + perm.size)
               + 4 * (w["are"].size + w["aim"].size))
    tiles = 2 * r * d_ssm * 2 * 2 + 3 * 4 * h0.size
    scratch = 4 * r * n_state + 4 * h0.size
    temps = 2 * 4 * r * d_ssm
    return pl.pallas_call(
        functools.partial(_s5_kernel, steps=steps),
        grid=(t_len // steps,),
        in_specs=in_specs,
        out_specs=out_specs,
        out_shape=out_shape,
        scratch_shapes=[pltpu.VMEM((r, n_state), F32), pltpu.VMEM(h0.shape, F32)],
        compiler_params=pltpu.CompilerParams(
            dimension_semantics=("arbitrary",),
            vmem_limit_bytes=_vmem_limit(weights + tiles + scratch + temps)),
        name="s5",
    )(u, h0, perm, w["bm"], w["cm"], w["are"], w["aim"], w["dskip"], w["wglu"], w["bglu"])


def _attn_kernel(q_ref, k_ref, vt_ref, km_ref, vmt_ref, o_ref):
    seq = q_ref.shape[0]
    half = ATT_TQ // 2
    neg = jnp.finfo(F32).min
    kc = lax.broadcasted_iota(jnp.int32, (half, half), 0) // CHUNK
    qc = lax.broadcasted_iota(jnp.int32, (half, half), 1) // CHUNK
    visible = kc <= qc
    colmax = lambda t: jnp.max(t, axis=0, keepdims=True)
    prob = lambda s, m: jnp.exp2(s - m).astype(BF16)
    order = [(hh, i) for i in range(seq // ATT_TQ - 1, -1, -1) for hh in range(ATT_HEADS)]
    scores = {}
    for hh, i in order:
        r0 = i * ATT_TQ
        qs = slice(hh * HEAD_W, (hh + 1) * HEAD_W)
        q = q_ref[r0:r0 + ATT_TQ, qs]
        s_meta = lax.dot_general(km_ref[:, qs], q, _NT, preferred_element_type=F32)
        s_diag = lax.dot_general(k_ref[r0:r0 + ATT_TQ, qs], q, _NT, preferred_element_type=F32)
        s_full = (lax.dot_general(k_ref[0:r0, qs], q, _NT, preferred_element_type=F32)
                  if i > 0 else None)
        scores[hh, i] = (s_meta, s_diag, s_full)
    for hh, i in order:
        r0 = i * ATT_TQ
        vs = slice(hh * VT_ROWS, (hh + 1) * VT_ROWS)
        vmt = vmt_ref[vs, :]
        s_meta, s_diag, s_full = scores[hh, i]
        s00 = jnp.where(visible, s_diag[:half, :half], neg)
        s01 = s_diag[:half, half:]
        s11 = jnp.where(visible, s_diag[half:, half:], neg)
        m = jnp.concatenate([colmax(s00), jnp.maximum(colmax(s01), colmax(s11))], axis=1)
        m = jnp.maximum(m, colmax(s_meta))
        if i > 0:
            m = jnp.maximum(m, colmax(s_full))
        p_meta = prob(s_meta, m)
        p00 = prob(s00, m[:, :half])
        p01 = prob(s01, m[:, half:])
        p11 = prob(s11, m[:, half:])
        p_diag = jnp.concatenate([jnp.concatenate([p00, p01], axis=1),
                                  jnp.concatenate([jnp.zeros_like(p11), p11], axis=1)], axis=0)
        o = jnp.dot(vmt, p_meta, preferred_element_type=F32)
        o = o + jnp.dot(vt_ref[vs, r0:r0 + ATT_TQ], p_diag, preferred_element_type=F32)
        if i > 0:
            o = o + jnp.dot(vt_ref[vs, 0:r0], prob(s_full, m), preferred_element_type=F32)
        out = o[:V_HEAD] / o[V_HEAD:V_HEAD + 1]
        o_ref[r0:r0 + ATT_TQ, hh * V_HEAD:(hh + 1) * V_HEAD] = out.T.astype(o_ref.dtype)


def _attention(q, k, vt, km, vmt, nb, seq):
    assert ATT_TQ % CHUNK == 0 and seq % ATT_TQ == 0 and V_HEAD == LANES and HEADS % ATT_HEADS == 0
    n_meta = km.shape[0]
    hw, vr = ATT_HEADS * HEAD_W, ATT_HEADS * VT_ROWS
    in_specs = [
        pl.BlockSpec((seq, hw), lambda b, h: (b, h)),
        pl.BlockSpec((seq, hw), lambda b, h: (b, h)),
        pl.BlockSpec((None, vr, seq), lambda b, h: (b, h, 0)),
        pl.BlockSpec((n_meta, hw), lambda b, h: (0, h)),
        pl.BlockSpec((vr, n_meta), lambda b, h: (h, 0)),
    ]
    tiles = 2 * 2 * seq * (2 * hw + vr + ATT_HEADS * V_HEAD)
    temps = 4 * 4 * seq * ATT_TQ * ATT_HEADS
    return pl.pallas_call(
        _attn_kernel,
        grid=(nb, HEADS // ATT_HEADS),
        in_specs=in_specs,
        out_specs=pl.BlockSpec((seq, ATT_HEADS * V_HEAD), lambda b, h: (b, h)),
        out_shape=jax.ShapeDtypeStruct((nb * seq, HEADS * V_HEAD), BF16),
        compiler_params=pltpu.CompilerParams(
            dimension_semantics=("arbitrary", "arbitrary"),
            vmem_limit_bytes=_vmem_limit(tiles + temps)),
        name="attn",
    )(q, k, vt, km, vmt)


def _attn_meta_kernel(q_ref, k_ref, vt_ref, o_ref):
    for h in range(HEADS):
        s = lax.dot_general(q_ref[:, h * HEAD_W:(h + 1) * HEAD_W],
                            k_ref[:, h * HEAD_W:(h + 1) * HEAD_W], _NT, preferred_element_type=F32)
        p = jnp.exp2(s - jnp.max(s, axis=1, keepdims=True))
        pv = lax.dot_general(p.astype(BF16), vt_ref[h * VT_ROWS:h * VT_ROWS + V_HEAD, :], _NT,
                             preferred_element_type=F32)
        o_ref[:, h * V_HEAD:(h + 1) * V_HEAD] = (pv / jnp.sum(p, axis=1, keepdims=True)).astype(o_ref.dtype)


def _attention_meta(q, k, vt):
    return pl.pallas_call(
        _attn_meta_kernel,
        out_shape=jax.ShapeDtypeStruct((q.shape[0], HEADS * V_HEAD), BF16),
        name="attn_meta",
    )(q, k, vt)


def _outproj_kernel(ya_ref, yb_ref, x_ref, ga_ref, gb_ref, wa_ref, wb_ref, o_ref):
    a = _rms(ya_ref[...], ga_ref[...]).astype(BF16)
    b = _rms(yb_ref[...], gb_ref[...]).astype(BF16)
    mix = jnp.dot(a, wa_ref[...], preferred_element_type=F32)
    mix = mix + jnp.dot(b, wb_ref[...], preferred_element_type=F32)
    o_ref[...] = x_ref[...].astype(F32) + mix


def _outproj_cast_kernel(ya_ref, yb_ref, x_ref, ga_ref, gb_ref, wa_ref, wb_ref, src_ref, o_ref, dst_ref):
    _outproj_kernel(ya_ref, yb_ref, x_ref, ga_ref, gb_ref, wa_ref, wb_ref, o_ref)
    dst_ref[...] = src_ref[...].astype(dst_ref.dtype)


def _outproj(ya, yb, x2d, w, nb, nt, tm, cast_src=None):
    rows, d = x2d.shape
    d_ssm, d_attn = w["wo_a"].shape[0], w["wo_b"].shape[0]
    row_spec = lambda width: pl.BlockSpec((tm, width), lambda b, j: (b * nt + j, 0))
    in_specs = [
        row_spec(d_ssm),
        row_spec(d_attn),
        row_spec(d),
        _const_spec((1, d_ssm)),
        _const_spec((1, d_attn)),
        _const_spec(w["wo_a"].shape),
        _const_spec(w["wo_b"].shape),
    ]
    args = [ya, yb, x2d, w["ga"], w["gb"], w["wo_a"], w["wo_b"]]
    out_specs = row_spec(d)
    out_shape = jax.ShapeDtypeStruct((rows, d), F32)
    weights = 2 * (w["wo_a"].size + w["wo_b"].size)
    tiles = 2 * tm * (2 * d_ssm + 2 * d_attn + 4 * d + 4 * d)
    temps = 4 * tm * d
    body = _outproj_kernel
    if cast_src is not None:
        src_rows, src_cols = cast_src.shape
        blk = src_rows // (nb * nt)
        assert blk * nb * nt == src_rows and blk % (2 * SUBLANES) == 0
        cast_spec = pl.BlockSpec((blk, src_cols), lambda b, j: (b * nt + j, 0))
        in_specs.append(cast_spec)
        args.append(cast_src)
        out_specs = (out_specs, cast_spec)
        out_shape = (out_shape, jax.ShapeDtypeStruct(cast_src.shape, BF16))
        tiles += 2 * blk * src_cols * (4 + 2)
        body = _outproj_cast_kernel
    return pl.pallas_call(
        body,
        grid=(nb, nt),
        in_specs=in_specs,
        out_specs=out_specs,
        out_shape=out_shape,
        compiler_params=pltpu.CompilerParams(
            dimension_semantics=("arbitrary", "arbitrary"),
            vmem_limit_bytes=_vmem_limit(weights + tiles + temps)),
        name="outproj",
    )(*args)


def _gate_kernel(h_ref, g_ref, wg_ref, o_ref):
    xn = _rms(h_ref[...], g_ref[...]).astype(BF16)
    o_ref[...] = jnp.dot(xn, wg_ref[...], preferred_element_type=F32)


def _meta_gate(h_meta, wu, w):
    rows, d = h_meta.shape
    n_chunks = w["conv_w"].shape[0]
    gate = pl.pallas_call(
        _gate_kernel,
        grid=(n_chunks,),
        in_specs=[_const_spec((rows, d)), _const_spec((1, d)),
                  pl.BlockSpec((d, FFN_TF), lambda j: (0, j))],
        out_specs=pl.BlockSpec((rows, FFN_TF), lambda j: (0, j)),
        out_shape=jax.ShapeDtypeStruct((rows, n_chunks * FFN_TF), F32),
        compiler_params=pltpu.CompilerParams(dimension_semantics=("arbitrary",)),
        name="meta_gate",
    )(h_meta, w["ffn_g"], wu)
    halo = gate[rows - SUBLANES:].reshape(SUBLANES, n_chunks, FFN_TF)
    return jnp.swapaxes(halo, 0, 1)


def _shift_rows(x, prev, k):
    rolled = pltpu.roll(x, k, 0)
    head = jnp.concatenate([prev, x[:SUBLANES]], axis=0)[SUBLANES - k:2 * SUBLANES - k]
    return jnp.concatenate([head, rolled[SUBLANES:]], axis=0)


def _ffn_kernel(h_ref, g_ref, wu_hbm, wd_hbm, cw_ref, cb_ref, mh_ref, fg_ref, o_ref,
                xn_scr, halo_scr, wg_buf, wv_buf, wd_buf, sem, *, tiles_per_batch, d_ff):
    i = pl.program_id(0)
    tm = h_ref.shape[0]
    ts = tm // FFN_SUB
    n_full = d_ff // FFN_TF
    rem = d_ff - n_full * FFN_TF
    assert 0 < rem and rem % LANES == 0

    def chunk_copies(col0, width, slot):
        return (
            pltpu.make_async_copy(wu_hbm.at[:, pl.ds(col0, width)],
                                  wg_buf.at[slot, :, pl.ds(0, width)], sem.at[slot, 0]),
            pltpu.make_async_copy(wu_hbm.at[:, pl.ds(d_ff + col0, width)],
                                  wv_buf.at[slot, :, pl.ds(0, width)], sem.at[slot, 1]),
            pltpu.make_async_copy(wd_hbm.at[pl.ds(col0, width), :],
                                  wd_buf.at[slot, pl.ds(0, width), :], sem.at[slot, 2]),
        )

    def start(copies):
        for c in copies:
            c.start()

    def wait(copies):
        for c in copies:
            c.wait()

    ragged = lambda slot: chunk_copies(n_full * FFN_TF, rem, slot)
    full = lambda c, slot: chunk_copies(pl.multiple_of(c * FFN_TF, FFN_TF), FFN_TF, slot)

    def compute(idx, width, slot):
        wg = wg_buf[slot, :, :width]
        wv = wv_buf[slot, :, :width]
        wd = wd_buf[slot, :width, :]
        cw = cw_ref[idx][:, :width]
        cb = cb_ref[idx][:, :width]
        prev = halo_scr[idx][:, :width]
        for s in range(FFN_SUB):
            rows = pl.ds(s * ts, ts)
            xn = xn_scr[rows, :]
            gate = jnp.dot(xn, wg, preferred_element_type=F32)
            val = jnp.dot(xn, wv, preferred_element_type=F32)
            conv = (cw[0:1] * _shift_rows(gate, prev, 2)
                    + cw[1:2] * _shift_rows(gate, prev, 1)
                    + cw[2:3] * gate + cb)
            prev = gate[ts - SUBLANES:, :]
            act = (jax.nn.silu(conv) * val).astype(BF16)
            o_ref[rows, :] += jnp.dot(act, wd, preferred_element_type=F32)
        halo_scr[idx, :, pl.ds(0, width)] = prev

    assert n_full >= 2
    slot0 = (i * (n_full + 1)) % 2

    @pl.when(i == 0)
    def _():
        start(full(0, slot0))

    @pl.when((i % tiles_per_batch) == 0)
    def _():
        halo_scr[...] = mh_ref[...]

    h = h_ref[...]
    xn_scr[...] = _rms(h, g_ref[...]).astype(BF16)
    o_ref[...] = h

    def body(c, carry):
        slot = (slot0 + c) % 2
        wait(full(c, slot))

        @pl.when(c + 1 < n_full)
        def _():
            start(full(c + 1, 1 - slot))

        @pl.when(c + 1 == n_full)
        def _():
            start(ragged(1 - slot))

        compute(c, FFN_TF, slot)
        return carry

    lax.fori_loop(0, n_full, body, 0)

    slot_r = (slot0 + n_full) % 2
    wait(ragged(slot_r))

    @pl.when(i + 1 < pl.num_programs(0))
    def _():
        start(full(0, 1 - slot_r))

    compute(n_full, rem, slot_r)
    o_ref[...] = _rms(o_ref[...], fg_ref[...])


def _ffn(h1, meta_halo, wu, w, tiles_per_batch, tm):
    rows, d = h1.shape
    d_ff = w["wd"].shape[0]
    n_chunks = w["conv_w"].shape[0]
    in_specs = [
        pl.BlockSpec((tm, d), lambda i: (i, 0)),
        _const_spec((1, d)),
        pl.BlockSpec(memory_space=pl.ANY),
        pl.BlockSpec(memory_space=pl.ANY),
        _const_spec(w["conv_w"].shape),
        _const_spec(w["conv_b"].shape),
        _const_spec(meta_halo.shape),
        _const_spec((1, d)),
    ]
    slots = 2
    tiles = 2 * 2 * 4 * tm * d + 4 * (w["conv_w"].size + w["conv_b"].size + meta_halo.size)
    scratch = 2 * tm * d + 4 * meta_halo.size + slots * 3 * 2 * d * FFN_TF
    temps = 4 * (tm // FFN_SUB) * FFN_TF * 16
    return pl.pallas_call(
        functools.partial(_ffn_kernel, tiles_per_batch=tiles_per_batch, d_ff=d_ff),
        grid=(rows // tm,),
        in_specs=in_specs,
        out_specs=pl.BlockSpec((tm, d), lambda i: (i, 0)),
        out_shape=jax.ShapeDtypeStruct((rows, d), F32),
        scratch_shapes=[
            pltpu.VMEM((tm, d), BF16),
            pltpu.VMEM((n_chunks, SUBLANES, FFN_TF), F32),
            pltpu.VMEM((slots, d, FFN_TF), BF16),
            pltpu.VMEM((slots, d, FFN_TF), BF16),
            pltpu.VMEM((slots, FFN_TF, d), BF16),
            pltpu.SemaphoreType.DMA((slots, 3)),
        ],
        compiler_params=pltpu.CompilerParams(
            dimension_semantics=("arbitrary",),
            vmem_limit_bytes=_vmem_limit(tiles + scratch + temps)),
        name="ffn",
    )(h1, w["ffn_g"], wu, w["wd"], w["conv_w"], w["conv_b"], meta_halo, w["final_g"])


def _swap_halves(w):
    half = w.shape[-1] // 2
    return jnp.concatenate([-w[..., half:], w[..., :half]], axis=-1)


def _block_diag(blocks, n):
    rows, c = blocks.shape[-2:]
    r = rows // n
    rep = (jnp.arange(c)[:, None] == jnp.arange(n * c)[None, :] % c).astype(blocks.dtype)
    tiled = jnp.einsum("...rc,cq->...rq", blocks, rep, precision=lax.Precision.HIGHEST)
    keep = (jnp.arange(rows)[:, None] // r) == (jnp.arange(n * c)[None, :] // c)
    return jnp.where(keep, tiled, 0.0)


def _prepare(p):
    d_ssm = p["d_skip"].shape[-1]
    q_lora = p["q_a_norm"].shape[-1]
    kv_lora = p["kv_a_norm"].shape[-1]
    d_ff = p["w_down"].shape[0]
    w = {"d_ssm": d_ssm, "scale": math.log2(math.e) / math.sqrt(QK_NOPE + QK_ROPE)}

    row = lambda v: v.reshape(1, -1).astype(F32)
    w["mix_g"] = row(p["mix_norm"])
    w["win"] = p["w_in"].astype(BF16)
    w["qg"] = row(p["q_a_norm"])
    w["kvg"] = row(p["kv_a_norm"])
    wq = p["w_q_b"].reshape(q_lora, HEADS, QK_NOPE + QK_ROPE)
    wq_pe = wq[..., QK_NOPE:]
    w["wq"] = jnp.concatenate([wq, _swap_halves(wq_pe)], axis=-1).reshape(q_lora, HEADS * HEAD_W).astype(BF16)
    wkv = p["w_kv_b"].reshape(kv_lora, HEADS, QK_NOPE + V_HEAD)
    w["wk"] = wkv[..., :QK_NOPE].reshape(kv_lora, HEADS * QK_NOPE).astype(BF16)
    wv = jnp.pad(jnp.moveaxis(wkv[..., QK_NOPE:], 0, 2), ((0, 0), (0, VT_ROWS - V_HEAD), (0, 0)))
    w["wv"] = wv.reshape(HEADS * VT_ROWS, kv_lora).astype(BF16)
    w["vone"] = jnp.tile((jnp.arange(VT_ROWS) == V_HEAD).astype(F32), HEADS).reshape(-1, 1)

    lam_re, lam_im = p["lam_re"].astype(F32), p["lam_im"].astype(F32)
    dt = jnp.exp(p["log_dt"].astype(F32))[:, None]
    mag = jnp.exp(lam_re * dt)
    a_re, a_im = mag * jnp.cos(lam_im * dt), mag * jnp.sin(lam_im * dt)
    den = lam_re * lam_re + lam_im * lam_im
    f_re = ((a_re - 1.0) * lam_re + a_im * lam_im) / den
    f_im = (a_im * lam_re - (a_re - 1.0) * lam_im) / den
    b_re, b_im = p["b_re"].astype(F32), p["b_im"].astype(F32)
    bb_re = f_re[..., None] * b_re - f_im[..., None] * b_im
    bb_im = f_re[..., None] * b_im + f_im[..., None] * b_re
    n_g = lam_re.shape[0]
    n_gb = n_g // GROUPS_PER_MXU
    bd = lambda t: _block_diag(t.reshape(n_gb, GROUPS_PER_MXU * t.shape[1], t.shape[2]), GROUPS_PER_MXU)
    w["bm"] = jnp.concatenate([bd(jnp.swapaxes(bb_re, 1, 2)),
                               bd(jnp.swapaxes(bb_im, 1, 2))], axis=2).astype(BF16)
    c_re, c_im = p["c_re"].astype(F32), p["c_im"].astype(F32)
    w["cm"] = jnp.concatenate([bd(jnp.swapaxes(c_re, 1, 2)),
                               bd(jnp.swapaxes(-c_im, 1, 2))], axis=1).astype(BF16)
    bcast = lambda a: jnp.broadcast_to(a.reshape(n_gb, 1, STATE_W), (n_gb, SUBLANES, STATE_W))
    w["are"], w["aim"] = bcast(a_re), bcast(a_im)
    w["dskip"] = row(p["d_skip"])
    w["wglu"] = p["w_glu"].astype(BF16)
    w["bglu"] = row(p["b_glu"])

    w["ga"], w["gb"] = row(p["out_norm_ssm"]), row(p["out_norm_attn"])
    w["wo_a"] = p["w_out"][:d_ssm].astype(BF16)
    w["wo_b"] = p["w_out"][d_ssm:].astype(BF16)

    n_chunks = -(-d_ff // FFN_TF)
    pad = n_chunks * FFN_TF - d_ff
    w["wd"] = p["w_down"].astype(BF16)
    per_chunk = lambda a: jnp.swapaxes(
        jnp.pad(a.astype(F32), ((0, 0), (0, pad))).reshape(a.shape[0], n_chunks, FFN_TF), 0, 1)
    w["conv_w"] = per_chunk(p["conv_w"])
    w["conv_b"] = per_chunk(p["conv_b"].reshape(1, -1))
    w["ffn_g"] = row(p["ffn_norm"])
    w["final_g"] = row(p["final_norm"])
    return w


def _rope_table(n_pos):
    pos = jnp.arange(n_pos, dtype=F32)
    inv_freq = 1.0 / (ROPE_BASE ** (jnp.arange(0, QK_ROPE, 2, dtype=F32) / QK_ROPE))
    ang = pos[:, None] * inv_freq[None, :]
    cos, sin = jnp.cos(ang), jnp.sin(ang)
    return jnp.concatenate([cos, cos, sin, sin], axis=1)


def kernel(x, meta_tokens, mix_norm, w_in, lam_re, lam_im, log_dt, b_re, b_im, c_re, c_im, d_skip, w_glu, b_glu, q_a_norm, w_q_b, kv_a_norm, w_kv_b, out_norm_ssm, out_norm_attn, w_out, ffn_norm, w_up, conv_w, conv_b, w_down, final_norm):
    bsz, seq, d = x.shape
    assert meta_tokens.shape == (N_META, d) and bsz == SUBLANES
    p = dict(mix_norm=mix_norm[0], w_in=w_in[0], lam_re=lam_re[0], lam_im=lam_im[0], log_dt=log_dt[0],
             b_re=b_re[0], b_im=b_im[0], c_re=c_re[0], c_im=c_im[0], d_skip=d_skip[0],
             w_glu=w_glu[0], b_glu=b_glu[0], q_a_norm=q_a_norm[0], w_q_b=w_q_b[0],
             kv_a_norm=kv_a_norm[0], w_kv_b=w_kv_b[0], out_norm_ssm=out_norm_ssm[0],
             out_norm_attn=out_norm_attn[0], w_out=w_out[0], ffn_norm=ffn_norm[0], w_up=w_up[0],
             conv_w=conv_w[0], conv_b=conv_b[0], w_down=w_down[0], final_norm=final_norm)
    w = _prepare(p)
    d_ssm = w["d_ssm"]
    cs = _rope_table(N_META + seq)
    x2d = x.reshape(bsz * seq, d)
    meta = meta_tokens.astype(x.dtype)

    u_m, q_m, k_m, v_m = _inproj(meta, cs[:N_META], w, 1, 1, N_META)
    n_state = 2 * STATE_W * w["bm"].shape[0]
    ya_m8, h_meta_state = _s5(jnp.broadcast_to(u_m[None], (SUBLANES, N_META, d_ssm)),
                              jnp.zeros((SUBLANES, n_state), F32), w, N_META)
    ya_m = ya_m8[0]
    yb_m = _attention_meta(q_m, k_m, v_m[0])
    h1_m = _outproj(ya_m, yb_m, meta, w, 1, 1, N_META)

    tm = 512
    nt = seq // tm
    u, q, k, v = _inproj(x2d, cs[N_META:], w, bsz, nt, tm)
    ya, _ = _s5(u.reshape(bsz, seq, d_ssm), h_meta_state, w, 64)
    yb = _attention(q, k, v, k_m, v_m[0], bsz, seq)
    h1, wu = _outproj(ya.reshape(bsz * seq, d_ssm), yb, x2d, w, bsz, nt, tm, cast_src=w_up[0])
    meta_halo = _meta_gate(h1_m, wu, w)
    out = _ffn(h1, meta_halo, wu, w, nt, tm)
    return out.reshape(bsz, seq, d)
```

```python
import functools
import math

import jax
import jax.numpy as jnp
from jax import lax
from jax.experimental import pallas as pl
from jax.experimental.pallas import tpu as pltpu

F32 = jnp.float32
BF16 = jnp.bfloat16

EPS = 1e-6
CHUNK = 64
N_META = 16
SSM_GROUP = 16
SSM_STATE = 64
HEADS = 8
QK_NOPE = 128
QK_ROPE = 64
V_HEAD = 128
ROPE_BASE = 10000.0
CONV_W = 3

LANES = 128
SUBLANES = 8
MXU_DIM = 256
VMEM_BYTES = 64 * 1024 * 1024

HEAD_W = QK_NOPE + 2 * QK_ROPE
GROUPS_PER_MXU = MXU_DIM // SSM_GROUP
STATE_W = GROUPS_PER_MXU * SSM_STATE

ATT_TQ = 512
ATT_HEADS = 2
VT_ROWS = V_HEAD + SUBLANES
FFN_TF = 1024
FFN_SUB = 2
S5_SUB = 2
S5_SCAN_W = 512


_NT = (((1,), (1,)), ((), ()))


def _vmem_limit(nbytes):
    return int(min(nbytes + (4 << 20), VMEM_BYTES - (4 << 20)))


def _rms(x, g):
    x = x.astype(F32)
    return x * lax.rsqrt(jnp.mean(x * x, axis=-1, keepdims=True) + EPS) * g


def _const_spec(shape):
    nd = len(shape)
    return pl.BlockSpec(shape, lambda *_: (0,) * nd, pipeline_mode=pl.Buffered(1))


def _inproj_kernel(x_ref, g_ref, win_ref, qg_ref, wq_ref, kvg_ref, wk_ref, wv_ref, vone_ref,
                   cs_ref, u_ref, q_ref, k_ref, v_ref, *, scale):
    d_ssm = u_ref.shape[1]
    q_lora = wq_ref.shape[0]
    kv_lora = wk_ref.shape[0]
    xn = _rms(x_ref[...], g_ref[...]).astype(BF16)
    z = jnp.dot(xn, win_ref[...], preferred_element_type=F32)
    o1, o2, o3 = d_ssm, d_ssm + q_lora, d_ssm + q_lora + kv_lora
    u_ref[...] = z[:, :o1].astype(u_ref.dtype)

    cs = cs_ref[...]
    qn = _rms(z[:, o1:o2], qg_ref[...]).astype(BF16)
    q = jnp.dot(qn, wq_ref[...], preferred_element_type=F32)
    cs_q = cs * scale
    for h in range(HEADS):
        c0 = h * HEAD_W
        q_ref[:, c0:c0 + QK_NOPE] = (q[:, c0:c0 + QK_NOPE] * scale).astype(q_ref.dtype)
        q_ref[:, c0 + QK_NOPE:c0 + HEAD_W] = (q[:, c0 + QK_NOPE:c0 + HEAD_W] * cs_q).astype(q_ref.dtype)

    kvn = _rms(z[:, o2:o3], kvg_ref[...]).astype(BF16)
    kn = jnp.dot(kvn, wk_ref[...], preferred_element_type=F32)
    vt = lax.dot_general(wv_ref[...], kvn, _NT, preferred_element_type=F32)
    v_ref[...] = (vt + vone_ref[...]).astype(v_ref.dtype)
    kpe = z[:, o3:o3 + QK_ROPE]
    x1, x2 = kpe[:, :QK_ROPE // 2], kpe[:, QK_ROPE // 2:]
    t = jnp.concatenate([kpe, -x2, x1], axis=1) * cs
    krot = (t + pltpu.roll(t, QK_ROPE, 1)).astype(k_ref.dtype)
    for h in range(HEADS):
        c0 = h * HEAD_W
        k_ref[:, c0:c0 + QK_NOPE] = kn[:, h * QK_NOPE:(h + 1) * QK_NOPE].astype(k_ref.dtype)
        k_ref[:, c0 + QK_NOPE:c0 + HEAD_W] = krot


def _inproj(x2d, cs, w, nb, nt, tm):
    rows, d = x2d.shape
    d_ssm = w["d_ssm"]
    n_in = w["win"].shape[1]
    row_spec = lambda width: pl.BlockSpec((tm, width), lambda b, j: (b * nt + j, 0))
    in_specs = [
        row_spec(d),
        _const_spec((1, d)),
        _const_spec(w["win"].shape),
        _const_spec(w["qg"].shape),
        _const_spec(w["wq"].shape),
        _const_spec(w["kvg"].shape),
        _const_spec(w["wk"].shape),
        _const_spec(w["wv"].shape),
        _const_spec(w["vone"].shape),
        pl.BlockSpec((tm, LANES), lambda b, j: (j, 0)),
    ]
    out_shape = (
        jax.ShapeDtypeStruct((rows, d_ssm), BF16),
        jax.ShapeDtypeStruct((rows, HEADS * HEAD_W), BF16),
        jax.ShapeDtypeStruct((rows, HEADS * HEAD_W), BF16),
        jax.ShapeDtypeStruct((nb, HEADS * VT_ROWS, nt * tm), BF16),
    )
    out_specs = (
        row_spec(d_ssm),
        row_spec(HEADS * HEAD_W),
        row_spec(HEADS * HEAD_W),
        pl.BlockSpec((None, HEADS * VT_ROWS, tm), lambda b, j: (b, 0, j)),
    )
    weights = 2 * (w["win"].size + w["wq"].size + w["wk"].size + w["wv"].size)
    tiles = 2 * tm * (4 * d + 2 * d_ssm + 4 * HEADS * HEAD_W + 2 * HEADS * V_HEAD + 4 * LANES)
    temps = 4 * tm * (d + n_in)
    return pl.pallas_call(
        functools.partial(_inproj_kernel, scale=w["scale"]),
        grid=(nb, nt),
        in_specs=in_specs,
        out_specs=out_specs,
        out_shape=out_shape,
        compiler_params=pltpu.CompilerParams(
            dimension_semantics=("arbitrary", "arbitrary"),
            vmem_limit_bytes=_vmem_limit(weights + tiles + temps)),
        name="inproj",
    )(x2d, w["mix_g"], w["win"], w["qg"], w["wq"], w["kvg"], w["wk"], w["wv"], w["vone"], cs)


def _s5_kernel(u_ref, h0_ref, perm_ref, bm_ref, cm_ref, are_ref, aim_ref, dskip_ref, wglu_ref, bglu_ref,
               y_ref, hT_ref, hbuf, state, *, steps):
    n_gb = bm_ref.shape[0]
    sw = are_ref.shape[2]
    cin = bm_ref.shape[1]
    nb, _, d_ssm = u_ref.shape

    @pl.when(pl.program_id(0) == 0)
    def _():
        state[...] = h0_ref[...]

    u = jnp.dot(perm_ref[0], u_ref[...].reshape(nb * steps, d_ssm),
                preferred_element_type=F32).astype(BF16)

    sub = steps // S5_SUB
    span = lambda k: slice(k * sub * SUBLANES, (k + 1) * sub * SUBLANES)
    for k in range(S5_SUB):
        for gb in range(n_gb):
            hbuf[span(k), 2 * sw * gb:2 * sw * (gb + 1)] = jnp.dot(
                u[span(k), cin * gb:cin * (gb + 1)], bm_ref[gb], preferred_element_type=F32)

    blocks = [(2 * sw * gb + o, 2 * sw * gb + sw + o, gb, o)
              for gb in range(n_gb) for o in range(0, sw, S5_SCAN_W)]
    carry = [(state[:, c_re:c_re + S5_SCAN_W], state[:, c_im:c_im + S5_SCAN_W])
             for c_re, c_im, _, _ in blocks]
    for k in range(S5_SUB):
        for j, (c_re, c_im, gb, o) in enumerate(blocks):
            are = are_ref[gb, :, o:o + S5_SCAN_W]
            aim = aim_ref[gb, :, o:o + S5_SCAN_W]
            hre, him = carry[j]
            for t in range(k * sub, (k + 1) * sub):
                r0 = t * SUBLANES
                nre = are * hre - aim * him + hbuf[r0:r0 + SUBLANES, c_re:c_re + S5_SCAN_W]
                nim = are * him + aim * hre + hbuf[r0:r0 + SUBLANES, c_im:c_im + S5_SCAN_W]
                hbuf[r0:r0 + SUBLANES, c_re:c_re + S5_SCAN_W] = nre
                hbuf[r0:r0 + SUBLANES, c_im:c_im + S5_SCAN_W] = nim
                hre, him = nre, nim
            carry[j] = (hre, him)
    for (c_re, c_im, _, _), (hre, him) in zip(blocks, carry):
        state[:, c_re:c_re + S5_SCAN_W] = hre
        state[:, c_im:c_im + S5_SCAN_W] = him

    outs = []
    for k in range(S5_SUB):
        ys = [jnp.dot(hbuf[span(k), 2 * sw * gb:2 * sw * (gb + 1)].astype(BF16), cm_ref[gb],
                      preferred_element_type=F32) for gb in range(n_gb)]
        y = jnp.concatenate(ys, axis=1) + dskip_ref[...] * u[span(k)].astype(F32)
        g = jax.nn.gelu(y)
        gate = jnp.dot(g.astype(BF16), wglu_ref[...], preferred_element_type=F32) + bglu_ref[...]
        outs.append((g * jax.nn.sigmoid(gate)).astype(BF16))
    out = jnp.concatenate(outs, axis=0)
    y_ref[...] = jnp.dot(perm_ref[1], out, preferred_element_type=F32).astype(y_ref.dtype).reshape(y_ref.shape)
    hT_ref[...] = state[...]


def _s5(u, h0, w, steps):
    nb, t_len, d_ssm = u.shape
    assert nb == SUBLANES
    r = steps * SUBLANES
    n_state = h0.shape[1]
    src = (jnp.arange(r) % SUBLANES) * steps + jnp.arange(r) // SUBLANES
    fwd = (src[:, None] == jnp.arange(r)[None, :]).astype(BF16)
    perm = jnp.stack([fwd, fwd.T])
    blk = pl.BlockSpec((nb, steps, d_ssm), lambda i: (0, i, 0))
    in_specs = [
        blk,
        _const_spec(h0.shape),
        _const_spec(perm.shape),
        _const_spec(w["bm"].shape),
        _const_spec(w["cm"].shape),
        _const_spec(w["are"].shape),
        _const_spec(w["aim"].shape),
        _const_spec((1, d_ssm)),
        _const_spec(w["wglu"].shape),
        _const_spec((1, d_ssm)),
    ]
    out_shape = (jax.ShapeDtypeStruct(u.shape, BF16),
                 jax.ShapeDtypeStruct(h0.shape, F32))
    out_specs = (blk, pl.BlockSpec(h0.shape, lambda i: (0, 0)))
    weights = (2 * (w["bm"].size + w["cm"].size + w["wglu"].size + perm.size)
               + 4 * (w["are"].size + w["aim"].size))
    tiles = 2 * r * d_ssm * 2 * 2 + 3 * 4 * h0.size
    scratch = 4 * r * n_state + 4 * h0.size
    temps = 2 * 4 * r * d_ssm
    return pl.pallas_call(
        functools.partial(_s5_kernel, steps=steps),
        grid=(t_len // steps,),
        in_specs=in_specs,
        out_specs=out_specs,
        out_shape=out_shape,
        scratch_shapes=[pltpu.VMEM((r, n_state), F32), pltpu.VMEM(h0.shape, F32)],
        compiler_params=pltpu.CompilerParams(
            dimension_semantics=("arbitrary",),
            vmem_limit_bytes=_vmem_limit(weights + tiles + scratch + temps)),
        name="s5",
    )(u, h0, perm, w["bm"], w["cm"], w["are"], w["aim"], w["dskip"], w["wglu"], w["bglu"])


def _attn_kernel(q_ref, k_ref, vt_ref, km_ref, vmt_ref, src_ref, o_ref, dst_ref):
    @pl.when(pl.program_id(1) == 0)
    def _():
        dst_ref[...] = src_ref[...].astype(dst_ref.dtype)

    seq = q_ref.shape[0]
    half = ATT_TQ // 2
    neg = jnp.finfo(F32).min
    kc = lax.broadcasted_iota(jnp.int32, (half, half), 0) // CHUNK
    qc = lax.broadcasted_iota(jnp.int32, (half, half), 1) // CHUNK
    visible = kc <= qc
    colmax = lambda t: jnp.max(t, axis=0, keepdims=True)
    prob = lambda s, m: jnp.exp2(s - m).astype(BF16)
    order = [(hh, i) for i in range(seq // ATT_TQ - 1, -1, -1) for hh in range(ATT_HEADS)]
    scores = {}
    for hh, i in order:
        r0 = i * ATT_TQ
        qs = slice(hh * HEAD_W, (hh + 1) * HEAD_W)
        q = q_ref[r0:r0 + ATT_TQ, qs]
        s_meta = lax.dot_general(km_ref[:, qs], q, _NT, preferred_element_type=F32)
        s_diag = lax.dot_general(k_ref[r0:r0 + ATT_TQ, qs], q, _NT, preferred_element_type=F32)
        s_full = (lax.dot_general(k_ref[0:r0, qs], q, _NT, preferred_element_type=F32)
                  if i > 0 else None)
        scores[hh, i] = (s_meta, s_diag, s_full)
    for hh, i in order:
        r0 = i * ATT_TQ
        vs = slice(hh * VT_ROWS, (hh + 1) * VT_ROWS)
        vmt = vmt_ref[vs, :]
        s_meta, s_diag, s_full = scores[hh, i]
        s00 = jnp.where(visible, s_diag[:half, :half], neg)
        s01 = s_diag[:half, half:]
        s11 = jnp.where(visible, s_diag[half:, half:], neg)
        m = jnp.concatenate([colmax(s00), jnp.maximum(colmax(s01), colmax(s11))], axis=1)
        m = jnp.maximum(m, colmax(s_meta))
        if i > 0:
            m = jnp.maximum(m, colmax(s_full))
        p_meta = prob(s_meta, m)
        p00 = prob(s00, m[:, :half])
        p01 = prob(s01, m[:, half:])
        p11 = prob(s11, m[:, half:])
        p_diag = jnp.concatenate([jnp.concatenate([p00, p01], axis=1),
                                  jnp.concatenate([jnp.zeros_like(p11), p11], axis=1)], axis=0)
        o = jnp.dot(vmt, p_meta, preferred_element_type=F32)
        o = o + jnp.dot(vt_ref[vs, r0:r0 + ATT_TQ], p_diag, preferred_element_type=F32)
        if i > 0:
            o = o + jnp.dot(vt_ref[vs, 0:r0], prob(s_full, m), preferred_element_type=F32)
        out = o[:V_HEAD] / o[V_HEAD:V_HEAD + 1]
        o_ref[r0:r0 + ATT_TQ, hh * V_HEAD:(hh + 1) * V_HEAD] = out.T.astype(o_ref.dtype)


def _attention(q, k, vt, km, vmt, cast_src, nb, seq):
    assert ATT_TQ % CHUNK == 0 and seq % ATT_TQ == 0 and V_HEAD == LANES and HEADS % ATT_HEADS == 0
    n_meta = km.shape[0]
    hw, vr = ATT_HEADS * HEAD_W, ATT_HEADS * VT_ROWS
    src_rows, src_cols = cast_src.shape
    blk = src_rows // nb
    assert blk * nb == src_rows and blk % (2 * SUBLANES) == 0
    cast_spec = pl.BlockSpec((blk, src_cols), lambda b, h: (b, 0))
    in_specs = [
        pl.BlockSpec((seq, hw), lambda b, h: (b, h)),
        pl.BlockSpec((seq, hw), lambda b, h: (b, h)),
        pl.BlockSpec((None, vr, seq), lambda b, h: (b, h, 0)),
        pl.BlockSpec((n_meta, hw), lambda b, h: (0, h)),
        pl.BlockSpec((vr, n_meta), lambda b, h: (h, 0)),
        cast_spec,
    ]
    tiles = 2 * 2 * seq * (2 * hw + vr + ATT_HEADS * V_HEAD) + 2 * blk * src_cols * (4 + 2)
    temps = 4 * 4 * seq * ATT_TQ * ATT_HEADS
    return pl.pallas_call(
        _attn_kernel,
        grid=(nb, HEADS // ATT_HEADS),
        in_specs=in_specs,
        out_specs=(pl.BlockSpec((seq, ATT_HEADS * V_HEAD), lambda b, h: (b, h)), cast_spec),
        out_shape=(jax.ShapeDtypeStruct((nb * seq, HEADS * V_HEAD), BF16),
                   jax.ShapeDtypeStruct(cast_src.shape, BF16)),
        compiler_params=pltpu.CompilerParams(
            dimension_semantics=("arbitrary", "arbitrary"),
            vmem_limit_bytes=_vmem_limit(tiles + temps)),
        name="attn",
    )(q, k, vt, km, vmt, cast_src)


def _attn_meta_kernel(q_ref, k_ref, vt_ref, o_ref):
    for h in range(HEADS):
        s = lax.dot_general(q_ref[:, h * HEAD_W:(h + 1) * HEAD_W],
                            k_ref[:, h * HEAD_W:(h + 1) * HEAD_W], _NT, preferred_element_type=F32)
        p = jnp.exp2(s - jnp.max(s, axis=1, keepdims=True))
        pv = lax.dot_general(p.astype(BF16), vt_ref[h * VT_ROWS:h * VT_ROWS + V_HEAD, :], _NT,
                             preferred_element_type=F32)
        o_ref[:, h * V_HEAD:(h + 1) * V_HEAD] = (pv / jnp.sum(p, axis=1, keepdims=True)).astype(o_ref.dtype)


def _attention_meta(q, k, vt):
    return pl.pallas_call(
        _attn_meta_kernel,
        out_shape=jax.ShapeDtypeStruct((q.shape[0], HEADS * V_HEAD), BF16),
        name="attn_meta",
    )(q, k, vt)


def _outproj_kernel(ya_ref, yb_ref, x_ref, ga_ref, gb_ref, wa_ref, wb_ref, o_ref):
    a = _rms(ya_ref[...], ga_ref[...]).astype(BF16)
    b = _rms(yb_ref[...], gb_ref[...]).astype(BF16)
    mix = jnp.dot(a, wa_ref[...], preferred_element_type=F32)
    mix = mix + jnp.dot(b, wb_ref[...], preferred_element_type=F32)
    o_ref[...] = x_ref[...].astype(F32) + mix


def _outproj_cast_kernel(ya_ref, yb_ref, x_ref, ga_ref, gb_ref, wa_ref, wb_ref, src_ref, o_ref, dst_ref):
    _outproj_kernel(ya_ref, yb_ref, x_ref, ga_ref, gb_ref, wa_ref, wb_ref, o_ref)
    dst_ref[...] = src_ref[...].astype(dst_ref.dtype)


def _outproj(ya, yb, x2d, w, nb, nt, tm, cast_src=None):
    rows, d = x2d.shape
    d_ssm, d_attn = w["wo_a"].shape[0], w["wo_b"].shape[0]
    row_spec = lambda width: pl.BlockSpec((tm, width), lambda b, j: (b * nt + j, 0))
    in_specs = [
        row_spec(d_ssm),
        row_spec(d_attn),
        row_spec(d),
        _const_spec((1, d_ssm)),
        _const_spec((1, d_attn)),
        _const_spec(w["wo_a"].shape),
        _const_spec(w["wo_b"].shape),
    ]
    args = [ya, yb, x2d, w["ga"], w["gb"], w["wo_a"], w["wo_b"]]
    out_specs = row_spec(d)
    out_shape = jax.ShapeDtypeStruct((rows, d), F32)
    weights = 2 * (w["wo_a"].size + w["wo_b"].size)
    tiles = 2 * tm * (2 * d_ssm + 2 * d_attn + 4 * d + 4 * d)
    temps = 4 * tm * d
    body = _outproj_kernel
    if cast_src is not None:
        src_rows, src_cols = cast_src.shape
        blk = src_rows // (nb * nt)
        assert blk * nb * nt == src_rows and blk % (2 * SUBLANES) == 0
        cast_spec = pl.BlockSpec((blk, src_cols), lambda b, j: (b * nt + j, 0))
        in_specs.append(cast_spec)
        args.append(cast_src)
        out_specs = (out_specs, cast_spec)
        out_shape = (out_shape, jax.ShapeDtypeStruct(cast_src.shape, BF16))
        tiles += 2 * blk * src_cols * (4 + 2)
        body = _outproj_cast_kernel
    return pl.pallas_call(
        body,
        grid=(nb, nt),
        in_specs=in_specs,
        out_specs=out_specs,
        out_shape=out_shape,
        compiler_params=pltpu.CompilerParams(
            dimension_semantics=("arbitrary", "arbitrary"),
            vmem_limit_bytes=_vmem_limit(weights + tiles + temps)),
        name="outproj",
    )(*args)


def _gate_kernel(h_ref, g_ref, wg_ref, o_ref):
    xn = _rms(h_ref[...], g_ref[...]).astype(BF16)
    o_ref[...] = jnp.dot(xn, wg_ref[...], preferred_element_type=F32)


def _meta_gate(h_meta, wu, w):
    rows, d = h_meta.shape
    n_chunks = w["conv_w"].shape[0]
    gate = pl.pallas_call(
        _gate_kernel,
        grid=(n_chunks,),
        in_specs=[_const_spec((rows, d)), _const_spec((1, d)),
                  pl.BlockSpec((d, FFN_TF), lambda j: (0, j))],
        out_specs=pl.BlockSpec((rows, FFN_TF), lambda j: (0, j)),
        out_shape=jax.ShapeDtypeStruct((rows, n_chunks * FFN_TF), F32),
        compiler_params=pltpu.CompilerParams(dimension_semantics=("arbitrary",)),
        name="meta_gate",
    )(h_meta, w["ffn_g"], wu)
    halo = gate[rows - SUBLANES:].reshape(SUBLANES, n_chunks, FFN_TF)
    return jnp.swapaxes(halo, 0, 1)


def _shift_rows(x, prev, k):
    rolled = pltpu.roll(x, k, 0)
    head = jnp.concatenate([prev, x[:SUBLANES]], axis=0)[SUBLANES - k:2 * SUBLANES - k]
    return jnp.concatenate([head, rolled[SUBLANES:]], axis=0)


def _ffn_kernel(h_ref, g_ref, wu_hbm, wd_hbm, cw_ref, cb_ref, mh_ref, fg_ref, o_ref,
                xn_scr, halo_scr, wg_buf, wv_buf, wd_buf, sem, *, tiles_per_batch, d_ff):
    i = pl.program_id(0)
    tm = h_ref.shape[0]
    ts = tm // FFN_SUB
    n_full = d_ff // FFN_TF
    rem = d_ff - n_full * FFN_TF
    assert 0 < rem and rem % LANES == 0

    def chunk_copies(col0, width, slot):
        return (
            pltpu.make_async_copy(wu_hbm.at[:, pl.ds(col0, width)],
                                  wg_buf.at[slot, :, pl.ds(0, width)], sem.at[slot, 0]),
            pltpu.make_async_copy(wu_hbm.at[:, pl.ds(d_ff + col0, width)],
                                  wv_buf.at[slot, :, pl.ds(0, width)], sem.at[slot, 1]),
            pltpu.make_async_copy(wd_hbm.at[pl.ds(col0, width), :],
                                  wd_buf.at[slot, pl.ds(0, width), :], sem.at[slot, 2]),
        )

    def start(copies):
        for c in copies:
            c.start()

    def wait(copies):
        for c in copies:
            c.wait()

    ragged = lambda slot: chunk_copies(n_full * FFN_TF, rem, slot)
    full = lambda c, slot: chunk_copies(pl.multiple_of(c * FFN_TF, FFN_TF), FFN_TF, slot)

    def compute(idx, width, slot):
        wg = wg_buf[slot, :, :width]
        wv = wv_buf[slot, :, :width]
        wd = wd_buf[slot, :width, :]
        cw = cw_ref[idx][:, :width]
        cb = cb_ref[idx][:, :width]
        prev = halo_scr[idx][:, :width]
        for s in range(FFN_SUB):
            rows = pl.ds(s * ts, ts)
            xn = xn_scr[rows, :]
            gate = jnp.dot(xn, wg, preferred_element_type=F32)
            val = jnp.dot(xn, wv, preferred_element_type=F32)
            conv = (cw[0:1] * _shift_rows(gate, prev, 2)
                    + cw[1:2] * _shift_rows(gate, prev, 1)
                    + cw[2:3] * gate + cb)
            prev = gate[ts - SUBLANES:, :]
            act = (jax.nn.silu(conv) * val).astype(BF16)
            o_ref[rows, :] += jnp.dot(act, wd, preferred_element_type=F32)
        halo_scr[idx, :, pl.ds(0, width)] = prev

    assert n_full >= 2
    slot0 = (i * (n_full + 1)) % 2

    @pl.when(i == 0)
    def _():
        start(full(0, slot0))

    @pl.when((i % tiles_per_batch) == 0)
    def _():
        halo_scr[...] = mh_ref[...]

    h = h_ref[...]
    xn_scr[...] = _rms(h, g_ref[...]).astype(BF16)
    o_ref[...] = h

    def body(c, carry):
        slot = (slot0 + c) % 2
        wait(full(c, slot))

        @pl.when(c + 1 < n_full)
        def _():
            start(full(c + 1, 1 - slot))

        @pl.when(c + 1 == n_full)
        def _():
            start(ragged(1 - slot))

        compute(c, FFN_TF, slot)
        return carry

    lax.fori_loop(0, n_full, body, 0)

    slot_r = (slot0 + n_full) % 2
    wait(ragged(slot_r))

    @pl.when(i + 1 < pl.num_programs(0))
    def _():
        start(full(0, 1 - slot_r))

    compute(n_full, rem, slot_r)
    o_ref[...] = _rms(o_ref[...], fg_ref[...])


def _ffn(h1, meta_halo, wu, wd, w, tiles_per_batch, tm):
    rows, d = h1.shape
    d_ff = wd.shape[0]
    n_chunks = w["conv_w"].shape[0]
    in_specs = [
        pl.BlockSpec((tm, d), lambda i: (i, 0)),
        _const_spec((1, d)),
        pl.BlockSpec(memory_space=pl.ANY),
        pl.BlockSpec(memory_space=pl.ANY),
        _const_spec(w["conv_w"].shape),
        _const_spec(w["conv_b"].shape),
        _const_spec(meta_halo.shape),
        _const_spec((1, d)),
    ]
    slots = 2
    tiles = 2 * 2 * 4 * tm * d + 4 * (w["conv_w"].size + w["conv_b"].size + meta_halo.size)
    scratch = 2 * tm * d + 4 * meta_halo.size + slots * 3 * 2 * d * FFN_TF
    temps = 4 * (tm // FFN_SUB) * FFN_TF * 16
    return pl.pallas_call(
        functools.partial(_ffn_kernel, tiles_per_batch=tiles_per_batch, d_ff=d_ff),
        grid=(rows // tm,),
        in_specs=in_specs,
        out_specs=pl.BlockSpec((tm, d), lambda i: (i, 0)),
        out_shape=jax.ShapeDtypeStruct((rows, d), F32),
        scratch_shapes=[
            pltpu.VMEM((tm, d), BF16),
            pltpu.VMEM((n_chunks, SUBLANES, FFN_TF), F32),
            pltpu.VMEM((slots, d, FFN_TF), BF16),
            pltpu.VMEM((slots, d, FFN_TF), BF16),
            pltpu.VMEM((slots, FFN_TF, d), BF16),
            pltpu.SemaphoreType.DMA((slots, 3)),
        ],
        compiler_params=pltpu.CompilerParams(
            dimension_semantics=("arbitrary",),
            vmem_limit_bytes=_vmem_limit(tiles + scratch + temps)),
        name="ffn",
    )(h1, w["ffn_g"], wu, wd, w["conv_w"], w["conv_b"], meta_halo, w["final_g"])


def _swap_halves(w):
    half = w.shape[-1] // 2
    return jnp.concatenate([-w[..., half:], w[..., :half]], axis=-1)


def _block_diag(blocks, n):
    rows, c = blocks.shape[-2:]
    r = rows // n
    rep = (jnp.arange(c)[:, None] == jnp.arange(n * c)[None, :] % c).astype(blocks.dtype)
    tiled = jnp.einsum("...rc,cq->...rq", blocks, rep, precision=lax.Precision.HIGHEST)
    keep = (jnp.arange(rows)[:, None] // r) == (jnp.arange(n * c)[None, :] // c)
    return jnp.where(keep, tiled, 0.0)


def _prepare(p):
    d_ssm = p["d_skip"].shape[-1]
    q_lora = p["q_a_norm"].shape[-1]
    kv_lora = p["kv_a_norm"].shape[-1]
    d_ff = p["w_down"].shape[0]
    w = {"d_ssm": d_ssm, "scale": math.log2(math.e) / math.sqrt(QK_NOPE + QK_ROPE)}

    row = lambda v: v.reshape(1, -1).astype(F32)
    w["mix_g"] = row(p["mix_norm"])
    w["win"] = p["w_in"].astype(BF16)
    w["qg"] = row(p["q_a_norm"])
    w["kvg"] = row(p["kv_a_norm"])
    wq = p["w_q_b"].reshape(q_lora, HEADS, QK_NOPE + QK_ROPE)
    wq_pe = wq[..., QK_NOPE:]
    w["wq"] = jnp.concatenate([wq, _swap_halves(wq_pe)], axis=-1).reshape(q_lora, HEADS * HEAD_W).astype(BF16)
    wkv = p["w_kv_b"].reshape(kv_lora, HEADS, QK_NOPE + V_HEAD)
    w["wk"] = wkv[..., :QK_NOPE].reshape(kv_lora, HEADS * QK_NOPE).astype(BF16)
    wv = jnp.pad(jnp.moveaxis(wkv[..., QK_NOPE:], 0, 2), ((0, 0), (0, VT_ROWS - V_HEAD), (0, 0)))
    w["wv"] = wv.reshape(HEADS * VT_ROWS, kv_lora).astype(BF16)
    w["vone"] = jnp.tile((jnp.arange(VT_ROWS) == V_HEAD).astype(F32), HEADS).reshape(-1, 1)

    lam_re, lam_im = p["lam_re"].astype(F32), p["lam_im"].astype(F32)
    dt = jnp.exp(p["log_dt"].astype(F32))[:, None]
    mag = jnp.exp(lam_re * dt)
    a_re, a_im = mag * jnp.cos(lam_im * dt), mag * jnp.sin(lam_im * dt)
    den = lam_re * lam_re + lam_im * lam_im
    f_re = ((a_re - 1.0) * lam_re + a_im * lam_im) / den
    f_im = (a_im * lam_re - (a_re - 1.0) * lam_im) / den
    b_re, b_im = p["b_re"].astype(F32), p["b_im"].astype(F32)
    bb_re = f_re[..., None] * b_re - f_im[..., None] * b_im
    bb_im = f_re[..., None] * b_im + f_im[..., None] * b_re
    n_g = lam_re.shape[0]
    n_gb = n_g // GROUPS_PER_MXU
    bd = lambda t: _block_diag(t.reshape(n_gb, GROUPS_PER_MXU * t.shape[1], t.shape[2]), GROUPS_PER_MXU)
    w["bm"] = jnp.concatenate([bd(jnp.swapaxes(bb_re, 1, 2)),
                               bd(jnp.swapaxes(bb_im, 1, 2))], axis=2).astype(BF16)
    c_re, c_im = p["c_re"].astype(F32), p["c_im"].astype(F32)
    w["cm"] = jnp.concatenate([bd(jnp.swapaxes(c_re, 1, 2)),
                               bd(jnp.swapaxes(-c_im, 1, 2))], axis=1).astype(BF16)
    bcast = lambda a: jnp.broadcast_to(a.reshape(n_gb, 1, STATE_W), (n_gb, SUBLANES, STATE_W))
    w["are"], w["aim"] = bcast(a_re), bcast(a_im)
    w["dskip"] = row(p["d_skip"])
    w["wglu"] = p["w_glu"].astype(BF16)
    w["bglu"] = row(p["b_glu"])

    w["ga"], w["gb"] = row(p["out_norm_ssm"]), row(p["out_norm_attn"])
    w["wo_a"] = p["w_out"][:d_ssm].astype(BF16)
    w["wo_b"] = p["w_out"][d_ssm:].astype(BF16)

    n_chunks = -(-d_ff // FFN_TF)
    pad = n_chunks * FFN_TF - d_ff
    per_chunk = lambda a: jnp.swapaxes(
        jnp.pad(a.astype(F32), ((0, 0), (0, pad))).reshape(a.shape[0], n_chunks, FFN_TF), 0, 1)
    w["conv_w"] = per_chunk(p["conv_w"])
    w["conv_b"] = per_chunk(p["conv_b"].reshape(1, -1))
    w["ffn_g"] = row(p["ffn_norm"])
    w["final_g"] = row(p["final_norm"])
    return w


def _rope_table(n_pos):
    pos = jnp.arange(n_pos, dtype=F32)
    inv_freq = 1.0 / (ROPE_BASE ** (jnp.arange(0, QK_ROPE, 2, dtype=F32) / QK_ROPE))
    ang = pos[:, None] * inv_freq[None, :]
    cos, sin = jnp.cos(ang), jnp.sin(ang)
    return jnp.concatenate([cos, cos, sin, sin], axis=1)


def kernel(x, meta_tokens, mix_norm, w_in, lam_re, lam_im, log_dt, b_re, b_im, c_re, c_im, d_skip, w_glu, b_glu, q_a_norm, w_q_b, kv_a_norm, w_kv_b, out_norm_ssm, out_norm_attn, w_out, ffn_norm, w_up, conv_w, conv_b, w_down, final_norm):
    bsz, seq, d = x.shape
    assert meta_tokens.shape == (N_META, d) and bsz == SUBLANES
    p = dict(mix_norm=mix_norm[0], w_in=w_in[0], lam_re=lam_re[0], lam_im=lam_im[0], log_dt=log_dt[0],
             b_re=b_re[0], b_im=b_im[0], c_re=c_re[0], c_im=c_im[0], d_skip=d_skip[0],
             w_glu=w_glu[0], b_glu=b_glu[0], q_a_norm=q_a_norm[0], w_q_b=w_q_b[0],
             kv_a_norm=kv_a_norm[0], w_kv_b=w_kv_b[0], out_norm_ssm=out_norm_ssm[0],
             out_norm_attn=out_norm_attn[0], w_out=w_out[0], ffn_norm=ffn_norm[0], w_up=w_up[0],
             conv_w=conv_w[0], conv_b=conv_b[0], w_down=w_down[0], final_norm=final_norm)
    w = _prepare(p)
    d_ssm = w["d_ssm"]
    cs = _rope_table(N_META + seq)
    x2d = x.reshape(bsz * seq, d)
    meta = meta_tokens.astype(x.dtype)

    u_m, q_m, k_m, v_m = _inproj(meta, cs[:N_META], w, 1, 1, N_META)
    n_state = 2 * STATE_W * w["bm"].shape[0]
    ya_m8, h_meta_state = _s5(jnp.broadcast_to(u_m[None], (SUBLANES, N_META, d_ssm)),
                              jnp.zeros((SUBLANES, n_state), F32), w, N_META)
    ya_m = ya_m8[0]
    yb_m = _attention_meta(q_m, k_m, v_m[0])
    h1_m = _outproj(ya_m, yb_m, meta, w, 1, 1, N_META)

    tm = 512
    nt = seq // tm
    u, q, k, v = _inproj(x2d, cs[N_META:], w, bsz, nt, tm)
    ya, _ = _s5(u.reshape(bsz, seq, d_ssm), h_meta_state, w, 64)
    yb, wd = _attention(q, k, v, k_m, v_m[0], w_down[0], bsz, seq)
    h1, wu = _outproj(ya.reshape(bsz * seq, d_ssm), yb, x2d, w, bsz, nt, tm, cast_src=w_up[0])
    meta_halo = _meta_gate(h1_m, wu, w)
    out = _ffn(h1, meta_halo, wu, wd, w, nt, tm)
    return out.reshape(bsz, seq, d)
```

```python
import functools
import math

import jax
import jax.numpy as jnp
from jax import lax
from jax.experimental import pallas as pl
from jax.experimental.pallas import tpu as pltpu

F32 = jnp.float32
BF16 = jnp.bfloat16

EPS = 1e-6
CHUNK = 64
N_META = 16
SSM_GROUP = 16
SSM_STATE = 64
HEADS = 8
QK_NOPE = 128
QK_ROPE = 64
V_HEAD = 128
ROPE_BASE = 10000.0
CONV_W = 3

LANES = 128
SUBLANES = 8
MXU_DIM = 256
VMEM_BYTES = 64 * 1024 * 1024

HEAD_W = QK_NOPE + 2 * QK_ROPE
GROUPS_PER_MXU = MXU_DIM // SSM_GROUP
STATE_W = GROUPS_PER_MXU * SSM_STATE

ATT_TQ = 512
ATT_HEADS = 2
VT_ROWS = V_HEAD + SUBLANES
FFN_TF = 1024
IN_SUB = 2
FFN_SUB = 2
S5_SUB = 2
S5_SCAN_W = 512


_NT = (((1,), (1,)), ((), ()))


def _vmem_limit(nbytes):
    return int(min(nbytes + (4 << 20), VMEM_BYTES - (4 << 20)))


def _rms(x, g):
    x = x.astype(F32)
    return x * lax.rsqrt(jnp.mean(x * x, axis=-1, keepdims=True) + EPS) * g


def _const_spec(shape):
    nd = len(shape)
    return pl.BlockSpec(shape, lambda *_: (0,) * nd, pipeline_mode=pl.Buffered(1))


def _inproj_kernel(x_ref, g_ref, win_ref, qg_ref, wq_ref, kvg_ref, wk_ref, wv_ref, vone_ref,
                   cs_ref, u_ref, q_ref, k_ref, v_ref, *, scale):
    d_ssm = u_ref.shape[1]
    q_lora = wq_ref.shape[0]
    kv_lora = wk_ref.shape[0]
    o1, o2, o3 = d_ssm, d_ssm + q_lora, d_ssm + q_lora + kv_lora
    tm = x_ref.shape[0]
    n_sub = IN_SUB if tm % (IN_SUB * LANES) == 0 else 1
    ts = tm // n_sub
    zs = []
    for s in range(n_sub):
        xn = _rms(x_ref[s * ts:(s + 1) * ts, :], g_ref[...]).astype(BF16)
        zs.append(jnp.dot(xn, win_ref[...], preferred_element_type=F32))
    for s, z in enumerate(zs):
        rows = slice(s * ts, (s + 1) * ts)
        u_ref[rows, :] = z[:, :o1].astype(u_ref.dtype)

        cs = cs_ref[rows, :]
        qn = _rms(z[:, o1:o2], qg_ref[...]).astype(BF16)
        q = jnp.dot(qn, wq_ref[...], preferred_element_type=F32)
        cs_q = cs * scale
        for h in range(HEADS):
            c0 = h * HEAD_W
            q_ref[rows, c0:c0 + QK_NOPE] = (q[:, c0:c0 + QK_NOPE] * scale).astype(q_ref.dtype)
            q_ref[rows, c0 + QK_NOPE:c0 + HEAD_W] = (q[:, c0 + QK_NOPE:c0 + HEAD_W] * cs_q).astype(q_ref.dtype)

        kvn = _rms(z[:, o2:o3], kvg_ref[...]).astype(BF16)
        kn = jnp.dot(kvn, wk_ref[...], preferred_element_type=F32)
        vt = lax.dot_general(wv_ref[...], kvn, _NT, preferred_element_type=F32)
        v_ref[:, rows] = (vt + vone_ref[...]).astype(v_ref.dtype)
        kpe = z[:, o3:o3 + QK_ROPE]
        x1, x2 = kpe[:, :QK_ROPE // 2], kpe[:, QK_ROPE // 2:]
        t = jnp.concatenate([kpe, -x2, x1], axis=1) * cs
        krot = (t + pltpu.roll(t, QK_ROPE, 1)).astype(k_ref.dtype)
        for h in range(HEADS):
            c0 = h * HEAD_W
            k_ref[rows, c0:c0 + QK_NOPE] = kn[:, h * QK_NOPE:(h + 1) * QK_NOPE].astype(k_ref.dtype)
            k_ref[rows, c0 + QK_NOPE:c0 + HEAD_W] = krot


def _inproj(x2d, cs, w, nb, nt, tm):
    rows, d = x2d.shape
    d_ssm = w["d_ssm"]
    n_in = w["win"].shape[1]
    row_spec = lambda width: pl.BlockSpec((tm, width), lambda b, j: (b * nt + j, 0))
    in_specs = [
        row_spec(d),
        _const_spec((1, d)),
        _const_spec(w["win"].shape),
        _const_spec(w["qg"].shape),
        _const_spec(w["wq"].shape),
        _const_spec(w["kvg"].shape),
        _const_spec(w["wk"].shape),
        _const_spec(w["wv"].shape),
        _const_spec(w["vone"].shape),
        pl.BlockSpec((tm, LANES), lambda b, j: (j, 0)),
    ]
    out_shape = (
        jax.ShapeDtypeStruct((rows, d_ssm), BF16),
        jax.ShapeDtypeStruct((rows, HEADS * HEAD_W), BF16),
        jax.ShapeDtypeStruct((rows, HEADS * HEAD_W), BF16),
        jax.ShapeDtypeStruct((nb, HEADS * VT_ROWS, nt * tm), BF16),
    )
    out_specs = (
        row_spec(d_ssm),
        row_spec(HEADS * HEAD_W),
        row_spec(HEADS * HEAD_W),
        pl.BlockSpec((None, HEADS * VT_ROWS, tm), lambda b, j: (b, 0, j)),
    )
    weights = 2 * (w["win"].size + w["wq"].size + w["wk"].size + w["wv"].size)
    tiles = 2 * tm * (4 * d + 2 * d_ssm + 4 * HEADS * HEAD_W + 2 * HEADS * V_HEAD + 4 * LANES)
    temps = 4 * tm * (d + n_in)
    return pl.pallas_call(
        functools.partial(_inproj_kernel, scale=w["scale"]),
        grid=(nb, nt),
        in_specs=in_specs,
        out_specs=out_specs,
        out_shape=out_shape,
        compiler_params=pltpu.CompilerParams(
            dimension_semantics=("arbitrary", "arbitrary"),
            vmem_limit_bytes=_vmem_limit(weights + tiles + temps)),
        name="inproj",
    )(x2d, w["mix_g"], w["win"], w["qg"], w["wq"], w["kvg"], w["wk"], w["wv"], w["vone"], cs)


def _s5_kernel(u_ref, h0_ref, perm_ref, bm_ref, cm_ref, are_ref, aim_ref, dskip_ref, wglu_ref, bglu_ref,
               y_ref, hT_ref, hbuf, state, *, steps):
    n_gb = bm_ref.shape[0]
    sw = are_ref.shape[2]
    cin = bm_ref.shape[1]
    nb, _, d_ssm = u_ref.shape

    @pl.when(pl.program_id(0) == 0)
    def _():
        state[...] = h0_ref[...]

    u = jnp.dot(perm_ref[0], u_ref[...].reshape(nb * steps, d_ssm),
                preferred_element_type=F32).astype(BF16)

    sub = steps // S5_SUB
    span = lambda k: slice(k * sub * SUBLANES, (k + 1) * sub * SUBLANES)
    for k in range(S5_SUB):
        for gb in range(n_gb):
            hbuf[span(k), 2 * sw * gb:2 * sw * (gb + 1)] = jnp.dot(
                u[span(k), cin * gb:cin * (gb + 1)], bm_ref[gb], preferred_element_type=F32)

    blocks = [(2 * sw * gb + o, 2 * sw * gb + sw + o, gb, o)
              for gb in range(n_gb) for o in range(0, sw, S5_SCAN_W)]
    carry = [(state[:, c_re:c_re + S5_SCAN_W], state[:, c_im:c_im + S5_SCAN_W])
             for c_re, c_im, _, _ in blocks]
    for k in range(S5_SUB):
        for j, (c_re, c_im, gb, o) in enumerate(blocks):
            are = are_ref[gb, :, o:o + S5_SCAN_W]
            aim = aim_ref[gb, :, o:o + S5_SCAN_W]
            hre, him = carry[j]
            for t in range(k * sub, (k + 1) * sub):
                r0 = t * SUBLANES
                nre = are * hre - aim * him + hbuf[r0:r0 + SUBLANES, c_re:c_re + S5_SCAN_W]
                nim = are * him + aim * hre + hbuf[r0:r0 + SUBLANES, c_im:c_im + S5_SCAN_W]
                hbuf[r0:r0 + SUBLANES, c_re:c_re + S5_SCAN_W] = nre
                hbuf[r0:r0 + SUBLANES, c_im:c_im + S5_SCAN_W] = nim
                hre, him = nre, nim
            carry[j] = (hre, him)
    for (c_re, c_im, _, _), (hre, him) in zip(blocks, carry):
        state[:, c_re:c_re + S5_SCAN_W] = hre
        state[:, c_im:c_im + S5_SCAN_W] = him

    outs = []
    for k in range(S5_SUB):
        ys = [jnp.dot(hbuf[span(k), 2 * sw * gb:2 * sw * (gb + 1)].astype(BF16), cm_ref[gb],
                      preferred_element_type=F32) for gb in range(n_gb)]
        y = jnp.concatenate(ys, axis=1) + dskip_ref[...] * u[span(k)].astype(F32)
        g = jax.nn.gelu(y)
        gate = jnp.dot(g.astype(BF16), wglu_ref[...], preferred_element_type=F32) + bglu_ref[...]
        outs.append((g * jax.nn.sigmoid(gate)).astype(BF16))
    out = jnp.concatenate(outs, axis=0)
    y_ref[...] = jnp.dot(perm_ref[1], out, preferred_element_type=F32).astype(y_ref.dtype).reshape(y_ref.shape)
    hT_ref[...] = state[...]


def _s5(u, h0, w, steps):
    nb, t_len, d_ssm = u.shape
    assert nb == SUBLANES
    r = steps * SUBLANES
    n_state = h0.shape[1]
    src = (jnp.arange(r) % SUBLANES) * steps + jnp.arange(r) // SUBLANES
    fwd = (src[:, None] == jnp.arange(r)[None, :]).astype(BF16)
    perm = jnp.stack([fwd, fwd.T])
    blk = pl.BlockSpec((nb, steps, d_ssm), lambda i: (0, i, 0))
    in_specs = [
        blk,
        _const_spec(h0.shape),
        _const_spec(perm.shape),
        _const_spec(w["bm"].shape),
        _const_spec(w["cm"].shape),
        _const_spec(w["are"].shape),
        _const_spec(w["aim"].shape),
        _const_spec((1, d_ssm)),
        _const_spec(w["wglu"].shape),
        _const_spec((1, d_ssm)),
    ]
    out_shape = (jax.ShapeDtypeStruct(u.shape, BF16),
                 jax.ShapeDtypeStruct(h0.shape, F32))
    out_specs = (blk, pl.BlockSpec(h0.shape, lambda i: (0, 0)))
    weights = (2 * (w["bm"].size + w["cm"].size + w["wglu"].size + perm.size)
               + 4 * (w["are"].size + w["aim"].size))
    tiles = 2 * r * d_ssm * 2 * 2 + 3 * 4 * h0.size
    scratch = 4 * r * n_state + 4 * h0.size
    temps = 2 * 4 * r * d_ssm
    return pl.pallas_call(
        functools.partial(_s5_kernel, steps=steps),
        grid=(t_len // steps,),
        in_specs=in_specs,
        out_specs=out_specs,
        out_shape=out_shape,
        scratch_shapes=[pltpu.VMEM((r, n_state), F32), pltpu.VMEM(h0.shape, F32)],
        compiler_params=pltpu.CompilerParams(
            dimension_semantics=("arbitrary",),
            vmem_limit_bytes=_vmem_limit(weights + tiles + scratch + temps)),
        name="s5",
    )(u, h0, perm, w["bm"], w["cm"], w["are"], w["aim"], w["dskip"], w["wglu"], w["bglu"])


def _attn_kernel(q_ref, k_ref, vt_ref, km_ref, vmt_ref, src_ref, o_ref, dst_ref):
    @pl.when(pl.program_id(1) == 0)
    def _():
        dst_ref[...] = src_ref[...].astype(dst_ref.dtype)

    seq = q_ref.shape[0]
    half = ATT_TQ // 2
    neg = jnp.finfo(F32).min
    kc = lax.broadcasted_iota(jnp.int32, (half, half), 0) // CHUNK
    qc = lax.broadcasted_iota(jnp.int32, (half, half), 1) // CHUNK
    visible = kc <= qc
    colmax = lambda t: jnp.max(t, axis=0, keepdims=True)
    prob = lambda s, m: jnp.exp2(s - m).astype(BF16)
    order = [(hh, i) for i in range(seq // ATT_TQ - 1, -1, -1) for hh in range(ATT_HEADS)]
    scores = {}
    for hh, i in order:
        r0 = i * ATT_TQ
        qs = slice(hh * HEAD_W, (hh + 1) * HEAD_W)
        q = q_ref[r0:r0 + ATT_TQ, qs]
        s_meta = lax.dot_general(km_ref[:, qs], q, _NT, preferred_element_type=F32)
        s_diag = lax.dot_general(k_ref[r0:r0 + ATT_TQ, qs], q, _NT, preferred_element_type=F32)
        s_full = (lax.dot_general(k_ref[0:r0, qs], q, _NT, preferred_element_type=F32)
                  if i > 0 else None)
        scores[hh, i] = (s_meta, s_diag, s_full)
    for hh, i in order:
        r0 = i * ATT_TQ
        vs = slice(hh * VT_ROWS, (hh + 1) * VT_ROWS)
        vmt = vmt_ref[vs, :]
        s_meta, s_diag, s_full = scores[hh, i]
        s00 = jnp.where(visible, s_diag[:half, :half], neg)
        s01 = s_diag[:half, half:]
        s11 = jnp.where(visible, s_diag[half:, half:], neg)
        m = jnp.concatenate([colmax(s00), jnp.maximum(colmax(s01), colmax(s11))], axis=1)
        m = jnp.maximum(m, colmax(s_meta))
        if i > 0:
            m = jnp.maximum(m, colmax(s_full))
        p_meta = prob(s_meta, m)
        p00 = prob(s00, m[:, :half])
        p01 = prob(s01, m[:, half:])
        p11 = prob(s11, m[:, half:])
        p_diag = jnp.concatenate([jnp.concatenate([p00, p01], axis=1),
                                  jnp.concatenate([jnp.zeros_like(p11), p11], axis=1)], axis=0)
        o = jnp.dot(vmt, p_meta, preferred_element_type=F32)
        o = o + jnp.dot(vt_ref[vs, r0:r0 + ATT_TQ], p_diag, preferred_element_type=F32)
        if i > 0:
            o = o + jnp.dot(vt_ref[vs, 0:r0], prob(s_full, m), preferred_element_type=F32)
        out = o[:V_HEAD] / o[V_HEAD:V_HEAD + 1]
        o_ref[r0:r0 + ATT_TQ, hh * V_HEAD:(hh + 1) * V_HEAD] = out.T.astype(o_ref.dtype)


def _attention(q, k, vt, km, vmt, cast_src, nb, seq):
    assert ATT_TQ % CHUNK == 0 and seq % ATT_TQ == 0 and V_HEAD == LANES and HEADS % ATT_HEADS == 0
    n_meta = km.shape[0]
    hw, vr = ATT_HEADS * HEAD_W, ATT_HEADS * VT_ROWS
    src_rows, src_cols = cast_src.shape
    blk = src_rows // nb
    assert blk * nb == src_rows and blk % (2 * SUBLANES) == 0
    cast_spec = pl.BlockSpec((blk, src_cols), lambda b, h: (b, 0))
    in_specs = [
        pl.BlockSpec((seq, hw), lambda b, h: (b, h)),
        pl.BlockSpec((seq, hw), lambda b, h: (b, h)),
        pl.BlockSpec((None, vr, seq), lambda b, h: (b, h, 0)),
        pl.BlockSpec((n_meta, hw), lambda b, h: (0, h)),
        pl.BlockSpec((vr, n_meta), lambda b, h: (h, 0)),
        cast_spec,
    ]
    tiles = 2 * 2 * seq * (2 * hw + vr + ATT_HEADS * V_HEAD) + 2 * blk * src_cols * (4 + 2)
    temps = 4 * 4 * seq * ATT_TQ * ATT_HEADS
    return pl.pallas_call(
        _attn_kernel,
        grid=(nb, HEADS // ATT_HEADS),
        in_specs=in_specs,
        out_specs=(pl.BlockSpec((seq, ATT_HEADS * V_HEAD), lambda b, h: (b, h)), cast_spec),
        out_shape=(jax.ShapeDtypeStruct((nb * seq, HEADS * V_HEAD), BF16),
                   jax.ShapeDtypeStruct(cast_src.shape, BF16)),
        compiler_params=pltpu.CompilerParams(
            dimension_semantics=("arbitrary", "arbitrary"),
            vmem_limit_bytes=_vmem_limit(tiles + temps)),
        name="attn",
    )(q, k, vt, km, vmt, cast_src)


def _attn_meta_kernel(q_ref, k_ref, vt_ref, o_ref):
    for h in range(HEADS):
        s = lax.dot_general(q_ref[:, h * HEAD_W:(h + 1) * HEAD_W],
                            k_ref[:, h * HEAD_W:(h + 1) * HEAD_W], _NT, preferred_element_type=F32)
        p = jnp.exp2(s - jnp.max(s, axis=1, keepdims=True))
        pv = lax.dot_general(p.astype(BF16), vt_ref[h * VT_ROWS:h * VT_ROWS + V_HEAD, :], _NT,
                             preferred_element_type=F32)
        o_ref[:, h * V_HEAD:(h + 1) * V_HEAD] = (pv / jnp.sum(p, axis=1, keepdims=True)).astype(o_ref.dtype)


def _attention_meta(q, k, vt):
    return pl.pallas_call(
        _attn_meta_kernel,
        out_shape=jax.ShapeDtypeStruct((q.shape[0], HEADS * V_HEAD), BF16),
        name="attn_meta",
    )(q, k, vt)


def _outproj_kernel(ya_ref, yb_ref, x_ref, ga_ref, gb_ref, wa_ref, wb_ref, o_ref):
    tm = x_ref.shape[0]
    n_sub = IN_SUB if tm % (IN_SUB * LANES) == 0 else 1
    ts = tm // n_sub
    for s in range(n_sub):
        rows = slice(s * ts, (s + 1) * ts)
        a = _rms(ya_ref[rows, :], ga_ref[...]).astype(BF16)
        b = _rms(yb_ref[rows, :], gb_ref[...]).astype(BF16)
        mix = jnp.dot(a, wa_ref[...], preferred_element_type=F32)
        mix = mix + jnp.dot(b, wb_ref[...], preferred_element_type=F32)
        o_ref[rows, :] = x_ref[rows, :].astype(F32) + mix


def _outproj_cast_kernel(ya_ref, yb_ref, x_ref, ga_ref, gb_ref, wa_ref, wb_ref, src_ref, o_ref, dst_ref):
    _outproj_kernel(ya_ref, yb_ref, x_ref, ga_ref, gb_ref, wa_ref, wb_ref, o_ref)
    dst_ref[...] = src_ref[...].astype(dst_ref.dtype)


def _outproj(ya, yb, x2d, w, nb, nt, tm, cast_src=None):
    rows, d = x2d.shape
    d_ssm, d_attn = w["wo_a"].shape[0], w["wo_b"].shape[0]
    row_spec = lambda width: pl.BlockSpec((tm, width), lambda b, j: (b * nt + j, 0))
    in_specs = [
        row_spec(d_ssm),
        row_spec(d_attn),
        row_spec(d),
        _const_spec((1, d_ssm)),
        _const_spec((1, d_attn)),
        _const_spec(w["wo_a"].shape),
        _const_spec(w["wo_b"].shape),
    ]
    args = [ya, yb, x2d, w["ga"], w["gb"], w["wo_a"], w["wo_b"]]
    out_specs = row_spec(d)
    out_shape = jax.ShapeDtypeStruct((rows, d), F32)
    weights = 2 * (w["wo_a"].size + w["wo_b"].size)
    tiles = 2 * tm * (2 * d_ssm + 2 * d_attn + 4 * d + 4 * d)
    temps = 4 * tm * d
    body = _outproj_kernel
    if cast_src is not None:
        src_rows, src_cols = cast_src.shape
        blk = src_rows // (nb * nt)
        assert blk * nb * nt == src_rows and blk % (2 * SUBLANES) == 0
        cast_spec = pl.BlockSpec((blk, src_cols), lambda b, j: (b * nt + j, 0))
        in_specs.append(cast_spec)
        args.append(cast_src)
        out_specs = (out_specs, cast_spec)
        out_shape = (out_shape, jax.ShapeDtypeStruct(cast_src.shape, BF16))
        tiles += 2 * blk * src_cols * (4 + 2)
        body = _outproj_cast_kernel
    return pl.pallas_call(
        body,
        grid=(nb, nt),
        in_specs=in_specs,
        out_specs=out_specs,
        out_shape=out_shape,
        compiler_params=pltpu.CompilerParams(
            dimension_semantics=("arbitrary", "arbitrary"),
            vmem_limit_bytes=_vmem_limit(weights + tiles + temps)),
        name="outproj",
    )(*args)


def _gate_kernel(h_ref, g_ref, wg_ref, o_ref):
    xn = _rms(h_ref[...], g_ref[...]).astype(BF16)
    o_ref[...] = jnp.dot(xn, wg_ref[...], preferred_element_type=F32)


def _meta_gate(h_meta, wu, w):
    rows, d = h_meta.shape
    n_chunks = w["conv_w"].shape[0]
    gate = pl.pallas_call(
        _gate_kernel,
        grid=(n_chunks,),
        in_specs=[_const_spec((rows, d)), _const_spec((1, d)),
                  pl.BlockSpec((d, FFN_TF), lambda j: (0, j))],
        out_specs=pl.BlockSpec((rows, FFN_TF), lambda j: (0, j)),
        out_shape=jax.ShapeDtypeStruct((rows, n_chunks * FFN_TF), F32),
        compiler_params=pltpu.CompilerParams(dimension_semantics=("arbitrary",)),
        name="meta_gate",
    )(h_meta, w["ffn_g"], wu)
    halo = gate[rows - SUBLANES:].reshape(SUBLANES, n_chunks, FFN_TF)
    return jnp.swapaxes(halo, 0, 1)


def _shift_rows(x, prev, k):
    rolled = pltpu.roll(x, k, 0)
    head = jnp.concatenate([prev, x[:SUBLANES]], axis=0)[SUBLANES - k:2 * SUBLANES - k]
    return jnp.concatenate([head, rolled[SUBLANES:]], axis=0)


def _ffn_kernel(h_ref, g_ref, wu_hbm, wd_hbm, cw_ref, cb_ref, mh_ref, fg_ref, o_ref,
                xn_scr, halo_scr, wg_buf, wv_buf, wd_buf, sem, *, tiles_per_batch, d_ff):
    i = pl.program_id(0)
    tm = h_ref.shape[0]
    ts = tm // FFN_SUB
    n_full = d_ff // FFN_TF
    rem = d_ff - n_full * FFN_TF
    assert 0 < rem and rem % LANES == 0

    def chunk_copies(col0, width, slot):
        return (
            pltpu.make_async_copy(wu_hbm.at[:, pl.ds(col0, width)],
                                  wg_buf.at[slot, :, pl.ds(0, width)], sem.at[slot, 0]),
            pltpu.make_async_copy(wu_hbm.at[:, pl.ds(d_ff + col0, width)],
                                  wv_buf.at[slot, :, pl.ds(0, width)], sem.at[slot, 1]),
            pltpu.make_async_copy(wd_hbm.at[pl.ds(col0, width), :],
                                  wd_buf.at[slot, pl.ds(0, width), :], sem.at[slot, 2]),
        )

    def start(copies):
        for c in copies:
            c.start()

    def wait(copies):
        for c in copies:
            c.wait()

    ragged = lambda slot: chunk_copies(n_full * FFN_TF, rem, slot)
    full = lambda c, slot: chunk_copies(pl.multiple_of(c * FFN_TF, FFN_TF), FFN_TF, slot)

    def compute(idx, width, slot):
        wg = wg_buf[slot, :, :width]
        wv = wv_buf[slot, :, :width]
        wd = wd_buf[slot, :width, :]
        cw = cw_ref[idx][:, :width]
        cb = cb_ref[idx][:, :width]
        prev = halo_scr[idx][:, :width]
        for s in range(FFN_SUB):
            rows = pl.ds(s * ts, ts)
            xn = xn_scr[rows, :]
            gate = jnp.dot(xn, wg, preferred_element_type=F32)
            val = jnp.dot(xn, wv, preferred_element_type=F32)
            conv = (cw[0:1] * _shift_rows(gate, prev, 2)
                    + cw[1:2] * _shift_rows(gate, prev, 1)
                    + cw[2:3] * gate + cb)
            prev = gate[ts - SUBLANES:, :]
            act = (jax.nn.silu(conv) * val).astype(BF16)
            o_ref[rows, :] += jnp.dot(act, wd, preferred_element_type=F32)
        halo_scr[idx, :, pl.ds(0, width)] = prev

    assert n_full >= 2
    slot0 = (i * (n_full + 1)) % 2

    @pl.when(i == 0)
    def _():
        start(full(0, slot0))

    @pl.when((i % tiles_per_batch) == 0)
    def _():
        halo_scr[...] = mh_ref[...]

    h = h_ref[...]
    xn_scr[...] = _rms(h, g_ref[...]).astype(BF16)
    o_ref[...] = h

    def body(c, carry):
        slot = (slot0 + c) % 2
        wait(full(c, slot))

        @pl.when(c + 1 < n_full)
        def _():
            start(full(c + 1, 1 - slot))

        @pl.when(c + 1 == n_full)
        def _():
            start(ragged(1 - slot))

        compute(c, FFN_TF, slot)
        return carry

    lax.fori_loop(0, n_full, body, 0)

    slot_r = (slot0 + n_full) % 2
    wait(ragged(slot_r))

    @pl.when(i + 1 < pl.num_programs(0))
    def _():
        start(full(0, 1 - slot_r))

    compute(n_full, rem, slot_r)
    o_ref[...] = _rms(o_ref[...], fg_ref[...])


def _ffn(h1, meta_halo, wu, wd, w, tiles_per_batch, tm):
    rows, d = h1.shape
    d_ff = wd.shape[0]
    n_chunks = w["conv_w"].shape[0]
    in_specs = [
        pl.BlockSpec((tm, d), lambda i: (i, 0)),
        _const_spec((1, d)),
        pl.BlockSpec(memory_space=pl.ANY),
        pl.BlockSpec(memory_space=pl.ANY),
        _const_spec(w["conv_w"].shape),
        _const_spec(w["conv_b"].shape),
        _const_spec(meta_halo.shape),
        _const_spec((1, d)),
    ]
    slots = 2
    tiles = 2 * 2 * 4 * tm * d + 4 * (w["conv_w"].size + w["conv_b"].size + meta_halo.size)
    scratch = 2 * tm * d + 4 * meta_halo.size + slots * 3 * 2 * d * FFN_TF
    temps = 4 * (tm // FFN_SUB) * FFN_TF * 16
    return pl.pallas_call(
        functools.partial(_ffn_kernel, tiles_per_batch=tiles_per_batch, d_ff=d_ff),
        grid=(rows // tm,),
        in_specs=in_specs,
        out_specs=pl.BlockSpec((tm, d), lambda i: (i, 0)),
        out_shape=jax.ShapeDtypeStruct((rows, d), F32),
        scratch_shapes=[
            pltpu.VMEM((tm, d), BF16),
            pltpu.VMEM((n_chunks, SUBLANES, FFN_TF), F32),
            pltpu.VMEM((slots, d, FFN_TF), BF16),
            pltpu.VMEM((slots, d, FFN_TF), BF16),
            pltpu.VMEM((slots, FFN_TF, d), BF16),
            pltpu.SemaphoreType.DMA((slots, 3)),
        ],
        compiler_params=pltpu.CompilerParams(
            dimension_semantics=("arbitrary",),
            vmem_limit_bytes=_vmem_limit(tiles + scratch + temps)),
        name="ffn",
    )(h1, w["ffn_g"], wu, wd, w["conv_w"], w["conv_b"], meta_halo, w["final_g"])


def _swap_halves(w):
    half = w.shape[-1] // 2
    return jnp.concatenate([-w[..., half:], w[..., :half]], axis=-1)


def _block_diag(blocks, n):
    rows, c = blocks.shape[-2:]
    r = rows // n
    rep = (jnp.arange(c)[:, None] == jnp.arange(n * c)[None, :] % c).astype(blocks.dtype)
    tiled = jnp.einsum("...rc,cq->...rq", blocks, rep, precision=lax.Precision.HIGHEST)
    keep = (jnp.arange(rows)[:, None] // r) == (jnp.arange(n * c)[None, :] // c)
    return jnp.where(keep, tiled, 0.0)


def _prepare(p):
    d_ssm = p["d_skip"].shape[-1]
    q_lora = p["q_a_norm"].shape[-1]
    kv_lora = p["kv_a_norm"].shape[-1]
    d_ff = p["w_down"].shape[0]
    w = {"d_ssm": d_ssm, "scale": math.log2(math.e) / math.sqrt(QK_NOPE + QK_ROPE)}

    row = lambda v: v.reshape(1, -1).astype(F32)
    w["mix_g"] = row(p["mix_norm"])
    w["win"] = p["w_in"].astype(BF16)
    w["qg"] = row(p["q_a_norm"])
    w["kvg"] = row(p["kv_a_norm"])
    wq = p["w_q_b"].reshape(q_lora, HEADS, QK_NOPE + QK_ROPE)
    wq_pe = wq[..., QK_NOPE:]
    w["wq"] = jnp.concatenate([wq, _swap_halves(wq_pe)], axis=-1).reshape(q_lora, HEADS * HEAD_W).astype(BF16)
    wkv = p["w_kv_b"].reshape(kv_lora, HEADS, QK_NOPE + V_HEAD)
    w["wk"] = wkv[..., :QK_NOPE].reshape(kv_lora, HEADS * QK_NOPE).astype(BF16)
    wv = jnp.pad(jnp.moveaxis(wkv[..., QK_NOPE:], 0, 2), ((0, 0), (0, VT_ROWS - V_HEAD), (0, 0)))
    w["wv"] = wv.reshape(HEADS * VT_ROWS, kv_lora).astype(BF16)
    w["vone"] = jnp.tile((jnp.arange(VT_ROWS) == V_HEAD).astype(F32), HEADS).reshape(-1, 1)

    lam_re, lam_im = p["lam_re"].astype(F32), p["lam_im"].astype(F32)
    dt = jnp.exp(p["log_dt"].astype(F32))[:, None]
    mag = jnp.exp(lam_re * dt)
    a_re, a_im = mag * jnp.cos(lam_im * dt), mag * jnp.sin(lam_im * dt)
    den = lam_re * lam_re + lam_im * lam_im
    f_re = ((a_re - 1.0) * lam_re + a_im * lam_im) / den
    f_im = (a_im * lam_re - (a_re - 1.0) * lam_im) / den
    b_re, b_im = p["b_re"].astype(F32), p["b_im"].astype(F32)
    bb_re = f_re[..., None] * b_re - f_im[..., None] * b_im
    bb_im = f_re[..., None] * b_im + f_im[..., None] * b_re
    n_g = lam_re.shape[0]
    n_gb = n_g // GROUPS_PER_MXU
    bd = lambda t: _block_diag(t.reshape(n_gb, GROUPS_PER_MXU * t.shape[1], t.shape[2]), GROUPS_PER_MXU)
    w["bm"] = jnp.concatenate([bd(jnp.swapaxes(bb_re, 1, 2)),
                               bd(jnp.swapaxes(bb_im, 1, 2))], axis=2).astype(BF16)
    c_re, c_im = p["c_re"].astype(F32), p["c_im"].astype(F32)
    w["cm"] = jnp.concatenate([bd(jnp.swapaxes(c_re, 1, 2)),
                               bd(jnp.swapaxes(-c_im, 1, 2))], axis=1).astype(BF16)
    bcast = lambda a: jnp.broadcast_to(a.reshape(n_gb, 1, STATE_W), (n_gb, SUBLANES, STATE_W))
    w["are"], w["aim"] = bcast(a_re), bcast(a_im)
    w["dskip"] = row(p["d_skip"])
    w["wglu"] = p["w_glu"].astype(BF16)
    w["bglu"] = row(p["b_glu"])

    w["ga"], w["gb"] = row(p["out_norm_ssm"]), row(p["out_norm_attn"])
    w["wo_a"] = p["w_out"][:d_ssm].astype(BF16)
    w["wo_b"] = p["w_out"][d_ssm:].astype(BF16)

    n_chunks = -(-d_ff // FFN_TF)
    pad = n_chunks * FFN_TF - d_ff
    per_chunk = lambda a: jnp.swapaxes(
        jnp.pad(a.astype(F32), ((0, 0), (0, pad))).reshape(a.shape[0], n_chunks, FFN_TF), 0, 1)
    w["conv_w"] = per_chunk(p["conv_w"])
    w["conv_b"] = per_chunk(p["conv_b"].reshape(1, -1))
    w["ffn_g"] = row(p["ffn_norm"])
    w["final_g"] = row(p["final_norm"])
    return w


def _rope_table(n_pos):
    pos = jnp.arange(n_pos, dtype=F32)
    inv_freq = 1.0 / (ROPE_BASE ** (jnp.arange(0, QK_ROPE, 2, dtype=F32) / QK_ROPE))
    ang = pos[:, None] * inv_freq[None, :]
    cos, sin = jnp.cos(ang), jnp.sin(ang)
    return jnp.concatenate([cos, cos, sin, sin], axis=1)


def kernel(x, meta_tokens, mix_norm, w_in, lam_re, lam_im, log_dt, b_re, b_im, c_re, c_im, d_skip, w_glu, b_glu, q_a_norm, w_q_b, kv_a_norm, w_kv_b, out_norm_ssm, out_norm_attn, w_out, ffn_norm, w_up, conv_w, conv_b, w_down, final_norm):
    bsz, seq, d = x.shape
    assert meta_tokens.shape == (N_META, d) and bsz == SUBLANES
    p = dict(mix_norm=mix_norm[0], w_in=w_in[0], lam_re=lam_re[0], lam_im=lam_im[0], log_dt=log_dt[0],
             b_re=b_re[0], b_im=b_im[0], c_re=c_re[0], c_im=c_im[0], d_skip=d_skip[0],
             w_glu=w_glu[0], b_glu=b_glu[0], q_a_norm=q_a_norm[0], w_q_b=w_q_b[0],
             kv_a_norm=kv_a_norm[0], w_kv_b=w_kv_b[0], out_norm_ssm=out_norm_ssm[0],
             out_norm_attn=out_norm_attn[0], w_out=w_out[0], ffn_norm=ffn_norm[0], w_up=w_up[0],
             conv_w=conv_w[0], conv_b=conv_b[0], w_down=w_down[0], final_norm=final_norm)
    w = _prepare(p)
    d_ssm = w["d_ssm"]
    cs = _rope_table(N_META + seq)
    x2d = x.reshape(bsz * seq, d)
    meta = meta_tokens.astype(x.dtype)

    u_m, q_m, k_m, v_m = _inproj(meta, cs[:N_META], w, 1, 1, N_META)
    n_state = 2 * STATE_W * w["bm"].shape[0]
    ya_m8, h_meta_state = _s5(jnp.broadcast_to(u_m[None], (SUBLANES, N_META, d_ssm)),
                              jnp.zeros((SUBLANES, n_state), F32), w, N_META)
    ya_m = ya_m8[0]
    yb_m = _attention_meta(q_m, k_m, v_m[0])
    h1_m = _outproj(ya_m, yb_m, meta, w, 1, 1, N_META)

    tm = 512
    nt = seq // tm
    u, q, k, v = _inproj(x2d, cs[N_META:], w, bsz, nt, tm)
    ya, _ = _s5(u.reshape(bsz, seq, d_ssm), h_meta_state, w, 64)
    yb, wd = _attention(q, k, v, k_m, v_m[0], w_down[0], bsz, seq)
    h1, wu = _outproj(ya.reshape(bsz * seq, d_ssm), yb, x2d, w, bsz, nt, tm, cast_src=w_up[0])
    meta_halo = _meta_gate(h1_m, wu, w)
    out = _ffn(h1, meta_halo, wu, wd, w, nt, tm)
    return out.reshape(bsz, seq, d)
```

```python
import functools
import math

import jax
import jax.numpy as jnp
from jax import lax
from jax.experimental import pallas as pl
from jax.experimental.pallas import tpu as pltpu

F32 = jnp.float32
BF16 = jnp.bfloat16

EPS = 1e-6
CHUNK = 64
N_META = 16
SSM_GROUP = 16
SSM_STATE = 64
HEADS = 8
QK_NOPE = 128
QK_ROPE = 64
V_HEAD = 128
ROPE_BASE = 10000.0
CONV_W = 3

LANES = 128
SUBLANES = 8
MXU_DIM = 256
VMEM_BYTES = 64 * 1024 * 1024

HEAD_W = QK_NOPE + 2 * QK_ROPE
GROUPS_PER_MXU = MXU_DIM // SSM_GROUP
STATE_W = GROUPS_PER_MXU * SSM_STATE

ATT_TQ = 512
ATT_HEADS = 2
VT_ROWS = V_HEAD + SUBLANES
FFN_TF = 1024
IN_SUB = 2
FFN_SUB = 2
S5_SUB = 2
S5_SCAN_W = 512


_NT = (((1,), (1,)), ((), ()))


def _vmem_limit(nbytes):
    return int(min(nbytes + (4 << 20), VMEM_BYTES - (4 << 20)))


def _rms(x, g):
    x = x.astype(F32)
    return x * lax.rsqrt(jnp.mean(x * x, axis=-1, keepdims=True) + EPS) * g


def _const_spec(shape):
    nd = len(shape)
    return pl.BlockSpec(shape, lambda *_: (0,) * nd, pipeline_mode=pl.Buffered(1))


def _inproj_kernel(x_ref, g_ref, win_ref, qg_ref, wq_ref, kvg_ref, wk_ref, wv_ref, vone_ref,
                   cs_ref, u_ref, q_ref, k_ref, v_ref, *, scale):
    d_ssm = u_ref.shape[1]
    q_lora = wq_ref.shape[0]
    kv_lora = wk_ref.shape[0]
    o1, o2, o3 = d_ssm, d_ssm + q_lora, d_ssm + q_lora + kv_lora
    tm = x_ref.shape[0]
    n_sub = IN_SUB if tm % (IN_SUB * LANES) == 0 else 1
    ts = tm // n_sub
    zs = []
    for s in range(n_sub):
        xn = _rms(x_ref[s * ts:(s + 1) * ts, :], g_ref[...]).astype(BF16)
        zs.append(jnp.dot(xn, win_ref[...], preferred_element_type=F32))
    for s, z in enumerate(zs):
        rows = slice(s * ts, (s + 1) * ts)
        u_ref[rows, :] = z[:, :o1].astype(u_ref.dtype)

        cs = cs_ref[rows, :]
        qn = _rms(z[:, o1:o2], qg_ref[...]).astype(BF16)
        q = jnp.dot(qn, wq_ref[...], preferred_element_type=F32)
        cs_q = cs * scale
        for h in range(HEADS):
            c0 = h * HEAD_W
            q_ref[rows, c0:c0 + QK_NOPE] = (q[:, c0:c0 + QK_NOPE] * scale).astype(q_ref.dtype)
            q_ref[rows, c0 + QK_NOPE:c0 + HEAD_W] = (q[:, c0 + QK_NOPE:c0 + HEAD_W] * cs_q).astype(q_ref.dtype)

        kvn = _rms(z[:, o2:o3], kvg_ref[...]).astype(BF16)
        kn = jnp.dot(kvn, wk_ref[...], preferred_element_type=F32)
        vt = lax.dot_general(wv_ref[...], kvn, _NT, preferred_element_type=F32)
        v_ref[:, rows] = (vt + vone_ref[...]).astype(v_ref.dtype)
        kpe = z[:, o3:o3 + QK_ROPE]
        x1, x2 = kpe[:, :QK_ROPE // 2], kpe[:, QK_ROPE // 2:]
        t = jnp.concatenate([kpe, -x2, x1], axis=1) * cs
        krot = (t + pltpu.roll(t, QK_ROPE, 1)).astype(k_ref.dtype)
        for h in range(HEADS):
            c0 = h * HEAD_W
            k_ref[rows, c0:c0 + QK_NOPE] = kn[:, h * QK_NOPE:(h + 1) * QK_NOPE].astype(k_ref.dtype)
            k_ref[rows, c0 + QK_NOPE:c0 + HEAD_W] = krot


def _inproj(x2d, cs, w, nb, nt, tm):
    rows, d = x2d.shape
    d_ssm = w["d_ssm"]
    n_in = w["win"].shape[1]
    row_spec = lambda width: pl.BlockSpec((tm, width), lambda b, j: (b * nt + j, 0))
    in_specs = [
        row_spec(d),
        _const_spec((1, d)),
        _const_spec(w["win"].shape),
        _const_spec(w["qg"].shape),
        _const_spec(w["wq"].shape),
        _const_spec(w["kvg"].shape),
        _const_spec(w["wk"].shape),
        _const_spec(w["wv"].shape),
        _const_spec(w["vone"].shape),
        pl.BlockSpec((tm, LANES), lambda b, j: (j, 0)),
    ]
    out_shape = (
        jax.ShapeDtypeStruct((rows, d_ssm), BF16),
        jax.ShapeDtypeStruct((rows, HEADS * HEAD_W), BF16),
        jax.ShapeDtypeStruct((rows, HEADS * HEAD_W), BF16),
        jax.ShapeDtypeStruct((nb, HEADS * VT_ROWS, nt * tm), BF16),
    )
    out_specs = (
        row_spec(d_ssm),
        row_spec(HEADS * HEAD_W),
        row_spec(HEADS * HEAD_W),
        pl.BlockSpec((None, HEADS * VT_ROWS, tm), lambda b, j: (b, 0, j)),
    )
    weights = 2 * (w["win"].size + w["wq"].size + w["wk"].size + w["wv"].size)
    tiles = 2 * tm * (4 * d + 2 * d_ssm + 4 * HEADS * HEAD_W + 2 * HEADS * V_HEAD + 4 * LANES)
    temps = 4 * tm * (d + n_in)
    return pl.pallas_call(
        functools.partial(_inproj_kernel, scale=w["scale"]),
        grid=(nb, nt),
        in_specs=in_specs,
        out_specs=out_specs,
        out_shape=out_shape,
        compiler_params=pltpu.CompilerParams(
            dimension_semantics=("arbitrary", "arbitrary"),
            vmem_limit_bytes=_vmem_limit(weights + tiles + temps)),
        name="inproj",
    )(x2d, w["mix_g"], w["win"], w["qg"], w["wq"], w["kvg"], w["wk"], w["wv"], w["vone"], cs)


def _s5_kernel(u_ref, h0_ref, bm_ref, cm_ref, are_ref, aim_ref, dskip_ref, wglu_ref, bglu_ref,
               y_ref, hT_ref, hbuf, state, *, steps):
    n_gb = bm_ref.shape[0]
    sw = are_ref.shape[2]
    cin = bm_ref.shape[1]
    nb, _, d_ssm = u_ref.shape

    @pl.when(pl.program_id(0) == 0)
    def _():
        state[...] = h0_ref[...]

    u = jnp.swapaxes(u_ref[...].astype(F32), 0, 1).reshape(nb * steps, d_ssm).astype(BF16)

    sub = steps // S5_SUB
    span = lambda k: slice(k * sub * SUBLANES, (k + 1) * sub * SUBLANES)
    for k in range(S5_SUB):
        for gb in range(n_gb):
            hbuf[span(k), 2 * sw * gb:2 * sw * (gb + 1)] = jnp.dot(
                u[span(k), cin * gb:cin * (gb + 1)], bm_ref[gb], preferred_element_type=F32)

    blocks = [(2 * sw * gb + o, 2 * sw * gb + sw + o, gb, o)
              for gb in range(n_gb) for o in range(0, sw, S5_SCAN_W)]
    carry = [(state[:, c_re:c_re + S5_SCAN_W], state[:, c_im:c_im + S5_SCAN_W])
             for c_re, c_im, _, _ in blocks]
    for k in range(S5_SUB):
        for j, (c_re, c_im, gb, o) in enumerate(blocks):
            are = are_ref[gb, :, o:o + S5_SCAN_W]
            aim = aim_ref[gb, :, o:o + S5_SCAN_W]
            hre, him = carry[j]
            for t in range(k * sub, (k + 1) * sub):
                r0 = t * SUBLANES
                nre = are * hre - aim * him + hbuf[r0:r0 + SUBLANES, c_re:c_re + S5_SCAN_W]
                nim = are * him + aim * hre + hbuf[r0:r0 + SUBLANES, c_im:c_im + S5_SCAN_W]
                hbuf[r0:r0 + SUBLANES, c_re:c_re + S5_SCAN_W] = nre
                hbuf[r0:r0 + SUBLANES, c_im:c_im + S5_SCAN_W] = nim
                hre, him = nre, nim
            carry[j] = (hre, him)
    for (c_re, c_im, _, _), (hre, him) in zip(blocks, carry):
        state[:, c_re:c_re + S5_SCAN_W] = hre
        state[:, c_im:c_im + S5_SCAN_W] = him

    outs = []
    for k in range(S5_SUB):
        ys = [jnp.dot(hbuf[span(k), 2 * sw * gb:2 * sw * (gb + 1)].astype(BF16), cm_ref[gb],
                      preferred_element_type=F32) for gb in range(n_gb)]
        y = jnp.concatenate(ys, axis=1) + dskip_ref[...] * u[span(k)].astype(F32)
        g = jax.nn.gelu(y)
        gate = jnp.dot(g.astype(BF16), wglu_ref[...], preferred_element_type=F32) + bglu_ref[...]
        outs.append(g * jax.nn.sigmoid(gate))
    out = jnp.concatenate(outs, axis=0).reshape(steps, nb, d_ssm)
    y_ref[...] = jnp.swapaxes(out, 0, 1).astype(y_ref.dtype)
    hT_ref[...] = state[...]


def _s5(u, h0, w, steps):
    nb, t_len, d_ssm = u.shape
    assert nb == SUBLANES
    r = steps * SUBLANES
    n_state = h0.shape[1]
    blk = pl.BlockSpec((nb, steps, d_ssm), lambda i: (0, i, 0))
    in_specs = [
        blk,
        _const_spec(h0.shape),
        _const_spec(w["bm"].shape),
        _const_spec(w["cm"].shape),
        _const_spec(w["are"].shape),
        _const_spec(w["aim"].shape),
        _const_spec((1, d_ssm)),
        _const_spec(w["wglu"].shape),
        _const_spec((1, d_ssm)),
    ]
    out_shape = (jax.ShapeDtypeStruct(u.shape, BF16),
                 jax.ShapeDtypeStruct(h0.shape, F32))
    out_specs = (blk, pl.BlockSpec(h0.shape, lambda i: (0, 0)))
    weights = 2 * (w["bm"].size + w["cm"].size + w["wglu"].size) + 4 * (w["are"].size + w["aim"].size)
    tiles = 2 * r * d_ssm * 2 * 2 + 3 * 4 * h0.size
    scratch = 4 * r * n_state + 4 * h0.size
    temps = 2 * 4 * r * d_ssm
    return pl.pallas_call(
        functools.partial(_s5_kernel, steps=steps),
        grid=(t_len // steps,),
        in_specs=in_specs,
        out_specs=out_specs,
        out_shape=out_shape,
        scratch_shapes=[pltpu.VMEM((r, n_state), F32), pltpu.VMEM(h0.shape, F32)],
        compiler_params=pltpu.CompilerParams(
            dimension_semantics=("arbitrary",),
            vmem_limit_bytes=_vmem_limit(weights + tiles + scratch + temps)),
        name="s5",
    )(u, h0, w["bm"], w["cm"], w["are"], w["aim"], w["dskip"], w["wglu"], w["bglu"])


def _attn_kernel(q_ref, k_ref, vt_ref, km_ref, vmt_ref, src_ref, o_ref, dst_ref):
    @pl.when(pl.program_id(1) == 0)
    def _():
        dst_ref[...] = src_ref[...].astype(dst_ref.dtype)

    seq = q_ref.shape[0]
    half = ATT_TQ // 2
    neg = jnp.finfo(F32).min
    kc = lax.broadcasted_iota(jnp.int32, (half, half), 0) // CHUNK
    qc = lax.broadcasted_iota(jnp.int32, (half, half), 1) // CHUNK
    visible = kc <= qc
    colmax = lambda t: jnp.max(t, axis=0, keepdims=True)
    prob = lambda s, m: jnp.exp2(s - m).astype(BF16)
    order = [(hh, i) for i in range(seq // ATT_TQ - 1, -1, -1) for hh in range(ATT_HEADS)]
    scores = {}
    for hh, i in order:
        r0 = i * ATT_TQ
        qs = slice(hh * HEAD_W, (hh + 1) * HEAD_W)
        q = q_ref[r0:r0 + ATT_TQ, qs]
        s_meta = lax.dot_general(km_ref[:, qs], q, _NT, preferred_element_type=F32)
        s_diag = lax.dot_general(k_ref[r0:r0 + ATT_TQ, qs], q, _NT, preferred_element_type=F32)
        s_full = (lax.dot_general(k_ref[0:r0, qs], q, _NT, preferred_element_type=F32)
                  if i > 0 else None)
        scores[hh, i] = (s_meta, s_diag, s_full)
    for hh, i in order:
        r0 = i * ATT_TQ
        vs = slice(hh * VT_ROWS, (hh + 1) * VT_ROWS)
        vmt = vmt_ref[vs, :]
        s_meta, s_diag, s_full = scores[hh, i]
        s00 = jnp.where(visible, s_diag[:half, :half], neg)
        s01 = s_diag[:half, half:]
        s11 = jnp.where(visible, s_diag[half:, half:], neg)
        m = jnp.concatenate([colmax(s00), jnp.maximum(colmax(s01), colmax(s11))], axis=1)
        m = jnp.maximum(m, colmax(s_meta))
        if i > 0:
            m = jnp.maximum(m, colmax(s_full))
        p_meta = prob(s_meta, m)
        p00 = prob(s00, m[:, :half])
        p01 = prob(s01, m[:, half:])
        p11 = prob(s11, m[:, half:])
        p_diag = jnp.concatenate([jnp.concatenate([p00, p01], axis=1),
                                  jnp.concatenate([jnp.zeros_like(p11), p11], axis=1)], axis=0)
        o = jnp.dot(vmt, p_meta, preferred_element_type=F32)
        o = o + jnp.dot(vt_ref[vs, r0:r0 + ATT_TQ], p_diag, preferred_element_type=F32)
        if i > 0:
            o = o + jnp.dot(vt_ref[vs, 0:r0], prob(s_full, m), preferred_element_type=F32)
        out = o[:V_HEAD] / o[V_HEAD:V_HEAD + 1]
        o_ref[r0:r0 + ATT_TQ, hh * V_HEAD:(hh + 1) * V_HEAD] = out.T.astype(o_ref.dtype)


def _attention(q, k, vt, km, vmt, cast_src, nb, seq):
    assert ATT_TQ % CHUNK == 0 and seq % ATT_TQ == 0 and V_HEAD == LANES and HEADS % ATT_HEADS == 0
    n_meta = km.shape[0]
    hw, vr = ATT_HEADS * HEAD_W, ATT_HEADS * VT_ROWS
    src_rows, src_cols = cast_src.shape
    blk = src_rows // nb
    assert blk * nb == src_rows and blk % (2 * SUBLANES) == 0
    cast_spec = pl.BlockSpec((blk, src_cols), lambda b, h: (b, 0))
    in_specs = [
        pl.BlockSpec((seq, hw), lambda b, h: (b, h)),
        pl.BlockSpec((seq, hw), lambda b, h: (b, h)),
        pl.BlockSpec((None, vr, seq), lambda b, h: (b, h, 0)),
        pl.BlockSpec((n_meta, hw), lambda b, h: (0, h)),
        pl.BlockSpec((vr, n_meta), lambda b, h: (h, 0)),
        cast_spec,
    ]
    tiles = 2 * 2 * seq * (2 * hw + vr + ATT_HEADS * V_HEAD) + 2 * blk * src_cols * (4 + 2)
    temps = 4 * 4 * seq * ATT_TQ * ATT_HEADS
    return pl.pallas_call(
        _attn_kernel,
        grid=(nb, HEADS // ATT_HEADS),
        in_specs=in_specs,
        out_specs=(pl.BlockSpec((seq, ATT_HEADS * V_HEAD), lambda b, h: (b, h)), cast_spec),
        out_shape=(jax.ShapeDtypeStruct((nb * seq, HEADS * V_HEAD), BF16),
                   jax.ShapeDtypeStruct(cast_src.shape, BF16)),
        compiler_params=pltpu.CompilerParams(
            dimension_semantics=("arbitrary", "arbitrary"),
            vmem_limit_bytes=_vmem_limit(tiles + temps)),
        name="attn",
    )(q, k, vt, km, vmt, cast_src)


def _attn_meta_kernel(q_ref, k_ref, vt_ref, o_ref):
    for h in range(HEADS):
        s = lax.dot_general(q_ref[:, h * HEAD_W:(h + 1) * HEAD_W],
                            k_ref[:, h * HEAD_W:(h + 1) * HEAD_W], _NT, preferred_element_type=F32)
        p = jnp.exp2(s - jnp.max(s, axis=1, keepdims=True))
        pv = lax.dot_general(p.astype(BF16), vt_ref[h * VT_ROWS:h * VT_ROWS + V_HEAD, :], _NT,
                             preferred_element_type=F32)
        o_ref[:, h * V_HEAD:(h + 1) * V_HEAD] = (pv / jnp.sum(p, axis=1, keepdims=True)).astype(o_ref.dtype)


def _attention_meta(q, k, vt):
    return pl.pallas_call(
        _attn_meta_kernel,
        out_shape=jax.ShapeDtypeStruct((q.shape[0], HEADS * V_HEAD), BF16),
        name="attn_meta",
    )(q, k, vt)


def _outproj_kernel(ya_ref, yb_ref, x_ref, ga_ref, gb_ref, wa_ref, wb_ref, o_ref):
    tm = x_ref.shape[0]
    n_sub = IN_SUB if tm % (IN_SUB * LANES) == 0 else 1
    ts = tm // n_sub
    for s in range(n_sub):
        rows = slice(s * ts, (s + 1) * ts)
        a = _rms(ya_ref[rows, :], ga_ref[...]).astype(BF16)
        b = _rms(yb_ref[rows, :], gb_ref[...]).astype(BF16)
        mix = jnp.dot(a, wa_ref[...], preferred_element_type=F32)
        mix = mix + jnp.dot(b, wb_ref[...], preferred_element_type=F32)
        o_ref[rows, :] = x_ref[rows, :].astype(F32) + mix


def _outproj_cast_kernel(ya_ref, yb_ref, x_ref, ga_ref, gb_ref, wa_ref, wb_ref, src_ref, o_ref, dst_ref):
    _outproj_kernel(ya_ref, yb_ref, x_ref, ga_ref, gb_ref, wa_ref, wb_ref, o_ref)
    dst_ref[...] = src_ref[...].astype(dst_ref.dtype)


def _outproj(ya, yb, x2d, w, nb, nt, tm, cast_src=None):
    rows, d = x2d.shape
    d_ssm, d_attn = w["wo_a"].shape[0], w["wo_b"].shape[0]
    row_spec = lambda width: pl.BlockSpec((tm, width), lambda b, j: (b * nt + j, 0))
    in_specs = [
        row_spec(d_ssm),
        row_spec(d_attn),
        row_spec(d),
        _const_spec((1, d_ssm)),
        _const_spec((1, d_attn)),
        _const_spec(w["wo_a"].shape),
        _const_spec(w["wo_b"].shape),
    ]
    args = [ya, yb, x2d, w["ga"], w["gb"], w["wo_a"], w["wo_b"]]
    out_specs = row_spec(d)
    out_shape = jax.ShapeDtypeStruct((rows, d), F32)
    weights = 2 * (w["wo_a"].size + w["wo_b"].size)
    tiles = 2 * tm * (2 * d_ssm + 2 * d_attn + 4 * d + 4 * d)
    temps = 4 * tm * d
    body = _outproj_kernel
    if cast_src is not None:
        src_rows, src_cols = cast_src.shape
        blk = src_rows // (nb * nt)
        assert blk * nb * nt == src_rows and blk % (2 * SUBLANES) == 0
        cast_spec = pl.BlockSpec((blk, src_cols), lambda b, j: (b * nt + j, 0))
        in_specs.append(cast_spec)
        args.append(cast_src)
        out_specs = (out_specs, cast_spec)
        out_shape = (out_shape, jax.ShapeDtypeStruct(cast_src.shape, BF16))
        tiles += 2 * blk * src_cols * (4 + 2)
        body = _outproj_cast_kernel
    return pl.pallas_call(
        body,
        grid=(nb, nt),
        in_specs=in_specs,
        out_specs=out_specs,
        out_shape=out_shape,
        compiler_params=pltpu.CompilerParams(
            dimension_semantics=("arbitrary", "arbitrary"),
            vmem_limit_bytes=_vmem_limit(weights + tiles + temps)),
        name="outproj",
    )(*args)


def _gate_kernel(h_ref, g_ref, wg_ref, o_ref):
    xn = _rms(h_ref[...], g_ref[...]).astype(BF16)
    o_ref[...] = jnp.dot(xn, wg_ref[...], preferred_element_type=F32)


def _meta_gate(h_meta, wu, w):
    rows, d = h_meta.shape
    n_chunks = w["conv_w"].shape[0]
    gate = pl.pallas_call(
        _gate_kernel,
        grid=(n_chunks,),
        in_specs=[_const_spec((rows, d)), _const_spec((1, d)),
                  pl.BlockSpec((d, FFN_TF), lambda j: (0, j))],
        out_specs=pl.BlockSpec((rows, FFN_TF), lambda j: (0, j)),
        out_shape=jax.ShapeDtypeStruct((rows, n_chunks * FFN_TF), F32),
        compiler_params=pltpu.CompilerParams(dimension_semantics=("arbitrary",)),
        name="meta_gate",
    )(h_meta, w["ffn_g"], wu)
    halo = gate[rows - SUBLANES:].reshape(SUBLANES, n_chunks, FFN_TF)
    return jnp.swapaxes(halo, 0, 1)


def _shift_rows(x, prev, k):
    rolled = pltpu.roll(x, k, 0)
    head = jnp.concatenate([prev, x[:SUBLANES]], axis=0)[SUBLANES - k:2 * SUBLANES - k]
    return jnp.concatenate([head, rolled[SUBLANES:]], axis=0)


def _ffn_kernel(h_ref, g_ref, wu_hbm, wd_hbm, cw_ref, cb_ref, mh_ref, fg_ref, o_ref,
                xn_scr, halo_scr, wg_buf, wv_buf, wd_buf, sem, *, tiles_per_batch, d_ff):
    i = pl.program_id(0)
    tm = h_ref.shape[0]
    ts = tm // FFN_SUB
    n_full = d_ff // FFN_TF
    rem = d_ff - n_full * FFN_TF
    assert 0 < rem and rem % LANES == 0

    def chunk_copies(col0, width, slot):
        return (
            pltpu.make_async_copy(wu_hbm.at[:, pl.ds(col0, width)],
                                  wg_buf.at[slot, :, pl.ds(0, width)], sem.at[slot, 0]),
            pltpu.make_async_copy(wu_hbm.at[:, pl.ds(d_ff + col0, width)],
                                  wv_buf.at[slot, :, pl.ds(0, width)], sem.at[slot, 1]),
            pltpu.make_async_copy(wd_hbm.at[pl.ds(col0, width), :],
                                  wd_buf.at[slot, pl.ds(0, width), :], sem.at[slot, 2]),
        )

    def start(copies):
        for c in copies:
            c.start()

    def wait(copies):
        for c in copies:
            c.wait()

    ragged = lambda slot: chunk_copies(n_full * FFN_TF, rem, slot)
    full = lambda c, slot: chunk_copies(pl.multiple_of(c * FFN_TF, FFN_TF), FFN_TF, slot)

    def compute(idx, width, slot):
        wg = wg_buf[slot, :, :width]
        wv = wv_buf[slot, :, :width]
        wd = wd_buf[slot, :width, :]
        cw = cw_ref[idx][:, :width]
        cb = cb_ref[idx][:, :width]
        prev = halo_scr[idx][:, :width]
        for s in range(FFN_SUB):
            rows = pl.ds(s * ts, ts)
            xn = xn_scr[rows, :]
            gate = jnp.dot(xn, wg, preferred_element_type=F32)
            val = jnp.dot(xn, wv, preferred_element_type=F32)
            conv = (cw[0:1] * _shift_rows(gate, prev, 2)
                    + cw[1:2] * _shift_rows(gate, prev, 1)
                    + cw[2:3] * gate + cb)
            prev = gate[ts - SUBLANES:, :]
            act = (jax.nn.silu(conv) * val).astype(BF16)
            o_ref[rows, :] += jnp.dot(act, wd, preferred_element_type=F32)
        halo_scr[idx, :, pl.ds(0, width)] = prev

    assert n_full >= 2
    slot0 = (i * (n_full + 1)) % 2

    @pl.when(i == 0)
    def _():
        start(full(0, slot0))

    @pl.when((i % tiles_per_batch) == 0)
    def _():
        halo_scr[...] = mh_ref[...]

    h = h_ref[...]
    xn_scr[...] = _rms(h, g_ref[...]).astype(BF16)
    o_ref[...] = h

    def body(c, carry):
        slot = (slot0 + c) % 2
        wait(full(c, slot))

        @pl.when(c + 1 < n_full)
        def _():
            start(full(c + 1, 1 - slot))

        @pl.when(c + 1 == n_full)
        def _():
            start(ragged(1 - slot))

        compute(c, FFN_TF, slot)
        return carry

    lax.fori_loop(0, n_full, body, 0)

    slot_r = (slot0 + n_full) % 2
    wait(ragged(slot_r))

    @pl.when(i + 1 < pl.num_programs(0))
    def _():
        start(full(0, 1 - slot_r))

    compute(n_full, rem, slot_r)
    o_ref[...] = _rms(o_ref[...], fg_ref[...])


def _ffn(h1, meta_halo, wu, wd, w, tiles_per_batch, tm):
    rows, d = h1.shape
    d_ff = wd.shape[0]
    n_chunks = w["conv_w"].shape[0]
    in_specs = [
        pl.BlockSpec((tm, d), lambda i: (i, 0)),
        _const_spec((1, d)),
        pl.BlockSpec(memory_space=pl.ANY),
        pl.BlockSpec(memory_space=pl.ANY),
        _const_spec(w["conv_w"].shape),
        _const_spec(w["conv_b"].shape),
        _const_spec(meta_halo.shape),
        _const_spec((1, d)),
    ]
    slots = 2
    tiles = 2 * 2 * 4 * tm * d + 4 * (w["conv_w"].size + w["conv_b"].size + meta_halo.size)
    scratch = 2 * tm * d + 4 * meta_halo.size + slots * 3 * 2 * d * FFN_TF
    temps = 4 * (tm // FFN_SUB) * FFN_TF * 16
    return pl.pallas_call(
        functools.partial(_ffn_kernel, tiles_per_batch=tiles_per_batch, d_ff=d_ff),
        grid=(rows // tm,),
        in_specs=in_specs,
        out_specs=pl.BlockSpec((tm, d), lambda i: (i, 0)),
        out_shape=jax.ShapeDtypeStruct((rows, d), F32),
        scratch_shapes=[
            pltpu.VMEM((tm, d), BF16),
            pltpu.VMEM((n_chunks, SUBLANES, FFN_TF), F32),
            pltpu.VMEM((slots, d, FFN_TF), BF16),
            pltpu.VMEM((slots, d, FFN_TF), BF16),
            pltpu.VMEM((slots, FFN_TF, d), BF16),
            pltpu.SemaphoreType.DMA((slots, 3)),
        ],
        compiler_params=pltpu.CompilerParams(
            dimension_semantics=("arbitrary",),
            vmem_limit_bytes=_vmem_limit(tiles + scratch + temps)),
        name="ffn",
    )(h1, w["ffn_g"], wu, wd, w["conv_w"], w["conv_b"], meta_halo, w["final_g"])


def _swap_halves(w):
    half = w.shape[-1] // 2
    return jnp.concatenate([-w[..., half:], w[..., :half]], axis=-1)


def _block_diag(blocks, n):
    rows, c = blocks.shape[-2:]
    r = rows // n
    rep = (jnp.arange(c)[:, None] == jnp.arange(n * c)[None, :] % c).astype(blocks.dtype)
    tiled = jnp.einsum("...rc,cq->...rq", blocks, rep, precision=lax.Precision.HIGHEST)
    keep = (jnp.arange(rows)[:, None] // r) == (jnp.arange(n * c)[None, :] // c)
    return jnp.where(keep, tiled, 0.0)


def _prepare(p):
    d_ssm = p["d_skip"].shape[-1]
    q_lora = p["q_a_norm"].shape[-1]
    kv_lora = p["kv_a_norm"].shape[-1]
    d_ff = p["w_down"].shape[0]
    w = {"d_ssm": d_ssm, "scale": math.log2(math.e) / math.sqrt(QK_NOPE + QK_ROPE)}

    row = lambda v: v.reshape(1, -1).astype(F32)
    w["mix_g"] = row(p["mix_norm"])
    w["win"] = p["w_in"].astype(BF16)
    w["qg"] = row(p["q_a_norm"])
    w["kvg"] = row(p["kv_a_norm"])
    wq = p["w_q_b"].reshape(q_lora, HEADS, QK_NOPE + QK_ROPE)
    wq_pe = wq[..., QK_NOPE:]
    w["wq"] = jnp.concatenate([wq, _swap_halves(wq_pe)], axis=-1).reshape(q_lora, HEADS * HEAD_W).astype(BF16)
    wkv = p["w_kv_b"].reshape(kv_lora, HEADS, QK_NOPE + V_HEAD)
    w["wk"] = wkv[..., :QK_NOPE].reshape(kv_lora, HEADS * QK_NOPE).astype(BF16)
    wv = jnp.pad(jnp.moveaxis(wkv[..., QK_NOPE:], 0, 2), ((0, 0), (0, VT_ROWS - V_HEAD), (0, 0)))
    w["wv"] = wv.reshape(HEADS * VT_ROWS, kv_lora).astype(BF16)
    w["vone"] = jnp.tile((jnp.arange(VT_ROWS) == V_HEAD).astype(F32), HEADS).reshape(-1, 1)

    lam_re, lam_im = p["lam_re"].astype(F32), p["lam_im"].astype(F32)
    dt = jnp.exp(p["log_dt"].astype(F32))[:, None]
    mag = jnp.exp(lam_re * dt)
    a_re, a_im = mag * jnp.cos(lam_im * dt), mag * jnp.sin(lam_im * dt)
    den = lam_re * lam_re + lam_im * lam_im
    f_re = ((a_re - 1.0) * lam_re + a_im * lam_im) / den
    f_im = (a_im * lam_re - (a_re - 1.0) * lam_im) / den
    b_re, b_im = p["b_re"].astype(F32), p["b_im"].astype(F32)
    bb_re = f_re[..., None] * b_re - f_im[..., None] * b_im
    bb_im = f_re[..., None] * b_im + f_im[..., None] * b_re
    n_g = lam_re.shape[0]
    n_gb = n_g // GROUPS_PER_MXU
    bd = lambda t: _block_diag(t.reshape(n_gb, GROUPS_PER_MXU * t.shape[1], t.shape[2]), GROUPS_PER_MXU)
    w["bm"] = jnp.concatenate([bd(jnp.swapaxes(bb_re, 1, 2)),
                               bd(jnp.swapaxes(bb_im, 1, 2))], axis=2).astype(BF16)
    c_re, c_im = p["c_re"].astype(F32), p["c_im"].astype(F32)
    w["cm"] = jnp.concatenate([bd(jnp.swapaxes(c_re, 1, 2)),
                               bd(jnp.swapaxes(-c_im, 1, 2))], axis=1).astype(BF16)
    bcast = lambda a: jnp.broadcast_to(a.reshape(n_gb, 1, STATE_W), (n_gb, SUBLANES, STATE_W))
    w["are"], w["aim"] = bcast(a_re), bcast(a_im)
    w["dskip"] = row(p["d_skip"])
    w["wglu"] = p["w_glu"].astype(BF16)
    w["bglu"] = row(p["b_glu"])

    w["ga"], w["gb"] = row(p["out_norm_ssm"]), row(p["out_norm_attn"])
    w["wo_a"] = p["w_out"][:d_ssm].astype(BF16)
    w["wo_b"] = p["w_out"][d_ssm:].astype(BF16)

    n_chunks = -(-d_ff // FFN_TF)
    pad = n_chunks * FFN_TF - d_ff
    per_chunk = lambda a: jnp.swapaxes(
        jnp.pad(a.astype(F32), ((0, 0), (0, pad))).reshape(a.shape[0], n_chunks, FFN_TF), 0, 1)
    w["conv_w"] = per_chunk(p["conv_w"])
    w["conv_b"] = per_chunk(p["conv_b"].reshape(1, -1))
    w["ffn_g"] = row(p["ffn_norm"])
    w["final_g"] = row(p["final_norm"])
    return w


def _rope_table(n_pos):
    pos = jnp.arange(n_pos, dtype=F32)
    inv_freq = 1.0 / (ROPE_BASE ** (jnp.arange(0, QK_ROPE, 2, dtype=F32) / QK_ROPE))
    ang = pos[:, None] * inv_freq[None, :]
    cos, sin = jnp.cos(ang), jnp.sin(ang)
    return jnp.concatenate([cos, cos, sin, sin], axis=1)


def kernel(x, meta_tokens, mix_norm, w_in, lam_re, lam_im, log_dt, b_re, b_im, c_re, c_im, d_skip, w_glu, b_glu, q_a_norm, w_q_b, kv_a_norm, w_kv_b, out_norm_ssm, out_norm_attn, w_out, ffn_norm, w_up, conv_w, conv_b, w_down, final_norm):
    bsz, seq, d = x.shape
    assert meta_tokens.shape == (N_META, d) and bsz == SUBLANES
    p = dict(mix_norm=mix_norm[0], w_in=w_in[0], lam_re=lam_re[0], lam_im=lam_im[0], log_dt=log_dt[0],
             b_re=b_re[0], b_im=b_im[0], c_re=c_re[0], c_im=c_im[0], d_skip=d_skip[0],
             w_glu=w_glu[0], b_glu=b_glu[0], q_a_norm=q_a_norm[0], w_q_b=w_q_b[0],
             kv_a_norm=kv_a_norm[0], w_kv_b=w_kv_b[0], out_norm_ssm=out_norm_ssm[0],
             out_norm_attn=out_norm_attn[0], w_out=w_out[0], ffn_norm=ffn_norm[0], w_up=w_up[0],
             conv_w=conv_w[0], conv_b=conv_b[0], w_down=w_down[0], final_norm=final_norm)
    w = _prepare(p)
    d_ssm = w["d_ssm"]
    cs = _rope_table(N_META + seq)
    x2d = x.reshape(bsz * seq, d)
    meta = meta_tokens.astype(x.dtype)

    u_m, q_m, k_m, v_m = _inproj(meta, cs[:N_META], w, 1, 1, N_META)
    n_state = 2 * STATE_W * w["bm"].shape[0]
    ya_m8, h_meta_state = _s5(jnp.broadcast_to(u_m[None], (SUBLANES, N_META, d_ssm)),
                              jnp.zeros((SUBLANES, n_state), F32), w, N_META)
    ya_m = ya_m8[0]
    yb_m = _attention_meta(q_m, k_m, v_m[0])
    h1_m = _outproj(ya_m, yb_m, meta, w, 1, 1, N_META)

    tm = 512
    nt = seq // tm
    u, q, k, v = _inproj(x2d, cs[N_META:], w, bsz, nt, tm)
    ya, _ = _s5(u.reshape(bsz, seq, d_ssm), h_meta_state, w, 64)
    yb, wd = _attention(q, k, v, k_m, v_m[0], w_down[0], bsz, seq)
    h1, wu = _outproj(ya.reshape(bsz * seq, d_ssm), yb, x2d, w, bsz, nt, tm, cast_src=w_up[0])
    meta_halo = _meta_gate(h1_m, wu, w)
    out = _ffn(h1, meta_halo, wu, wd, w, nt, tm)
    return out.reshape(bsz, seq, d)
```

```python
import functools
import math

import jax
import jax.numpy as jnp
from jax import lax
from jax.experimental import pallas as pl
from jax.experimental.pallas import tpu as pltpu

F32 = jnp.float32
BF16 = jnp.bfloat16

EPS = 1e-6
CHUNK = 64
N_META = 16
SSM_GROUP = 16
SSM_STATE = 64
HEADS = 8
QK_NOPE = 128
QK_ROPE = 64
V_HEAD = 128
ROPE_BASE = 10000.0

LANES = 128
SUBLANES = 8
MXU_DIM = 256
VMEM_BYTES = 64 * 1024 * 1024

HEAD_W = QK_NOPE + 2 * QK_ROPE
GROUPS_PER_MXU = MXU_DIM // SSM_GROUP
STATE_W = GROUPS_PER_MXU * SSM_STATE

ATT_TQ = 256
ATT_HEADS = 2
VT_ROWS = V_HEAD + SUBLANES
FFN_TF = 1024
IN_SUB = 2
FFN_SUB = 2
S5_SUB = 2
S5_SCAN_W = 512


_NT = (((1,), (1,)), ((), ()))


def _vmem_limit(nbytes):
    return int(min(nbytes + (4 << 20), VMEM_BYTES - (4 << 20)))


def _rms(x, g):
    x = x.astype(F32)
    return x * lax.rsqrt(jnp.mean(x * x, axis=-1, keepdims=True) + EPS) * g


def _const_spec(shape):
    nd = len(shape)
    return pl.BlockSpec(shape, lambda *_: (0,) * nd, pipeline_mode=pl.Buffered(1))


def _inproj_kernel(x_ref, g_ref, win_ref, qg_ref, wq_ref, kvg_ref, wk_ref, wv_ref, vone_ref,
                   cs_ref, u_ref, q_ref, k_ref, v_ref, *, scale):
    d_ssm = u_ref.shape[1]
    q_lora = wq_ref.shape[0]
    kv_lora = wk_ref.shape[0]
    o1, o2, o3 = d_ssm, d_ssm + q_lora, d_ssm + q_lora + kv_lora
    tm = x_ref.shape[0]
    n_sub = IN_SUB if tm % (IN_SUB * LANES) == 0 else 1
    ts = tm // n_sub
    zs = []
    for s in range(n_sub):
        xn = _rms(x_ref[s * ts:(s + 1) * ts, :], g_ref[...]).astype(BF16)
        zs.append(jnp.dot(xn, win_ref[...], preferred_element_type=F32))
    for s, z in enumerate(zs):
        rows = slice(s * ts, (s + 1) * ts)
        u_ref[rows, :] = z[:, :o1].astype(u_ref.dtype)

        cs = cs_ref[rows, :]
        qn = _rms(z[:, o1:o2], qg_ref[...]).astype(BF16)
        q = jnp.dot(qn, wq_ref[...], preferred_element_type=F32)
        cs_q = cs * scale
        for h in range(HEADS):
            c0 = h * HEAD_W
            q_ref[rows, c0:c0 + QK_NOPE] = (q[:, c0:c0 + QK_NOPE] * scale).astype(q_ref.dtype)
            q_ref[rows, c0 + QK_NOPE:c0 + HEAD_W] = (q[:, c0 + QK_NOPE:c0 + HEAD_W] * cs_q).astype(q_ref.dtype)

        kvn = _rms(z[:, o2:o3], kvg_ref[...]).astype(BF16)
        kn = jnp.dot(kvn, wk_ref[...], preferred_element_type=F32)
        vt = lax.dot_general(wv_ref[...], kvn, _NT, preferred_element_type=F32)
        v_ref[:, rows] = (vt + vone_ref[...]).astype(v_ref.dtype)
        kpe = z[:, o3:o3 + QK_ROPE]
        x1, x2 = kpe[:, :QK_ROPE // 2], kpe[:, QK_ROPE // 2:]
        t = jnp.concatenate([kpe, -x2, x1], axis=1) * cs
        krot = (t + pltpu.roll(t, QK_ROPE, 1)).astype(k_ref.dtype)
        for h in range(HEADS):
            c0 = h * HEAD_W
            k_ref[rows, c0:c0 + QK_NOPE] = kn[:, h * QK_NOPE:(h + 1) * QK_NOPE].astype(k_ref.dtype)
            k_ref[rows, c0 + QK_NOPE:c0 + HEAD_W] = krot


def _inproj(x2d, cs, w, nb, nt, tm):
    rows, d = x2d.shape
    d_ssm = w["d_ssm"]
    n_in = w["win"].shape[1]
    row_spec = lambda width: pl.BlockSpec((tm, width), lambda b, j: (b * nt + j, 0))
    in_specs = [
        row_spec(d),
        _const_spec((1, d)),
        _const_spec(w["win"].shape),
        _const_spec(w["qg"].shape),
        _const_spec(w["wq"].shape),
        _const_spec(w["kvg"].shape),
        _const_spec(w["wk"].shape),
        _const_spec(w["wv"].shape),
        _const_spec(w["vone"].shape),
        pl.BlockSpec((tm, LANES), lambda b, j: (j, 0)),
    ]
    out_shape = (
        jax.ShapeDtypeStruct((rows, d_ssm), BF16),
        jax.ShapeDtypeStruct((rows, HEADS * HEAD_W), BF16),
        jax.ShapeDtypeStruct((rows, HEADS * HEAD_W), BF16),
        jax.ShapeDtypeStruct((nb, HEADS * VT_ROWS, nt * tm), BF16),
    )
    out_specs = (
        row_spec(d_ssm),
        row_spec(HEADS * HEAD_W),
        row_spec(HEADS * HEAD_W),
        pl.BlockSpec((None, HEADS * VT_ROWS, tm), lambda b, j: (b, 0, j)),
    )
    weights = 2 * (w["win"].size + w["wq"].size + w["wk"].size + w["wv"].size)
    tiles = 2 * tm * (4 * d + 2 * d_ssm + 4 * HEADS * HEAD_W + 2 * HEADS * V_HEAD + 4 * LANES)
    temps = 4 * tm * (d + n_in)
    return pl.pallas_call(
        functools.partial(_inproj_kernel, scale=w["scale"]),
        grid=(nb, nt),
        in_specs=in_specs,
        out_specs=out_specs,
        out_shape=out_shape,
        compiler_params=pltpu.CompilerParams(
            dimension_semantics=("arbitrary", "arbitrary"),
            vmem_limit_bytes=_vmem_limit(weights + tiles + temps)),
        name="inproj",
    )(x2d, w["mix_g"], w["win"], w["qg"], w["wq"], w["kvg"], w["wk"], w["wv"], w["vone"], cs)


def _s5_kernel(u_ref, h0_ref, bm_ref, cm_ref, are_ref, aim_ref, dskip_ref, wglu_ref, bglu_ref,
               y_ref, hT_ref, hbuf, state, *, steps):
    n_gb = bm_ref.shape[0]
    sw = are_ref.shape[2]
    cin = bm_ref.shape[1]
    nb, _, d_ssm = u_ref.shape

    @pl.when(pl.program_id(0) == 0)
    def _():
        state[...] = h0_ref[...]

    u = jnp.swapaxes(u_ref[...].astype(F32), 0, 1).reshape(nb * steps, d_ssm).astype(BF16)

    sub = steps // S5_SUB
    span = lambda k: slice(k * sub * SUBLANES, (k + 1) * sub * SUBLANES)
    for k in range(S5_SUB):
        for gb in range(n_gb):
            hbuf[span(k), 2 * sw * gb:2 * sw * (gb + 1)] = jnp.dot(
                u[span(k), cin * gb:cin * (gb + 1)], bm_ref[gb], preferred_element_type=F32)

    blocks = [(2 * sw * gb + o, 2 * sw * gb + sw + o, gb, o)
              for gb in range(n_gb) for o in range(0, sw, S5_SCAN_W)]
    carry = [(state[:, c_re:c_re + S5_SCAN_W], state[:, c_im:c_im + S5_SCAN_W])
             for c_re, c_im, _, _ in blocks]
    for k in range(S5_SUB):
        for j, (c_re, c_im, gb, o) in enumerate(blocks):
            are = are_ref[gb, :, o:o + S5_SCAN_W]
            aim = aim_ref[gb, :, o:o + S5_SCAN_W]
            hre, him = carry[j]
            for t in range(k * sub, (k + 1) * sub):
                r0 = t * SUBLANES
                nre = are * hre - aim * him + hbuf[r0:r0 + SUBLANES, c_re:c_re + S5_SCAN_W]
                nim = are * him + aim * hre + hbuf[r0:r0 + SUBLANES, c_im:c_im + S5_SCAN_W]
                hbuf[r0:r0 + SUBLANES, c_re:c_re + S5_SCAN_W] = nre
                hbuf[r0:r0 + SUBLANES, c_im:c_im + S5_SCAN_W] = nim
                hre, him = nre, nim
            carry[j] = (hre, him)
    for (c_re, c_im, _, _), (hre, him) in zip(blocks, carry):
        state[:, c_re:c_re + S5_SCAN_W] = hre
        state[:, c_im:c_im + S5_SCAN_W] = him

    outs = []
    for k in range(S5_SUB):
        ys = [jnp.dot(hbuf[span(k), 2 * sw * gb:2 * sw * (gb + 1)].astype(BF16), cm_ref[gb],
                      preferred_element_type=F32) for gb in range(n_gb)]
        y = jnp.concatenate(ys, axis=1) + dskip_ref[...] * u[span(k)].astype(F32)
        g = jax.nn.gelu(y)
        gate = jnp.dot(g.astype(BF16), wglu_ref[...], preferred_element_type=F32) + bglu_ref[...]
        outs.append(g * jax.nn.sigmoid(gate))
    out = jnp.concatenate(outs, axis=0).reshape(steps, nb, d_ssm)
    y_ref[...] = jnp.swapaxes(out, 0, 1).astype(y_ref.dtype)
    hT_ref[...] = state[...]


def _s5(u, h0, w, steps):
    nb, t_len, d_ssm = u.shape
    assert nb == SUBLANES
    r = steps * SUBLANES
    n_state = h0.shape[1]
    blk = pl.BlockSpec((nb, steps, d_ssm), lambda i: (0, i, 0))
    in_specs = [
        blk,
        _const_spec(h0.shape),
        _const_spec(w["bm"].shape),
        _const_spec(w["cm"].shape),
        _const_spec(w["are"].shape),
        _const_spec(w["aim"].shape),
        _const_spec((1, d_ssm)),
        _const_spec(w["wglu"].shape),
        _const_spec((1, d_ssm)),
    ]
    out_shape = (jax.ShapeDtypeStruct(u.shape, BF16),
                 jax.ShapeDtypeStruct(h0.shape, F32))
    out_specs = (blk, pl.BlockSpec(h0.shape, lambda i: (0, 0)))
    weights = 2 * (w["bm"].size + w["cm"].size + w["wglu"].size) + 4 * (w["are"].size + w["aim"].size)
    tiles = 2 * r * d_ssm * 2 * 2 + 3 * 4 * h0.size
    scratch = 4 * r * n_state + 4 * h0.size
    temps = 2 * 4 * r * d_ssm
    return pl.pallas_call(
        functools.partial(_s5_kernel, steps=steps),
        grid=(t_len // steps,),
        in_specs=in_specs,
        out_specs=out_specs,
        out_shape=out_shape,
        scratch_shapes=[pltpu.VMEM((r, n_state), F32), pltpu.VMEM(h0.shape, F32)],
        compiler_params=pltpu.CompilerParams(
            dimension_semantics=("arbitrary",),
            vmem_limit_bytes=_vmem_limit(weights + tiles + scratch + temps)),
        name="s5",
    )(u, h0, w["bm"], w["cm"], w["are"], w["aim"], w["dskip"], w["wglu"], w["bglu"])


def _attn_kernel(q_ref, k_ref, vt_ref, km_ref, vmt_ref, src_ref, o_ref, dst_ref):
    @pl.when(pl.program_id(1) == 0)
    def _():
        dst_ref[...] = src_ref[...].astype(dst_ref.dtype)

    seq = q_ref.shape[0]
    half = ATT_TQ // 2
    neg = jnp.finfo(F32).min
    kc = lax.broadcasted_iota(jnp.int32, (half, half), 0) // CHUNK
    qc = lax.broadcasted_iota(jnp.int32, (half, half), 1) // CHUNK
    visible = kc <= qc
    colmax = lambda t: jnp.max(t, axis=0, keepdims=True)
    prob = lambda s, m: jnp.exp2(s - m).astype(BF16)
    order = [(hh, i) for i in range(seq // ATT_TQ - 1, -1, -1) for hh in range(ATT_HEADS)]
    scores = {}
    for hh, i in order:
        r0 = i * ATT_TQ
        qs = slice(hh * HEAD_W, (hh + 1) * HEAD_W)
        q = q_ref[r0:r0 + ATT_TQ, qs]
        s_meta = lax.dot_general(km_ref[:, qs], q, _NT, preferred_element_type=F32)
        s_diag = lax.dot_general(k_ref[r0:r0 + ATT_TQ, qs], q, _NT, preferred_element_type=F32)
        s_full = (lax.dot_general(k_ref[0:r0, qs], q, _NT, preferred_element_type=F32)
                  if i > 0 else None)
        scores[hh, i] = (s_meta, s_diag, s_full)
    for hh, i in order:
        r0 = i * ATT_TQ
        vs = slice(hh * VT_ROWS, (hh + 1) * VT_ROWS)
        vmt = vmt_ref[vs, :]
        s_meta, s_diag, s_full = scores[hh, i]
        s00 = jnp.where(visible, s_diag[:half, :half], neg)
        s01 = s_diag[:half, half:]
        s11 = jnp.where(visible, s_diag[half:, half:], neg)
        m = jnp.concatenate([colmax(s00), jnp.maximum(colmax(s01), colmax(s11))], axis=1)
        m = jnp.maximum(m, colmax(s_meta))
        if i > 0:
            m = jnp.maximum(m, colmax(s_full))
        p_meta = prob(s_meta, m)
        p00 = prob(s00, m[:, :half])
        p01 = prob(s01, m[:, half:])
        p11 = prob(s11, m[:, half:])
        p_diag = jnp.concatenate([jnp.concatenate([p00, p01], axis=1),
                                  jnp.concatenate([jnp.zeros_like(p11), p11], axis=1)], axis=0)
        o = jnp.dot(vmt, p_meta, preferred_element_type=F32)
        o = o + jnp.dot(vt_ref[vs, r0:r0 + ATT_TQ], p_diag, preferred_element_type=F32)
        if i > 0:
            o = o + jnp.dot(vt_ref[vs, 0:r0], prob(s_full, m), preferred_element_type=F32)
        out = o[:V_HEAD] / o[V_HEAD:V_HEAD + 1]
        o_ref[r0:r0 + ATT_TQ, hh * V_HEAD:(hh + 1) * V_HEAD] = out.T.astype(o_ref.dtype)


def _attention(q, k, vt, km, vmt, cast_src, nb, seq):
    assert ATT_TQ % CHUNK == 0 and seq % ATT_TQ == 0 and V_HEAD == LANES and HEADS % ATT_HEADS == 0
    n_meta = km.shape[0]
    hw, vr = ATT_HEADS * HEAD_W, ATT_HEADS * VT_ROWS
    src_rows, src_cols = cast_src.shape
    blk = src_rows // nb
    assert blk * nb == src_rows and blk % (2 * SUBLANES) == 0
    cast_spec = pl.BlockSpec((blk, src_cols), lambda b, h: (b, 0))
    in_specs = [
        pl.BlockSpec((seq, hw), lambda b, h: (b, h)),
        pl.BlockSpec((seq, hw), lambda b, h: (b, h)),
        pl.BlockSpec((None, vr, seq), lambda b, h: (b, h, 0)),
        pl.BlockSpec((n_meta, hw), lambda b, h: (0, h)),
        pl.BlockSpec((vr, n_meta), lambda b, h: (h, 0)),
        cast_spec,
    ]
    tiles = 2 * 2 * seq * (2 * hw + vr + ATT_HEADS * V_HEAD) + 2 * blk * src_cols * (4 + 2)
    temps = 4 * 4 * seq * ATT_TQ * ATT_HEADS
    return pl.pallas_call(
        _attn_kernel,
        grid=(nb, HEADS // ATT_HEADS),
        in_specs=in_specs,
        out_specs=(pl.BlockSpec((seq, ATT_HEADS * V_HEAD), lambda b, h: (b, h)), cast_spec),
        out_shape=(jax.ShapeDtypeStruct((nb * seq, HEADS * V_HEAD), BF16),
                   jax.ShapeDtypeStruct(cast_src.shape, BF16)),
        compiler_params=pltpu.CompilerParams(
            dimension_semantics=("arbitrary", "arbitrary"),
            vmem_limit_bytes=_vmem_limit(tiles + temps)),
        name="attn",
    )(q, k, vt, km, vmt, cast_src)


def _attn_meta_kernel(q_ref, k_ref, vt_ref, o_ref):
    for h in range(HEADS):
        s = lax.dot_general(q_ref[:, h * HEAD_W:(h + 1) * HEAD_W],
                            k_ref[:, h * HEAD_W:(h + 1) * HEAD_W], _NT, preferred_element_type=F32)
        p = jnp.exp2(s - jnp.max(s, axis=1, keepdims=True))
        pv = lax.dot_general(p.astype(BF16), vt_ref[h * VT_ROWS:h * VT_ROWS + V_HEAD, :], _NT,
                             preferred_element_type=F32)
        o_ref[:, h * V_HEAD:(h + 1) * V_HEAD] = (pv / jnp.sum(p, axis=1, keepdims=True)).astype(o_ref.dtype)


def _attention_meta(q, k, vt):
    return pl.pallas_call(
        _attn_meta_kernel,
        out_shape=jax.ShapeDtypeStruct((q.shape[0], HEADS * V_HEAD), BF16),
        name="attn_meta",
    )(q, k, vt)


def _outproj_kernel(ya_ref, yb_ref, x_ref, ga_ref, gb_ref, wa_ref, wb_ref, o_ref):
    tm = x_ref.shape[0]
    n_sub = IN_SUB if tm % (IN_SUB * LANES) == 0 else 1
    ts = tm // n_sub
    for s in range(n_sub):
        rows = slice(s * ts, (s + 1) * ts)
        a = _rms(ya_ref[rows, :], ga_ref[...]).astype(BF16)
        b = _rms(yb_ref[rows, :], gb_ref[...]).astype(BF16)
        mix = jnp.dot(a, wa_ref[...], preferred_element_type=F32)
        mix = mix + jnp.dot(b, wb_ref[...], preferred_element_type=F32)
        o_ref[rows, :] = x_ref[rows, :].astype(F32) + mix


def _outproj_cast_kernel(ya_ref, yb_ref, x_ref, ga_ref, gb_ref, wa_ref, wb_ref, src_ref, o_ref, dst_ref):
    _outproj_kernel(ya_ref, yb_ref, x_ref, ga_ref, gb_ref, wa_ref, wb_ref, o_ref)
    dst_ref[...] = src_ref[...].astype(dst_ref.dtype)


def _outproj(ya, yb, x2d, w, nb, nt, tm, cast_src=None):
    rows, d = x2d.shape
    d_ssm, d_attn = w["wo_a"].shape[0], w["wo_b"].shape[0]
    row_spec = lambda width: pl.BlockSpec((tm, width), lambda b, j: (b * nt + j, 0))
    in_specs = [
        row_spec(d_ssm),
        row_spec(d_attn),
        row_spec(d),
        _const_spec((1, d_ssm)),
        _const_spec((1, d_attn)),
        _const_spec(w["wo_a"].shape),
        _const_spec(w["wo_b"].shape),
    ]
    args = [ya, yb, x2d, w["ga"], w["gb"], w["wo_a"], w["wo_b"]]
    out_specs = row_spec(d)
    out_shape = jax.ShapeDtypeStruct((rows, d), F32)
    weights = 2 * (w["wo_a"].size + w["wo_b"].size)
    tiles = 2 * tm * (2 * d_ssm + 2 * d_attn + 4 * d + 4 * d)
    temps = 4 * tm * d
    body = _outproj_kernel
    if cast_src is not None:
        src_rows, src_cols = cast_src.shape
        blk = src_rows // (nb * nt)
        assert blk * nb * nt == src_rows and blk % (2 * SUBLANES) == 0
        cast_spec = pl.BlockSpec((blk, src_cols), lambda b, j: (b * nt + j, 0))
        in_specs.append(cast_spec)
        args.append(cast_src)
        out_specs = (out_specs, cast_spec)
        out_shape = (out_shape, jax.ShapeDtypeStruct(cast_src.shape, BF16))
        tiles += 2 * blk * src_cols * (4 + 2)
        body = _outproj_cast_kernel
    return pl.pallas_call(
        body,
        grid=(nb, nt),
        in_specs=in_specs,
        out_specs=out_specs,
        out_shape=out_shape,
        compiler_params=pltpu.CompilerParams(
            dimension_semantics=("arbitrary", "arbitrary"),
            vmem_limit_bytes=_vmem_limit(weights + tiles + temps)),
        name="outproj",
    )(*args)


def _gate_kernel(h_ref, g_ref, wg_ref, o_ref):
    xn = _rms(h_ref[...], g_ref[...]).astype(BF16)
    o_ref[...] = jnp.dot(xn, wg_ref[...], preferred_element_type=F32)


def _meta_gate(h_meta, wu, w):
    rows, d = h_meta.shape
    n_chunks = w["conv_w"].shape[0]
    gate = pl.pallas_call(
        _gate_kernel,
        grid=(n_chunks,),
        in_specs=[_const_spec((rows, d)), _const_spec((1, d)),
                  pl.BlockSpec((d, FFN_TF), lambda j: (0, j))],
        out_specs=pl.BlockSpec((rows, FFN_TF), lambda j: (0, j)),
        out_shape=jax.ShapeDtypeStruct((rows, n_chunks * FFN_TF), F32),
        compiler_params=pltpu.CompilerParams(dimension_semantics=("arbitrary",)),
        name="meta_gate",
    )(h_meta, w["ffn_g"], wu)
    halo = gate[rows - SUBLANES:].reshape(SUBLANES, n_chunks, FFN_TF)
    return jnp.swapaxes(halo, 0, 1)


def _shift_rows(x, prev, k):
    rolled = pltpu.roll(x, k, 0)
    head = jnp.concatenate([prev, x[:SUBLANES]], axis=0)[SUBLANES - k:2 * SUBLANES - k]
    return jnp.concatenate([head, rolled[SUBLANES:]], axis=0)


def _ffn_kernel(h_ref, g_ref, wu_hbm, wd_hbm, cw_ref, cb_ref, mh_ref, fg_ref, o_ref,
                xn_scr, halo_scr, wg_buf, wv_buf, wd_buf, sem, *, tiles_per_batch, d_ff):
    i = pl.program_id(0)
    tm = h_ref.shape[0]
    ts = tm // FFN_SUB
    n_full = d_ff // FFN_TF
    rem = d_ff - n_full * FFN_TF
    assert 0 < rem and rem % LANES == 0

    def chunk_copies(col0, width, slot):
        return (
            pltpu.make_async_copy(wu_hbm.at[:, pl.ds(col0, width)],
                                  wg_buf.at[slot, :, pl.ds(0, width)], sem.at[slot, 0]),
            pltpu.make_async_copy(wu_hbm.at[:, pl.ds(d_ff + col0, width)],
                                  wv_buf.at[slot, :, pl.ds(0, width)], sem.at[slot, 1]),
            pltpu.make_async_copy(wd_hbm.at[pl.ds(col0, width), :],
                                  wd_buf.at[slot, pl.ds(0, width), :], sem.at[slot, 2]),
        )

    def start(copies):
        for c in copies:
            c.start()

    def wait(copies):
        for c in copies:
            c.wait()

    ragged = lambda slot: chunk_copies(n_full * FFN_TF, rem, slot)
    full = lambda c, slot: chunk_copies(pl.multiple_of(c * FFN_TF, FFN_TF), FFN_TF, slot)

    def compute(idx, width, slot):
        wg = wg_buf[slot, :, :width]
        wv = wv_buf[slot, :, :width]
        wd = wd_buf[slot, :width, :]
        cw = cw_ref[idx][:, :width]
        cb = cb_ref[idx][:, :width]
        prev = halo_scr[idx][:, :width]
        for s in range(FFN_SUB):
            rows = pl.ds(s * ts, ts)
            xn = xn_scr[rows, :]
            gate = jnp.dot(xn, wg, preferred_element_type=F32)
            val = jnp.dot(xn, wv, preferred_element_type=F32)
            conv = (cw[0:1] * _shift_rows(gate, prev, 2)
                    + cw[1:2] * _shift_rows(gate, prev, 1)
                    + cw[2:3] * gate + cb)
            prev = gate[ts - SUBLANES:, :]
            act = (jax.nn.silu(conv) * val).astype(BF16)
            o_ref[rows, :] += jnp.dot(act, wd, preferred_element_type=F32)
        halo_scr[idx, :, pl.ds(0, width)] = prev

    assert n_full >= 2
    slot0 = (i * (n_full + 1)) % 2

    @pl.when(i == 0)
    def _():
        start(full(0, slot0))

    @pl.when((i % tiles_per_batch) == 0)
    def _():
        halo_scr[...] = mh_ref[...]

    h = h_ref[...]
    xn_scr[...] = _rms(h, g_ref[...]).astype(BF16)
    o_ref[...] = h

    def body(c, carry):
        slot = (slot0 + c) % 2
        wait(full(c, slot))

        @pl.when(c + 1 < n_full)
        def _():
            start(full(c + 1, 1 - slot))

        @pl.when(c + 1 == n_full)
        def _():
            start(ragged(1 - slot))

        compute(c, FFN_TF, slot)
        return carry

    lax.fori_loop(0, n_full, body, 0)

    slot_r = (slot0 + n_full) % 2
    wait(ragged(slot_r))

    @pl.when(i + 1 < pl.num_programs(0))
    def _():
        start(full(0, 1 - slot_r))

    compute(n_full, rem, slot_r)
    o_ref[...] = _rms(o_ref[...], fg_ref[...])


def _ffn(h1, meta_halo, wu, wd, w, tiles_per_batch, tm):
    rows, d = h1.shape
    d_ff = wd.shape[0]
    n_chunks = w["conv_w"].shape[0]
    in_specs = [
        pl.BlockSpec((tm, d), lambda i: (i, 0)),
        _const_spec((1, d)),
        pl.BlockSpec(memory_space=pl.ANY),
        pl.BlockSpec(memory_space=pl.ANY),
        _const_spec(w["conv_w"].shape),
        _const_spec(w["conv_b"].shape),
        _const_spec(meta_halo.shape),
        _const_spec((1, d)),
    ]
    slots = 2
    tiles = 2 * 2 * 4 * tm * d + 4 * (w["conv_w"].size + w["conv_b"].size + meta_halo.size)
    scratch = 2 * tm * d + 4 * meta_halo.size + slots * 3 * 2 * d * FFN_TF
    temps = 4 * (tm // FFN_SUB) * FFN_TF * 16
    return pl.pallas_call(
        functools.partial(_ffn_kernel, tiles_per_batch=tiles_per_batch, d_ff=d_ff),
        grid=(rows // tm,),
        in_specs=in_specs,
        out_specs=pl.BlockSpec((tm, d), lambda i: (i, 0)),
        out_shape=jax.ShapeDtypeStruct((rows, d), F32),
        scratch_shapes=[
            pltpu.VMEM((tm, d), BF16),
            pltpu.VMEM((n_chunks, SUBLANES, FFN_TF), F32),
            pltpu.VMEM((slots, d, FFN_TF), BF16),
            pltpu.VMEM((slots, d, FFN_TF), BF16),
            pltpu.VMEM((slots, FFN_TF, d), BF16),
            pltpu.SemaphoreType.DMA((slots, 3)),
        ],
        compiler_params=pltpu.CompilerParams(
            dimension_semantics=("arbitrary",),
            vmem_limit_bytes=_vmem_limit(tiles + scratch + temps)),
        name="ffn",
    )(h1, w["ffn_g"], wu, wd, w["conv_w"], w["conv_b"], meta_halo, w["final_g"])


def _swap_halves(w):
    half = w.shape[-1] // 2
    return jnp.concatenate([-w[..., half:], w[..., :half]], axis=-1)


def _block_diag(blocks, n):
    rows, c = blocks.shape[-2:]
    r = rows // n
    rep = (jnp.arange(c)[:, None] == jnp.arange(n * c)[None, :] % c).astype(blocks.dtype)
    tiled = jnp.einsum("...rc,cq->...rq", blocks, rep, precision=lax.Precision.HIGHEST)
    keep = (jnp.arange(rows)[:, None] // r) == (jnp.arange(n * c)[None, :] // c)
    return jnp.where(keep, tiled, 0.0)


def _prepare(p):
    d_ssm = p["d_skip"].shape[-1]
    q_lora = p["q_a_norm"].shape[-1]
    kv_lora = p["kv_a_norm"].shape[-1]
    d_ff = p["w_down"].shape[0]
    w = {"d_ssm": d_ssm, "scale": math.log2(math.e) / math.sqrt(QK_NOPE + QK_ROPE)}

    row = lambda v: v.reshape(1, -1).astype(F32)
    w["mix_g"] = row(p["mix_norm"])
    w["win"] = p["w_in"].astype(BF16)
    w["qg"] = row(p["q_a_norm"])
    w["kvg"] = row(p["kv_a_norm"])
    wq = p["w_q_b"].reshape(q_lora, HEADS, QK_NOPE + QK_ROPE)
    wq_pe = wq[..., QK_NOPE:]
    w["wq"] = jnp.concatenate([wq, _swap_halves(wq_pe)], axis=-1).reshape(q_lora, HEADS * HEAD_W).astype(BF16)
    wkv = p["w_kv_b"].reshape(kv_lora, HEADS, QK_NOPE + V_HEAD)
    w["wk"] = wkv[..., :QK_NOPE].reshape(kv_lora, HEADS * QK_NOPE).astype(BF16)
    wv = jnp.pad(jnp.moveaxis(wkv[..., QK_NOPE:], 0, 2), ((0, 0), (0, VT_ROWS - V_HEAD), (0, 0)))
    w["wv"] = wv.reshape(HEADS * VT_ROWS, kv_lora).astype(BF16)
    w["vone"] = jnp.tile((jnp.arange(VT_ROWS) == V_HEAD).astype(F32), HEADS).reshape(-1, 1)

    lam_re, lam_im = p["lam_re"].astype(F32), p["lam_im"].astype(F32)
    dt = jnp.exp(p["log_dt"].astype(F32))[:, None]
    mag = jnp.exp(lam_re * dt)
    a_re, a_im = mag * jnp.cos(lam_im * dt), mag * jnp.sin(lam_im * dt)
    den = lam_re * lam_re + lam_im * lam_im
    f_re = ((a_re - 1.0) * lam_re + a_im * lam_im) / den
    f_im = (a_im * lam_re - (a_re - 1.0) * lam_im) / den
    b_re, b_im = p["b_re"].astype(F32), p["b_im"].astype(F32)
    bb_re = f_re[..., None] * b_re - f_im[..., None] * b_im
    bb_im = f_re[..., None] * b_im + f_im[..., None] * b_re
    n_g = lam_re.shape[0]
    n_gb = n_g // GROUPS_PER_MXU
    bd = lambda t: _block_diag(t.reshape(n_gb, GROUPS_PER_MXU * t.shape[1], t.shape[2]), GROUPS_PER_MXU)
    w["bm"] = jnp.concatenate([bd(jnp.swapaxes(bb_re, 1, 2)),
                               bd(jnp.swapaxes(bb_im, 1, 2))], axis=2).astype(BF16)
    c_re, c_im = p["c_re"].astype(F32), p["c_im"].astype(F32)
    w["cm"] = jnp.concatenate([bd(jnp.swapaxes(c_re, 1, 2)),
                               bd(jnp.swapaxes(-c_im, 1, 2))], axis=1).astype(BF16)
    bcast = lambda a: jnp.broadcast_to(a.reshape(n_gb, 1, STATE_W), (n_gb, SUBLANES, STATE_W))
    w["are"], w["aim"] = bcast(a_re), bcast(a_im)
    w["dskip"] = row(p["d_skip"])
    w["wglu"] = p["w_glu"].astype(BF16)
    w["bglu"] = row(p["b_glu"])

    w["ga"], w["gb"] = row(p["out_norm_ssm"]), row(p["out_norm_attn"])
    w["wo_a"] = p["w_out"][:d_ssm].astype(BF16)
    w["wo_b"] = p["w_out"][d_ssm:].astype(BF16)

    n_chunks = -(-d_ff // FFN_TF)
    pad = n_chunks * FFN_TF - d_ff
    per_chunk = lambda a: jnp.swapaxes(
        jnp.pad(a.astype(F32), ((0, 0), (0, pad))).reshape(a.shape[0], n_chunks, FFN_TF), 0, 1)
    w["conv_w"] = per_chunk(p["conv_w"])
    w["conv_b"] = per_chunk(p["conv_b"].reshape(1, -1))
    w["ffn_g"] = row(p["ffn_norm"])
    w["final_g"] = row(p["final_norm"])
    return w


def _rope_table(n_pos):
    pos = jnp.arange(n_pos, dtype=F32)
    inv_freq = 1.0 / (ROPE_BASE ** (jnp.arange(0, QK_ROPE, 2, dtype=F32) / QK_ROPE))
    ang = pos[:, None] * inv_freq[None, :]
    cos, sin = jnp.cos(ang), jnp.sin(ang)
    return jnp.concatenate([cos, cos, sin, sin], axis=1)


def kernel(x, meta_tokens, mix_norm, w_in, lam_re, lam_im, log_dt, b_re, b_im, c_re, c_im, d_skip, w_glu, b_glu, q_a_norm, w_q_b, kv_a_norm, w_kv_b, out_norm_ssm, out_norm_attn, w_out, ffn_norm, w_up, conv_w, conv_b, w_down, final_norm):
    bsz, seq, d = x.shape
    assert meta_tokens.shape == (N_META, d) and bsz == SUBLANES
    p = dict(mix_norm=mix_norm[0], w_in=w_in[0], lam_re=lam_re[0], lam_im=lam_im[0], log_dt=log_dt[0],
             b_re=b_re[0], b_im=b_im[0], c_re=c_re[0], c_im=c_im[0], d_skip=d_skip[0],
             w_glu=w_glu[0], b_glu=b_glu[0], q_a_norm=q_a_norm[0], w_q_b=w_q_b[0],
             kv_a_norm=kv_a_norm[0], w_kv_b=w_kv_b[0], out_norm_ssm=out_norm_ssm[0],
             out_norm_attn=out_norm_attn[0], w_out=w_out[0], ffn_norm=ffn_norm[0], w_up=w_up[0],
             conv_w=conv_w[0], conv_b=conv_b[0], w_down=w_down[0], final_norm=final_norm)
    w = _prepare(p)
    d_ssm = w["d_ssm"]
    cs = _rope_table(N_META + seq)
    x2d = x.reshape(bsz * seq, d)
    meta = meta_tokens.astype(x.dtype)

    u_m, q_m, k_m, v_m = _inproj(meta, cs[:N_META], w, 1, 1, N_META)
    n_state = 2 * STATE_W * w["bm"].shape[0]
    ya_m8, h_meta_state = _s5(jnp.broadcast_to(u_m[None], (SUBLANES, N_META, d_ssm)),
                              jnp.zeros((SUBLANES, n_state), F32), w, N_META)
    ya_m = ya_m8[0]
    yb_m = _attention_meta(q_m, k_m, v_m[0])
    h1_m = _outproj(ya_m, yb_m, meta, w, 1, 1, N_META)

    tm = 512
    nt = seq // tm
    u, q, k, v = _inproj(x2d, cs[N_META:], w, bsz, nt, tm)
    ya, _ = _s5(u.reshape(bsz, seq, d_ssm), h_meta_state, w, 64)
    yb, wd = _attention(q, k, v, k_m, v_m[0], w_down[0], bsz, seq)
    h1, wu = _outproj(ya.reshape(bsz * seq, d_ssm), yb, x2d, w, bsz, nt, tm, cast_src=w_up[0])
    meta_halo = _meta_gate(h1_m, wu, w)
    out = _ffn(h1, meta_halo, wu, wd, w, nt, tm)
    return out.reshape(bsz, seq, d)
```

```python
import functools
import math

import jax
import jax.numpy as jnp
from jax import lax
from jax.experimental import pallas as pl
from jax.experimental.pallas import tpu as pltpu

F32 = jnp.float32
BF16 = jnp.bfloat16

EPS = 1e-6
CHUNK = 64
N_META = 16
SSM_GROUP = 16
SSM_STATE = 64
HEADS = 8
QK_NOPE = 128
QK_ROPE = 64
V_HEAD = 128
ROPE_BASE = 10000.0

LANES = 128
SUBLANES = 8
MXU_DIM = 256
VMEM_BYTES = 64 * 1024 * 1024

HEAD_W = QK_NOPE + 2 * QK_ROPE
GROUPS_PER_MXU = MXU_DIM // SSM_GROUP
STATE_W = GROUPS_PER_MXU * SSM_STATE

ATT_TQ = 256
ATT_HEADS = 2
VT_ROWS = V_HEAD + SUBLANES
FFN_TF = 1024
IN_SUB = 2
FFN_SUB = 2
S5_SUB = 2
S5_SCAN_W = 512


_NT = (((1,), (1,)), ((), ()))


def _vmem_limit(nbytes):
    return int(min(nbytes + (4 << 20), VMEM_BYTES - (4 << 20)))


def _rms(x, g):
    x = x.astype(F32)
    return x * lax.rsqrt(jnp.mean(x * x, axis=-1, keepdims=True) + EPS) * g


def _const_spec(shape):
    nd = len(shape)
    return pl.BlockSpec(shape, lambda *_: (0,) * nd, pipeline_mode=pl.Buffered(1))


def _inproj_kernel(x_ref, g_ref, win_ref, qg_ref, wq_ref, kvg_ref, wk_ref, wv_ref, vone_ref,
                   cs_ref, u_ref, q_ref, k_ref, v_ref, *, scale):
    d_ssm = u_ref.shape[1]
    q_lora = wq_ref.shape[0]
    kv_lora = wk_ref.shape[0]
    o1, o2, o3 = d_ssm, d_ssm + q_lora, d_ssm + q_lora + kv_lora
    tm = x_ref.shape[0]
    n_sub = IN_SUB if tm % (IN_SUB * LANES) == 0 else 1
    ts = tm // n_sub
    zs = []
    for s in range(n_sub):
        xn = _rms(x_ref[s * ts:(s + 1) * ts, :], g_ref[...]).astype(BF16)
        zs.append(jnp.dot(xn, win_ref[...], preferred_element_type=F32))
    for s, z in enumerate(zs):
        rows = slice(s * ts, (s + 1) * ts)
        u_ref[rows, :] = z[:, :o1].astype(u_ref.dtype)

        cs = cs_ref[rows, :]
        qn = _rms(z[:, o1:o2], qg_ref[...]).astype(BF16)
        q = jnp.dot(qn, wq_ref[...], preferred_element_type=F32)
        cs_q = cs * scale
        for h in range(HEADS):
            c0 = h * HEAD_W
            q_ref[rows, c0:c0 + QK_NOPE] = (q[:, c0:c0 + QK_NOPE] * scale).astype(q_ref.dtype)
            q_ref[rows, c0 + QK_NOPE:c0 + HEAD_W] = (q[:, c0 + QK_NOPE:c0 + HEAD_W] * cs_q).astype(q_ref.dtype)

        kvn = _rms(z[:, o2:o3], kvg_ref[...]).astype(BF16)
        kn = jnp.dot(kvn, wk_ref[...], preferred_element_type=F32)
        vt = lax.dot_general(wv_ref[...], kvn, _NT, preferred_element_type=F32)
        v_ref[:, rows] = (vt + vone_ref[...]).astype(v_ref.dtype)
        kpe = z[:, o3:o3 + QK_ROPE]
        x1, x2 = kpe[:, :QK_ROPE // 2], kpe[:, QK_ROPE // 2:]
        t = jnp.concatenate([kpe, -x2, x1], axis=1) * cs
        krot = (t + pltpu.roll(t, QK_ROPE, 1)).astype(k_ref.dtype)
        for h in range(HEADS):
            c0 = h * HEAD_W
            k_ref[rows, c0:c0 + QK_NOPE] = kn[:, h * QK_NOPE:(h + 1) * QK_NOPE].astype(k_ref.dtype)
            k_ref[rows, c0 + QK_NOPE:c0 + HEAD_W] = krot


def _inproj(x2d, cs, w, nb, nt, tm):
    rows, d = x2d.shape
    d_ssm = w["d_ssm"]
    n_in = w["win"].shape[1]
    row_spec = lambda width: pl.BlockSpec((tm, width), lambda b, j: (b * nt + j, 0))
    in_specs = [
        row_spec(d),
        _const_spec((1, d)),
        _const_spec(w["win"].shape),
        _const_spec(w["qg"].shape),
        _const_spec(w["wq"].shape),
        _const_spec(w["kvg"].shape),
        _const_spec(w["wk"].shape),
        _const_spec(w["wv"].shape),
        _const_spec(w["vone"].shape),
        pl.BlockSpec((tm, LANES), lambda b, j: (j, 0)),
    ]
    out_shape = (
        jax.ShapeDtypeStruct((rows, d_ssm), BF16),
        jax.ShapeDtypeStruct((rows, HEADS * HEAD_W), BF16),
        jax.ShapeDtypeStruct((rows, HEADS * HEAD_W), BF16),
        jax.ShapeDtypeStruct((nb, HEADS * VT_ROWS, nt * tm), BF16),
    )
    out_specs = (
        row_spec(d_ssm),
        row_spec(HEADS * HEAD_W),
        row_spec(HEADS * HEAD_W),
        pl.BlockSpec((None, HEADS * VT_ROWS, tm), lambda b, j: (b, 0, j)),
    )
    weights = 2 * (w["win"].size + w["wq"].size + w["wk"].size + w["wv"].size)
    tiles = 2 * tm * (4 * d + 2 * d_ssm + 4 * HEADS * HEAD_W + 2 * HEADS * V_HEAD + 4 * LANES)
    temps = 4 * tm * (d + n_in)
    return pl.pallas_call(
        functools.partial(_inproj_kernel, scale=w["scale"]),
        grid=(nb, nt),
        in_specs=in_specs,
        out_specs=out_specs,
        out_shape=out_shape,
        compiler_params=pltpu.CompilerParams(
            dimension_semantics=("arbitrary", "arbitrary"),
            vmem_limit_bytes=_vmem_limit(weights + tiles + temps)),
        name="inproj",
    )(x2d, w["mix_g"], w["win"], w["qg"], w["wq"], w["kvg"], w["wk"], w["wv"], w["vone"], cs)


def _s5_kernel(u_ref, h0_ref, bm_ref, cm_ref, are_ref, aim_ref, dskip_ref, wglu_ref, bglu_ref,
               y_ref, hT_ref, hbuf, state, *, steps):
    n_gb = bm_ref.shape[0]
    sw = are_ref.shape[2]
    cin = bm_ref.shape[1]
    nb, _, d_ssm = u_ref.shape

    @pl.when(pl.program_id(0) == 0)
    def _():
        state[...] = h0_ref[...]

    u = jnp.swapaxes(u_ref[...].astype(F32), 0, 1).reshape(nb * steps, d_ssm).astype(BF16)

    sub = steps // S5_SUB
    span = lambda k: slice(k * sub * SUBLANES, (k + 1) * sub * SUBLANES)
    for k in range(S5_SUB):
        for gb in range(n_gb):
            hbuf[span(k), 2 * sw * gb:2 * sw * (gb + 1)] = jnp.dot(
                u[span(k), cin * gb:cin * (gb + 1)], bm_ref[gb], preferred_element_type=F32)

    blocks = [(2 * sw * gb + o, 2 * sw * gb + sw + o, gb, o)
              for gb in range(n_gb) for o in range(0, sw, S5_SCAN_W)]
    carry = [(state[:, c_re:c_re + S5_SCAN_W], state[:, c_im:c_im + S5_SCAN_W])
             for c_re, c_im, _, _ in blocks]
    for k in range(S5_SUB):
        for j, (c_re, c_im, gb, o) in enumerate(blocks):
            are = are_ref[gb, :, o:o + S5_SCAN_W]
            aim = aim_ref[gb, :, o:o + S5_SCAN_W]
            hre, him = carry[j]
            for t in range(k * sub, (k + 1) * sub):
                r0 = t * SUBLANES
                nre = are * hre - aim * him + hbuf[r0:r0 + SUBLANES, c_re:c_re + S5_SCAN_W]
                nim = are * him + aim * hre + hbuf[r0:r0 + SUBLANES, c_im:c_im + S5_SCAN_W]
                hbuf[r0:r0 + SUBLANES, c_re:c_re + S5_SCAN_W] = nre
                hbuf[r0:r0 + SUBLANES, c_im:c_im + S5_SCAN_W] = nim
                hre, him = nre, nim
            carry[j] = (hre, him)
    for (c_re, c_im, _, _), (hre, him) in zip(blocks, carry):
        state[:, c_re:c_re + S5_SCAN_W] = hre
        state[:, c_im:c_im + S5_SCAN_W] = him

    outs = []
    for k in range(S5_SUB):
        ys = [jnp.dot(hbuf[span(k), 2 * sw * gb:2 * sw * (gb + 1)].astype(BF16), cm_ref[gb],
                      preferred_element_type=F32) for gb in range(n_gb)]
        y = jnp.concatenate(ys, axis=1) + dskip_ref[...] * u[span(k)].astype(F32)
        g = jax.nn.gelu(y)
        gate = jnp.dot(g.astype(BF16), wglu_ref[...], preferred_element_type=F32) + bglu_ref[...]
        outs.append(g * jax.nn.sigmoid(gate))
    out = jnp.concatenate(outs, axis=0).reshape(steps, nb, d_ssm)
    y_ref[...] = jnp.swapaxes(out, 0, 1).astype(y_ref.dtype)
    hT_ref[...] = state[...]


def _s5_cast_kernel(u_ref, h0_ref, bm_ref, cm_ref, are_ref, aim_ref, dskip_ref, wglu_ref, bglu_ref,
                    src_ref, y_ref, hT_ref, dst_ref, hbuf, state, *, steps):
    _s5_kernel(u_ref, h0_ref, bm_ref, cm_ref, are_ref, aim_ref, dskip_ref, wglu_ref, bglu_ref,
               y_ref, hT_ref, hbuf, state, steps=steps)
    dst_ref[...] = src_ref[...].astype(dst_ref.dtype)


def _s5(u, h0, w, steps, cast_src=None):
    nb, t_len, d_ssm = u.shape
    assert nb == SUBLANES
    r = steps * SUBLANES
    n_state = h0.shape[1]
    n_steps = t_len // steps
    blk = pl.BlockSpec((nb, steps, d_ssm), lambda i: (0, i, 0))
    in_specs = [
        blk,
        _const_spec(h0.shape),
        _const_spec(w["bm"].shape),
        _const_spec(w["cm"].shape),
        _const_spec(w["are"].shape),
        _const_spec(w["aim"].shape),
        _const_spec((1, d_ssm)),
        _const_spec(w["wglu"].shape),
        _const_spec((1, d_ssm)),
    ]
    args = [u, h0, w["bm"], w["cm"], w["are"], w["aim"], w["dskip"], w["wglu"], w["bglu"]]
    out_shape = [jax.ShapeDtypeStruct(u.shape, BF16), jax.ShapeDtypeStruct(h0.shape, F32)]
    out_specs = [blk, pl.BlockSpec(h0.shape, lambda i: (0, 0))]
    weights = 2 * (w["bm"].size + w["cm"].size + w["wglu"].size) + 4 * (w["are"].size + w["aim"].size)
    tiles = 2 * r * d_ssm * 2 * 2 + 3 * 4 * h0.size
    scratch = 4 * r * n_state + 4 * h0.size
    temps = 2 * 4 * r * d_ssm
    body = _s5_kernel
    if cast_src is not None:
        src_rows, src_cols = cast_src.shape
        rows_blk = src_rows // n_steps
        assert rows_blk * n_steps == src_rows and rows_blk % (2 * SUBLANES) == 0
        cast_spec = pl.BlockSpec((rows_blk, src_cols), lambda i: (i, 0))
        in_specs.append(cast_spec)
        args.append(cast_src)
        out_specs.append(cast_spec)
        out_shape.append(jax.ShapeDtypeStruct(cast_src.shape, BF16))
        tiles += 2 * rows_blk * src_cols * (4 + 2)
        body = _s5_cast_kernel
    return pl.pallas_call(
        functools.partial(body, steps=steps),
        grid=(n_steps,),
        in_specs=in_specs,
        out_specs=tuple(out_specs),
        out_shape=tuple(out_shape),
        scratch_shapes=[pltpu.VMEM((r, n_state), F32), pltpu.VMEM(h0.shape, F32)],
        compiler_params=pltpu.CompilerParams(
            dimension_semantics=("arbitrary",),
            vmem_limit_bytes=_vmem_limit(weights + tiles + scratch + temps)),
        name="s5",
    )(*args)


def _attn_kernel(q_ref, k_ref, vt_ref, km_ref, vmt_ref, src_ref, o_ref, dst_ref):
    @pl.when(pl.program_id(1) == 0)
    def _():
        dst_ref[...] = src_ref[...].astype(dst_ref.dtype)

    seq = q_ref.shape[0]
    half = ATT_TQ // 2
    neg = jnp.finfo(F32).min
    kc = lax.broadcasted_iota(jnp.int32, (half, half), 0) // CHUNK
    qc = lax.broadcasted_iota(jnp.int32, (half, half), 1) // CHUNK
    visible = kc <= qc
    colmax = lambda t: jnp.max(t, axis=0, keepdims=True)
    prob = lambda s, m: jnp.exp2(s - m).astype(BF16)
    order = [(hh, i) for i in range(seq // ATT_TQ - 1, -1, -1) for hh in range(ATT_HEADS)]
    scores = {}
    for hh, i in order:
        r0 = i * ATT_TQ
        qs = slice(hh * HEAD_W, (hh + 1) * HEAD_W)
        q = q_ref[r0:r0 + ATT_TQ, qs]
        s_meta = lax.dot_general(km_ref[:, qs], q, _NT, preferred_element_type=F32)
        s_diag = lax.dot_general(k_ref[r0:r0 + ATT_TQ, qs], q, _NT, preferred_element_type=F32)
        s_full = (lax.dot_general(k_ref[0:r0, qs], q, _NT, preferred_element_type=F32)
                  if i > 0 else None)
        scores[hh, i] = (s_meta, s_diag, s_full)
    for hh, i in order:
        r0 = i * ATT_TQ
        vs = slice(hh * VT_ROWS, (hh + 1) * VT_ROWS)
        vmt = vmt_ref[vs, :]
        s_meta, s_diag, s_full = scores[hh, i]
        s00 = jnp.where(visible, s_diag[:half, :half], neg)
        s01 = s_diag[:half, half:]
        s11 = jnp.where(visible, s_diag[half:, half:], neg)
        m = jnp.concatenate([colmax(s00), jnp.maximum(colmax(s01), colmax(s11))], axis=1)
        m = jnp.maximum(m, colmax(s_meta))
        if i > 0:
            m = jnp.maximum(m, colmax(s_full))
        p_meta = prob(s_meta, m)
        p00 = prob(s00, m[:, :half])
        p01 = prob(s01, m[:, half:])
        p11 = prob(s11, m[:, half:])
        p_diag = jnp.concatenate([jnp.concatenate([p00, p01], axis=1),
                                  jnp.concatenate([jnp.zeros_like(p11), p11], axis=1)], axis=0)
        o = jnp.dot(vmt, p_meta, preferred_element_type=F32)
        o = o + jnp.dot(vt_ref[vs, r0:r0 + ATT_TQ], p_diag, preferred_element_type=F32)
        if i > 0:
            o = o + jnp.dot(vt_ref[vs, 0:r0], prob(s_full, m), preferred_element_type=F32)
        out = o[:V_HEAD] / o[V_HEAD:V_HEAD + 1]
        o_ref[r0:r0 + ATT_TQ, hh * V_HEAD:(hh + 1) * V_HEAD] = out.T.astype(o_ref.dtype)


def _attention(q, k, vt, km, vmt, cast_src, nb, seq):
    assert ATT_TQ % CHUNK == 0 and seq % ATT_TQ == 0 and V_HEAD == LANES and HEADS % ATT_HEADS == 0
    n_meta = km.shape[0]
    hw, vr = ATT_HEADS * HEAD_W, ATT_HEADS * VT_ROWS
    src_rows, src_cols = cast_src.shape
    blk = src_rows // nb
    assert blk * nb == src_rows and blk % (2 * SUBLANES) == 0
    cast_spec = pl.BlockSpec((blk, src_cols), lambda b, h: (b, 0))
    in_specs = [
        pl.BlockSpec((seq, hw), lambda b, h: (b, h)),
        pl.BlockSpec((seq, hw), lambda b, h: (b, h)),
        pl.BlockSpec((None, vr, seq), lambda b, h: (b, h, 0)),
        pl.BlockSpec((n_meta, hw), lambda b, h: (0, h)),
        pl.BlockSpec((vr, n_meta), lambda b, h: (h, 0)),
        cast_spec,
    ]
    tiles = 2 * 2 * seq * (2 * hw + vr + ATT_HEADS * V_HEAD) + 2 * blk * src_cols * (4 + 2)
    temps = 4 * 4 * seq * ATT_TQ * ATT_HEADS
    return pl.pallas_call(
        _attn_kernel,
        grid=(nb, HEADS // ATT_HEADS),
        in_specs=in_specs,
        out_specs=(pl.BlockSpec((seq, ATT_HEADS * V_HEAD), lambda b, h: (b, h)), cast_spec),
        out_shape=(jax.ShapeDtypeStruct((nb * seq, HEADS * V_HEAD), BF16),
                   jax.ShapeDtypeStruct(cast_src.shape, BF16)),
        compiler_params=pltpu.CompilerParams(
            dimension_semantics=("arbitrary", "arbitrary"),
            vmem_limit_bytes=_vmem_limit(tiles + temps)),
        name="attn",
    )(q, k, vt, km, vmt, cast_src)


def _attn_meta_kernel(q_ref, k_ref, vt_ref, o_ref):
    for h in range(HEADS):
        s = lax.dot_general(q_ref[:, h * HEAD_W:(h + 1) * HEAD_W],
                            k_ref[:, h * HEAD_W:(h + 1) * HEAD_W], _NT, preferred_element_type=F32)
        p = jnp.exp2(s - jnp.max(s, axis=1, keepdims=True))
        pv = lax.dot_general(p.astype(BF16), vt_ref[h * VT_ROWS:h * VT_ROWS + V_HEAD, :], _NT,
                             preferred_element_type=F32)
        o_ref[:, h * V_HEAD:(h + 1) * V_HEAD] = (pv / jnp.sum(p, axis=1, keepdims=True)).astype(o_ref.dtype)


def _attention_meta(q, k, vt):
    return pl.pallas_call(
        _attn_meta_kernel,
        out_shape=jax.ShapeDtypeStruct((q.shape[0], HEADS * V_HEAD), BF16),
        name="attn_meta",
    )(q, k, vt)


def _outproj_kernel(ya_ref, yb_ref, x_ref, ga_ref, gb_ref, wa_ref, wb_ref, o_ref):
    tm = x_ref.shape[0]
    n_sub = IN_SUB if tm % (IN_SUB * LANES) == 0 else 1
    ts = tm // n_sub
    for s in range(n_sub):
        rows = slice(s * ts, (s + 1) * ts)
        a = _rms(ya_ref[rows, :], ga_ref[...]).astype(BF16)
        b = _rms(yb_ref[rows, :], gb_ref[...]).astype(BF16)
        mix = jnp.dot(a, wa_ref[...], preferred_element_type=F32)
        mix = mix + jnp.dot(b, wb_ref[...], preferred_element_type=F32)
        o_ref[rows, :] = x_ref[rows, :].astype(F32) + mix


def _outproj(ya, yb, x2d, w, nb, nt, tm):
    rows, d = x2d.shape
    d_ssm, d_attn = w["wo_a"].shape[0], w["wo_b"].shape[0]
    row_spec = lambda width: pl.BlockSpec((tm, width), lambda b, j: (b * nt + j, 0))
    in_specs = [
        row_spec(d_ssm),
        row_spec(d_attn),
        row_spec(d),
        _const_spec((1, d_ssm)),
        _const_spec((1, d_attn)),
        _const_spec(w["wo_a"].shape),
        _const_spec(w["wo_b"].shape),
    ]
    weights = 2 * (w["wo_a"].size + w["wo_b"].size)
    tiles = 2 * tm * (2 * d_ssm + 2 * d_attn + 4 * d + 4 * d)
    temps = 4 * tm * d
    return pl.pallas_call(
        _outproj_kernel,
        grid=(nb, nt),
        in_specs=in_specs,
        out_specs=row_spec(d),
        out_shape=jax.ShapeDtypeStruct((rows, d), F32),
        compiler_params=pltpu.CompilerParams(
            dimension_semantics=("arbitrary", "arbitrary"),
            vmem_limit_bytes=_vmem_limit(weights + tiles + temps)),
        name="outproj",
    )(ya, yb, x2d, w["ga"], w["gb"], w["wo_a"], w["wo_b"])


def _gate_kernel(h_ref, g_ref, wg_ref, o_ref):
    xn = _rms(h_ref[...], g_ref[...]).astype(BF16)
    o_ref[...] = jnp.dot(xn, wg_ref[...], preferred_element_type=F32)


def _meta_gate(h_meta, wu, w):
    rows, d = h_meta.shape
    n_chunks = w["conv_w"].shape[0]
    gate = pl.pallas_call(
        _gate_kernel,
        grid=(n_chunks,),
        in_specs=[_const_spec((rows, d)), _const_spec((1, d)),
                  pl.BlockSpec((d, FFN_TF), lambda j: (0, j))],
        out_specs=pl.BlockSpec((rows, FFN_TF), lambda j: (0, j)),
        out_shape=jax.ShapeDtypeStruct((rows, n_chunks * FFN_TF), F32),
        compiler_params=pltpu.CompilerParams(dimension_semantics=("arbitrary",)),
        name="meta_gate",
    )(h_meta, w["ffn_g"], wu)
    halo = gate[rows - SUBLANES:].reshape(SUBLANES, n_chunks, FFN_TF)
    return jnp.swapaxes(halo, 0, 1)


def _shift_rows(x, prev, k):
    rolled = pltpu.roll(x, k, 0)
    head = jnp.concatenate([prev, x[:SUBLANES]], axis=0)[SUBLANES - k:2 * SUBLANES - k]
    return jnp.concatenate([head, rolled[SUBLANES:]], axis=0)


def _ffn_kernel(h_ref, g_ref, wu_hbm, wd_hbm, cw_ref, cb_ref, mh_ref, fg_ref, o_ref,
                xn_scr, halo_scr, wg_buf, wv_buf, wd_buf, sem, *, tiles_per_batch, d_ff):
    i = pl.program_id(0)
    tm = h_ref.shape[0]
    ts = tm // FFN_SUB
    n_full = d_ff // FFN_TF
    rem = d_ff - n_full * FFN_TF
    assert 0 < rem and rem % LANES == 0

    def chunk_copies(col0, width, slot):
        return (
            pltpu.make_async_copy(wu_hbm.at[:, pl.ds(col0, width)],
                                  wg_buf.at[slot, :, pl.ds(0, width)], sem.at[slot, 0]),
            pltpu.make_async_copy(wu_hbm.at[:, pl.ds(d_ff + col0, width)],
                                  wv_buf.at[slot, :, pl.ds(0, width)], sem.at[slot, 1]),
            pltpu.make_async_copy(wd_hbm.at[pl.ds(col0, width), :],
                                  wd_buf.at[slot, pl.ds(0, width), :], sem.at[slot, 2]),
        )

    def start(copies):
        for c in copies:
            c.start()

    def wait(copies):
        for c in copies:
            c.wait()

    ragged = lambda slot: chunk_copies(n_full * FFN_TF, rem, slot)
    full = lambda c, slot: chunk_copies(pl.multiple_of(c * FFN_TF, FFN_TF), FFN_TF, slot)

    def compute(idx, width, slot):
        wg = wg_buf[slot, :, :width]
        wv = wv_buf[slot, :, :width]
        wd = wd_buf[slot, :width, :]
        cw = cw_ref[idx][:, :width]
        cb = cb_ref[idx][:, :width]
        prev = halo_scr[idx][:, :width]
        for s in range(FFN_SUB):
            rows = pl.ds(s * ts, ts)
            xn = xn_scr[rows, :]
            gate = jnp.dot(xn, wg, preferred_element_type=F32)
            val = jnp.dot(xn, wv, preferred_element_type=F32)
            conv = (cw[0:1] * _shift_rows(gate, prev, 2)
                    + cw[1:2] * _shift_rows(gate, prev, 1)
                    + cw[2:3] * gate + cb)
            prev = gate[ts - SUBLANES:, :]
            act = (jax.nn.silu(conv) * val).astype(BF16)
            o_ref[rows, :] += jnp.dot(act, wd, preferred_element_type=F32)
        halo_scr[idx, :, pl.ds(0, width)] = prev

    assert n_full >= 2
    slot0 = (i * (n_full + 1)) % 2

    @pl.when(i == 0)
    def _():
        start(full(0, slot0))

    @pl.when((i % tiles_per_batch) == 0)
    def _():
        halo_scr[...] = mh_ref[...]

    h = h_ref[...]
    xn_scr[...] = _rms(h, g_ref[...]).astype(BF16)
    o_ref[...] = h

    def body(c, carry):
        slot = (slot0 + c) % 2
        wait(full(c, slot))

        @pl.when(c + 1 < n_full)
        def _():
            start(full(c + 1, 1 - slot))

        @pl.when(c + 1 == n_full)
        def _():
            start(ragged(1 - slot))

        compute(c, FFN_TF, slot)
        return carry

    lax.fori_loop(0, n_full, body, 0)

    slot_r = (slot0 + n_full) % 2
    wait(ragged(slot_r))

    @pl.when(i + 1 < pl.num_programs(0))
    def _():
        start(full(0, 1 - slot_r))

    compute(n_full, rem, slot_r)
    o_ref[...] = _rms(o_ref[...], fg_ref[...])


def _ffn(h1, meta_halo, wu, wd, w, tiles_per_batch, tm):
    rows, d = h1.shape
    d_ff = wd.shape[0]
    n_chunks = w["conv_w"].shape[0]
    in_specs = [
        pl.BlockSpec((tm, d), lambda i: (i, 0)),
        _const_spec((1, d)),
        pl.BlockSpec(memory_space=pl.ANY),
        pl.BlockSpec(memory_space=pl.ANY),
        _const_spec(w["conv_w"].shape),
        _const_spec(w["conv_b"].shape),
        _const_spec(meta_halo.shape),
        _const_spec((1, d)),
    ]
    slots = 2
    tiles = 2 * 2 * 4 * tm * d + 4 * (w["conv_w"].size + w["conv_b"].size + meta_halo.size)
    scratch = 2 * tm * d + 4 * meta_halo.size + slots * 3 * 2 * d * FFN_TF
    temps = 4 * (tm // FFN_SUB) * FFN_TF * 16
    return pl.pallas_call(
        functools.partial(_ffn_kernel, tiles_per_batch=tiles_per_batch, d_ff=d_ff),
        grid=(rows // tm,),
        in_specs=in_specs,
        out_specs=pl.BlockSpec((tm, d), lambda i: (i, 0)),
        out_shape=jax.ShapeDtypeStruct((rows, d), F32),
        scratch_shapes=[
            pltpu.VMEM((tm, d), BF16),
            pltpu.VMEM((n_chunks, SUBLANES, FFN_TF), F32),
            pltpu.VMEM((slots, d, FFN_TF), BF16),
            pltpu.VMEM((slots, d, FFN_TF), BF16),
            pltpu.VMEM((slots, FFN_TF, d), BF16),
            pltpu.SemaphoreType.DMA((slots, 3)),
        ],
        compiler_params=pltpu.CompilerParams(
            dimension_semantics=("arbitrary",),
            vmem_limit_bytes=_vmem_limit(tiles + scratch + temps)),
        name="ffn",
    )(h1, w["ffn_g"], wu, wd, w["conv_w"], w["conv_b"], meta_halo, w["final_g"])


def _swap_halves(w):
    half = w.shape[-1] // 2
    return jnp.concatenate([-w[..., half:], w[..., :half]], axis=-1)


def _block_diag(blocks, n):
    rows, c = blocks.shape[-2:]
    r = rows // n
    rep = (jnp.arange(c)[:, None] == jnp.arange(n * c)[None, :] % c).astype(blocks.dtype)
    tiled = jnp.einsum("...rc,cq->...rq", blocks, rep, precision=lax.Precision.HIGHEST)
    keep = (jnp.arange(rows)[:, None] // r) == (jnp.arange(n * c)[None, :] // c)
    return jnp.where(keep, tiled, 0.0)


def _prepare(p):
    d_ssm = p["d_skip"].shape[-1]
    q_lora = p["q_a_norm"].shape[-1]
    kv_lora = p["kv_a_norm"].shape[-1]
    d_ff = p["w_down"].shape[0]
    w = {"d_ssm": d_ssm, "scale": math.log2(math.e) / math.sqrt(QK_NOPE + QK_ROPE)}

    row = lambda v: v.reshape(1, -1).astype(F32)
    w["mix_g"] = row(p["mix_norm"])
    w["win"] = p["w_in"].astype(BF16)
    w["qg"] = row(p["q_a_norm"])
    w["kvg"] = row(p["kv_a_norm"])
    wq = p["w_q_b"].reshape(q_lora, HEADS, QK_NOPE + QK_ROPE)
    wq_pe = wq[..., QK_NOPE:]
    w["wq"] = jnp.concatenate([wq, _swap_halves(wq_pe)], axis=-1).reshape(q_lora, HEADS * HEAD_W).astype(BF16)
    wkv = p["w_kv_b"].reshape(kv_lora, HEADS, QK_NOPE + V_HEAD)
    w["wk"] = wkv[..., :QK_NOPE].reshape(kv_lora, HEADS * QK_NOPE).astype(BF16)
    wv = jnp.pad(jnp.moveaxis(wkv[..., QK_NOPE:], 0, 2), ((0, 0), (0, VT_ROWS - V_HEAD), (0, 0)))
    w["wv"] = wv.reshape(HEADS * VT_ROWS, kv_lora).astype(BF16)
    w["vone"] = jnp.tile((jnp.arange(VT_ROWS) == V_HEAD).astype(F32), HEADS).reshape(-1, 1)

    lam_re, lam_im = p["lam_re"].astype(F32), p["lam_im"].astype(F32)
    dt = jnp.exp(p["log_dt"].astype(F32))[:, None]
    mag = jnp.exp(lam_re * dt)
    a_re, a_im = mag * jnp.cos(lam_im * dt), mag * jnp.sin(lam_im * dt)
    den = lam_re * lam_re + lam_im * lam_im
    f_re = ((a_re - 1.0) * lam_re + a_im * lam_im) / den
    f_im = (a_im * lam_re - (a_re - 1.0) * lam_im) / den
    b_re, b_im = p["b_re"].astype(F32), p["b_im"].astype(F32)
    bb_re = f_re[..., None] * b_re - f_im[..., None] * b_im
    bb_im = f_re[..., None] * b_im + f_im[..., None] * b_re
    n_g = lam_re.shape[0]
    n_gb = n_g // GROUPS_PER_MXU
    bd = lambda t: _block_diag(t.reshape(n_gb, GROUPS_PER_MXU * t.shape[1], t.shape[2]), GROUPS_PER_MXU)
    w["bm"] = jnp.concatenate([bd(jnp.swapaxes(bb_re, 1, 2)),
                               bd(jnp.swapaxes(bb_im, 1, 2))], axis=2).astype(BF16)
    c_re, c_im = p["c_re"].astype(F32), p["c_im"].astype(F32)
    w["cm"] = jnp.concatenate([bd(jnp.swapaxes(c_re, 1, 2)),
                               bd(jnp.swapaxes(-c_im, 1, 2))], axis=1).astype(BF16)
    bcast = lambda a: jnp.broadcast_to(a.reshape(n_gb, 1, STATE_W), (n_gb, SUBLANES, STATE_W))
    w["are"], w["aim"] = bcast(a_re), bcast(a_im)
    w["dskip"] = row(p["d_skip"])
    w["wglu"] = p["w_glu"].astype(BF16)
    w["bglu"] = row(p["b_glu"])

    w["ga"], w["gb"] = row(p["out_norm_ssm"]), row(p["out_norm_attn"])
    w["wo_a"] = p["w_out"][:d_ssm].astype(BF16)
    w["wo_b"] = p["w_out"][d_ssm:].astype(BF16)

    n_chunks = -(-d_ff // FFN_TF)
    pad = n_chunks * FFN_TF - d_ff
    per_chunk = lambda a: jnp.swapaxes(
        jnp.pad(a.astype(F32), ((0, 0), (0, pad))).reshape(a.shape[0], n_chunks, FFN_TF), 0, 1)
    w["conv_w"] = per_chunk(p["conv_w"])
    w["conv_b"] = per_chunk(p["conv_b"].reshape(1, -1))
    w["ffn_g"] = row(p["ffn_norm"])
    w["final_g"] = row(p["final_norm"])
    return w


def _rope_table(n_pos):
    pos = jnp.arange(n_pos, dtype=F32)
    inv_freq = 1.0 / (ROPE_BASE ** (jnp.arange(0, QK_ROPE, 2, dtype=F32) / QK_ROPE))
    ang = pos[:, None] * inv_freq[None, :]
    cos, sin = jnp.cos(ang), jnp.sin(ang)
    return jnp.concatenate([cos, cos, sin, sin], axis=1)


def kernel(x, meta_tokens, mix_norm, w_in, lam_re, lam_im, log_dt, b_re, b_im, c_re, c_im, d_skip, w_glu, b_glu, q_a_norm, w_q_b, kv_a_norm, w_kv_b, out_norm_ssm, out_norm_attn, w_out, ffn_norm, w_up, conv_w, conv_b, w_down, final_norm):
    bsz, seq, d = x.shape
    assert meta_tokens.shape == (N_META, d) and bsz == SUBLANES
    p = dict(mix_norm=mix_norm[0], w_in=w_in[0], lam_re=lam_re[0], lam_im=lam_im[0], log_dt=log_dt[0],
             b_re=b_re[0], b_im=b_im[0], c_re=c_re[0], c_im=c_im[0], d_skip=d_skip[0],
             w_glu=w_glu[0], b_glu=b_glu[0], q_a_norm=q_a_norm[0], w_q_b=w_q_b[0],
             kv_a_norm=kv_a_norm[0], w_kv_b=w_kv_b[0], out_norm_ssm=out_norm_ssm[0],
             out_norm_attn=out_norm_attn[0], w_out=w_out[0], ffn_norm=ffn_norm[0], w_up=w_up[0],
             conv_w=conv_w[0], conv_b=conv_b[0], w_down=w_down[0], final_norm=final_norm)
    w = _prepare(p)
    d_ssm = w["d_ssm"]
    cs = _rope_table(N_META + seq)
    x2d = x.reshape(bsz * seq, d)
    meta = meta_tokens.astype(x.dtype)

    u_m, q_m, k_m, v_m = _inproj(meta, cs[:N_META], w, 1, 1, N_META)
    n_state = 2 * STATE_W * w["bm"].shape[0]
    ya_m8, h_meta_state = _s5(jnp.broadcast_to(u_m[None], (SUBLANES, N_META, d_ssm)),
                              jnp.zeros((SUBLANES, n_state), F32), w, N_META)
    ya_m = ya_m8[0]
    yb_m = _attention_meta(q_m, k_m, v_m[0])
    h1_m = _outproj(ya_m, yb_m, meta, w, 1, 1, N_META)

    tm = 512
    nt = seq // tm
    u, q, k, v = _inproj(x2d, cs[N_META:], w, bsz, nt, tm)
    ya, _, wu = _s5(u.reshape(bsz, seq, d_ssm), h_meta_state, w, 64, cast_src=w_up[0])
    yb, wd = _attention(q, k, v, k_m, v_m[0], w_down[0], bsz, seq)
    h1 = _outproj(ya.reshape(bsz * seq, d_ssm), yb, x2d, w, bsz, nt, tm)
    meta_halo = _meta_gate(h1_m, wu, w)
    out = _ffn(h1, meta_halo, wu, wd, w, nt, tm)
    return out.reshape(bsz, seq, d)
```

```python
import functools
import math

import jax
import jax.numpy as jnp
from jax import lax
from jax.experimental import pallas as pl
from jax.experimental.pallas import tpu as pltpu

F32 = jnp.float32
BF16 = jnp.bfloat16

EPS = 1e-6
CHUNK = 64
N_META = 16
SSM_GROUP = 16
SSM_STATE = 64
HEADS = 8
QK_NOPE = 128
QK_ROPE = 64
V_HEAD = 128
ROPE_BASE = 10000.0

LANES = 128
SUBLANES = 8
MXU_DIM = 256
VMEM_BYTES = 64 * 1024 * 1024

HEAD_W = QK_NOPE + 2 * QK_ROPE
GROUPS_PER_MXU = MXU_DIM // SSM_GROUP
STATE_W = GROUPS_PER_MXU * SSM_STATE

ATT_TQ = 256
ATT_HEADS = 2
VT_ROWS = V_HEAD + SUBLANES
FFN_TF = 1024
IN_SUB = 2
FFN_SUB = 2
S5_SUB = 2
S5_SCAN_W = 512


_NT = (((1,), (1,)), ((), ()))


def _vmem_limit(nbytes):
    return int(min(nbytes + (4 << 20), VMEM_BYTES - (4 << 20)))


def _rms(x, g):
    x = x.astype(F32)
    return x * lax.rsqrt(jnp.mean(x * x, axis=-1, keepdims=True) + EPS) * g


def _const_spec(shape):
    nd = len(shape)
    return pl.BlockSpec(shape, lambda *_: (0,) * nd, pipeline_mode=pl.Buffered(1))


def _inproj_kernel(x_ref, g_ref, win_ref, qg_ref, wq_ref, kvg_ref, wk_ref, wv_ref, vone_ref,
                   cs_ref, u_ref, q_ref, k_ref, v_ref, *, scale):
    d_ssm = u_ref.shape[1]
    q_lora = wq_ref.shape[0]
    kv_lora = wk_ref.shape[0]
    o1, o2, o3 = d_ssm, d_ssm + q_lora, d_ssm + q_lora + kv_lora
    tm = x_ref.shape[0]
    n_sub = IN_SUB if tm % (IN_SUB * LANES) == 0 else 1
    ts = tm // n_sub
    zs = []
    for s in range(n_sub):
        xn = _rms(x_ref[s * ts:(s + 1) * ts, :], g_ref[...]).astype(BF16)
        zs.append(jnp.dot(xn, win_ref[...], preferred_element_type=F32))
    for s, z in enumerate(zs):
        rows = slice(s * ts, (s + 1) * ts)
        u_ref[rows, :] = z[:, :o1].astype(u_ref.dtype)

        cs = cs_ref[rows, :]
        qn = _rms(z[:, o1:o2], qg_ref[...]).astype(BF16)
        q = jnp.dot(qn, wq_ref[...], preferred_element_type=F32)
        cs_q = cs * scale
        for h in range(HEADS):
            c0 = h * HEAD_W
            q_ref[rows, c0:c0 + QK_NOPE] = (q[:, c0:c0 + QK_NOPE] * scale).astype(q_ref.dtype)
            q_ref[rows, c0 + QK_NOPE:c0 + HEAD_W] = (q[:, c0 + QK_NOPE:c0 + HEAD_W] * cs_q).astype(q_ref.dtype)

        kvn = _rms(z[:, o2:o3], kvg_ref[...]).astype(BF16)
        kn = jnp.dot(kvn, wk_ref[...], preferred_element_type=F32)
        vt = lax.dot_general(wv_ref[...], kvn, _NT, preferred_element_type=F32)
        v_ref[:, rows] = (vt + vone_ref[...]).astype(v_ref.dtype)
        kpe = z[:, o3:o3 + QK_ROPE]
        x1, x2 = kpe[:, :QK_ROPE // 2], kpe[:, QK_ROPE // 2:]
        t = jnp.concatenate([kpe, -x2, x1], axis=1) * cs
        krot = (t + pltpu.roll(t, QK_ROPE, 1)).astype(k_ref.dtype)
        for h in range(HEADS):
            c0 = h * HEAD_W
            k_ref[rows, c0:c0 + QK_NOPE] = kn[:, h * QK_NOPE:(h + 1) * QK_NOPE].astype(k_ref.dtype)
            k_ref[rows, c0 + QK_NOPE:c0 + HEAD_W] = krot


def _inproj(x2d, cs, w, nb, nt, tm):
    rows, d = x2d.shape
    d_ssm = w["d_ssm"]
    n_in = w["win"].shape[1]
    row_spec = lambda width: pl.BlockSpec((tm, width), lambda b, j: (b * nt + j, 0))
    in_specs = [
        row_spec(d),
        _const_spec((1, d)),
        _const_spec(w["win"].shape),
        _const_spec(w["qg"].shape),
        _const_spec(w["wq"].shape),
        _const_spec(w["kvg"].shape),
        _const_spec(w["wk"].shape),
        _const_spec(w["wv"].shape),
        _const_spec(w["vone"].shape),
        pl.BlockSpec((tm, LANES), lambda b, j: (j, 0)),
    ]
    out_shape = (
        jax.ShapeDtypeStruct((rows, d_ssm), BF16),
        jax.ShapeDtypeStruct((rows, HEADS * HEAD_W), BF16),
        jax.ShapeDtypeStruct((rows, HEADS * HEAD_W), BF16),
        jax.ShapeDtypeStruct((nb, HEADS * VT_ROWS, nt * tm), BF16),
    )
    out_specs = (
        row_spec(d_ssm),
        row_spec(HEADS * HEAD_W),
        row_spec(HEADS * HEAD_W),
        pl.BlockSpec((None, HEADS * VT_ROWS, tm), lambda b, j: (b, 0, j)),
    )
    weights = 2 * (w["win"].size + w["wq"].size + w["wk"].size + w["wv"].size)
    tiles = 2 * tm * (4 * d + 2 * d_ssm + 4 * HEADS * HEAD_W + 2 * HEADS * V_HEAD + 4 * LANES)
    temps = 4 * tm * (d + n_in)
    return pl.pallas_call(
        functools.partial(_inproj_kernel, scale=w["scale"]),
        grid=(nb, nt),
        in_specs=in_specs,
        out_specs=out_specs,
        out_shape=out_shape,
        compiler_params=pltpu.CompilerParams(
            dimension_semantics=("arbitrary", "arbitrary"),
            vmem_limit_bytes=_vmem_limit(weights + tiles + temps)),
        name="inproj",
    )(x2d, w["mix_g"], w["win"], w["qg"], w["wq"], w["kvg"], w["wk"], w["wv"], w["vone"], cs)


def _s5_kernel(u_ref, h0_ref, bm_ref, cm_ref, are_ref, aim_ref, dskip_ref, wglu_ref, bglu_ref,
               y_ref, hT_ref, hbuf, state, *, steps):
    n_gb = bm_ref.shape[0]
    sw = are_ref.shape[2]
    cin = bm_ref.shape[1]
    nb, _, d_ssm = u_ref.shape

    @pl.when(pl.program_id(0) == 0)
    def _():
        state[...] = h0_ref[...]

    u = jnp.swapaxes(u_ref[...].astype(F32), 0, 1).reshape(nb * steps, d_ssm).astype(BF16)

    sub = steps // S5_SUB
    span = lambda k: slice(k * sub * SUBLANES, (k + 1) * sub * SUBLANES)
    for k in range(S5_SUB):
        for gb in range(n_gb):
            hbuf[span(k), 2 * sw * gb:2 * sw * (gb + 1)] = jnp.dot(
                u[span(k), cin * gb:cin * (gb + 1)], bm_ref[gb], preferred_element_type=F32)

    blocks = [(2 * sw * gb + o, 2 * sw * gb + sw + o, gb, o)
              for gb in range(n_gb) for o in range(0, sw, S5_SCAN_W)]
    carry = [(state[:, c_re:c_re + S5_SCAN_W], state[:, c_im:c_im + S5_SCAN_W])
             for c_re, c_im, _, _ in blocks]
    for k in range(S5_SUB):
        for j, (c_re, c_im, gb, o) in enumerate(blocks):
            are = are_ref[gb, :, o:o + S5_SCAN_W]
            aim = aim_ref[gb, :, o:o + S5_SCAN_W]
            hre, him = carry[j]
            for t in range(k * sub, (k + 1) * sub):
                r0 = t * SUBLANES
                nre = are * hre - aim * him + hbuf[r0:r0 + SUBLANES, c_re:c_re + S5_SCAN_W]
                nim = are * him + aim * hre + hbuf[r0:r0 + SUBLANES, c_im:c_im + S5_SCAN_W]
                hbuf[r0:r0 + SUBLANES, c_re:c_re + S5_SCAN_W] = nre
                hbuf[r0:r0 + SUBLANES, c_im:c_im + S5_SCAN_W] = nim
                hre, him = nre, nim
            carry[j] = (hre, him)
    for (c_re, c_im, _, _), (hre, him) in zip(blocks, carry):
        state[:, c_re:c_re + S5_SCAN_W] = hre
        state[:, c_im:c_im + S5_SCAN_W] = him

    outs = []
    for k in range(S5_SUB):
        ys = [jnp.dot(hbuf[span(k), 2 * sw * gb:2 * sw * (gb + 1)].astype(BF16), cm_ref[gb],
                      preferred_element_type=F32) for gb in range(n_gb)]
        y = jnp.concatenate(ys, axis=1) + dskip_ref[...] * u[span(k)].astype(F32)
        g = jax.nn.gelu(y)
        gate = jnp.dot(g.astype(BF16), wglu_ref[...], preferred_element_type=F32) + bglu_ref[...]
        outs.append(g * jax.nn.sigmoid(gate))
    out = jnp.concatenate(outs, axis=0).reshape(steps, nb, d_ssm)
    y_ref[...] = jnp.swapaxes(out, 0, 1).astype(y_ref.dtype)
    hT_ref[...] = state[...]


def _s5_cast_kernel(u_ref, h0_ref, bm_ref, cm_ref, are_ref, aim_ref, dskip_ref, wglu_ref, bglu_ref,
                    src_ref, y_ref, hT_ref, dst_ref, hbuf, state, *, steps):
    _s5_kernel(u_ref, h0_ref, bm_ref, cm_ref, are_ref, aim_ref, dskip_ref, wglu_ref, bglu_ref,
               y_ref, hT_ref, hbuf, state, steps=steps)
    dst_ref[...] = src_ref[...].astype(dst_ref.dtype)


def _s5(u, h0, w, steps, cast_src=None):
    nb, t_len, d_ssm = u.shape
    assert nb == SUBLANES
    r = steps * SUBLANES
    n_state = h0.shape[1]
    n_steps = t_len // steps
    blk = pl.BlockSpec((nb, steps, d_ssm), lambda i: (0, i, 0))
    in_specs = [
        blk,
        _const_spec(h0.shape),
        _const_spec(w["bm"].shape),
        _const_spec(w["cm"].shape),
        _const_spec(w["are"].shape),
        _const_spec(w["aim"].shape),
        _const_spec((1, d_ssm)),
        _const_spec(w["wglu"].shape),
        _const_spec((1, d_ssm)),
    ]
    args = [u, h0, w["bm"], w["cm"], w["are"], w["aim"], w["dskip"], w["wglu"], w["bglu"]]
    out_shape = [jax.ShapeDtypeStruct(u.shape, BF16), jax.ShapeDtypeStruct(h0.shape, F32)]
    out_specs = [blk, pl.BlockSpec(h0.shape, lambda i: (0, 0))]
    weights = 2 * (w["bm"].size + w["cm"].size + w["wglu"].size) + 4 * (w["are"].size + w["aim"].size)
    tiles = 2 * r * d_ssm * 2 * 2 + 3 * 4 * h0.size
    scratch = 4 * r * n_state + 4 * h0.size
    temps = 2 * 4 * r * d_ssm
    body = _s5_kernel
    if cast_src is not None:
        src_rows, src_cols = cast_src.shape
        rows_blk = src_rows // n_steps
        assert rows_blk * n_steps == src_rows and rows_blk % (2 * SUBLANES) == 0
        cast_spec = pl.BlockSpec((rows_blk, src_cols), lambda i: (i, 0))
        in_specs.append(cast_spec)
        args.append(cast_src)
        out_specs.append(cast_spec)
        out_shape.append(jax.ShapeDtypeStruct(cast_src.shape, BF16))
        tiles += 2 * rows_blk * src_cols * (4 + 2)
        body = _s5_cast_kernel
    return pl.pallas_call(
        functools.partial(body, steps=steps),
        grid=(n_steps,),
        in_specs=in_specs,
        out_specs=tuple(out_specs),
        out_shape=tuple(out_shape),
        scratch_shapes=[pltpu.VMEM((r, n_state), F32), pltpu.VMEM(h0.shape, F32)],
        compiler_params=pltpu.CompilerParams(
            dimension_semantics=("arbitrary",),
            vmem_limit_bytes=_vmem_limit(weights + tiles + scratch + temps)),
        name="s5",
    )(*args)


def _attn_kernel(q_ref, k_ref, vt_ref, km_ref, vmt_ref, src_ref, o_ref, dst_ref):
    @pl.when(pl.program_id(1) == 0)
    def _():
        dst_ref[...] = src_ref[...].astype(dst_ref.dtype)

    seq = q_ref.shape[0]
    half = ATT_TQ // 2
    neg = jnp.finfo(F32).min
    kc = lax.broadcasted_iota(jnp.int32, (half, half), 0) // CHUNK
    qc = lax.broadcasted_iota(jnp.int32, (half, half), 1) // CHUNK
    visible = kc <= qc
    colmax = lambda t: jnp.max(t, axis=0, keepdims=True)
    prob = lambda s, m: jnp.exp2(s - m).astype(BF16)
    order = [(hh, i) for hh in range(ATT_HEADS) for i in range(seq // ATT_TQ - 1, -1, -1)]
    scores = {}
    for hh, i in order:
        r0 = i * ATT_TQ
        qs = slice(hh * HEAD_W, (hh + 1) * HEAD_W)
        q = q_ref[r0:r0 + ATT_TQ, qs]
        s_meta = lax.dot_general(km_ref[:, qs], q, _NT, preferred_element_type=F32)
        s_diag = lax.dot_general(k_ref[r0:r0 + ATT_TQ, qs], q, _NT, preferred_element_type=F32)
        s_full = (lax.dot_general(k_ref[0:r0, qs], q, _NT, preferred_element_type=F32)
                  if i > 0 else None)
        scores[hh, i] = (s_meta, s_diag, s_full)
    for hh, i in order:
        r0 = i * ATT_TQ
        vs = slice(hh * VT_ROWS, (hh + 1) * VT_ROWS)
        vmt = vmt_ref[vs, :]
        s_meta, s_diag, s_full = scores[hh, i]
        s00 = jnp.where(visible, s_diag[:half, :half], neg)
        s01 = s_diag[:half, half:]
        s11 = jnp.where(visible, s_diag[half:, half:], neg)
        m = jnp.concatenate([colmax(s00), jnp.maximum(colmax(s01), colmax(s11))], axis=1)
        m = jnp.maximum(m, colmax(s_meta))
        if i > 0:
            m = jnp.maximum(m, colmax(s_full))
        p_meta = prob(s_meta, m)
        p00 = prob(s00, m[:, :half])
        p01 = prob(s01, m[:, half:])
        p11 = prob(s11, m[:, half:])
        p_diag = jnp.concatenate([jnp.concatenate([p00, p01], axis=1),
                                  jnp.concatenate([jnp.zeros_like(p11), p11], axis=1)], axis=0)
        o = jnp.dot(vmt, p_meta, preferred_element_type=F32)
        o = o + jnp.dot(vt_ref[vs, r0:r0 + ATT_TQ], p_diag, preferred_element_type=F32)
        if i > 0:
            o = o + jnp.dot(vt_ref[vs, 0:r0], prob(s_full, m), preferred_element_type=F32)
        out = o[:V_HEAD] / o[V_HEAD:V_HEAD + 1]
        o_ref[r0:r0 + ATT_TQ, hh * V_HEAD:(hh + 1) * V_HEAD] = out.T.astype(o_ref.dtype)


def _attention(q, k, vt, km, vmt, cast_src, nb, seq):
    assert ATT_TQ % CHUNK == 0 and seq % ATT_TQ == 0 and V_HEAD == LANES and HEADS % ATT_HEADS == 0
    n_meta = km.shape[0]
    hw, vr = ATT_HEADS * HEAD_W, ATT_HEADS * VT_ROWS
    src_rows, src_cols = cast_src.shape
    blk = src_rows // nb
    assert blk * nb == src_rows and blk % (2 * SUBLANES) == 0
    cast_spec = pl.BlockSpec((blk, src_cols), lambda b, h: (b, 0))
    in_specs = [
        pl.BlockSpec((seq, hw), lambda b, h: (b, h)),
        pl.BlockSpec((seq, hw), lambda b, h: (b, h)),
        pl.BlockSpec((None, vr, seq), lambda b, h: (b, h, 0)),
        pl.BlockSpec((n_meta, hw), lambda b, h: (0, h)),
        pl.BlockSpec((vr, n_meta), lambda b, h: (h, 0)),
        cast_spec,
    ]
    tiles = 2 * 2 * seq * (2 * hw + vr + ATT_HEADS * V_HEAD) + 2 * blk * src_cols * (4 + 2)
    temps = 4 * 4 * seq * ATT_TQ * ATT_HEADS
    return pl.pallas_call(
        _attn_kernel,
        grid=(nb, HEADS // ATT_HEADS),
        in_specs=in_specs,
        out_specs=(pl.BlockSpec((seq, ATT_HEADS * V_HEAD), lambda b, h: (b, h)), cast_spec),
        out_shape=(jax.ShapeDtypeStruct((nb * seq, HEADS * V_HEAD), BF16),
                   jax.ShapeDtypeStruct(cast_src.shape, BF16)),
        compiler_params=pltpu.CompilerParams(
            dimension_semantics=("arbitrary", "arbitrary"),
            vmem_limit_bytes=_vmem_limit(tiles + temps)),
        name="attn",
    )(q, k, vt, km, vmt, cast_src)


def _attn_meta_kernel(q_ref, k_ref, vt_ref, o_ref):
    for h in range(HEADS):
        s = lax.dot_general(q_ref[:, h * HEAD_W:(h + 1) * HEAD_W],
                            k_ref[:, h * HEAD_W:(h + 1) * HEAD_W], _NT, preferred_element_type=F32)
        p = jnp.exp2(s - jnp.max(s, axis=1, keepdims=True))
        pv = lax.dot_general(p.astype(BF16), vt_ref[h * VT_ROWS:h * VT_ROWS + V_HEAD, :], _NT,
                             preferred_element_type=F32)
        o_ref[:, h * V_HEAD:(h + 1) * V_HEAD] = (pv / jnp.sum(p, axis=1, keepdims=True)).astype(o_ref.dtype)


def _attention_meta(q, k, vt):
    return pl.pallas_call(
        _attn_meta_kernel,
        out_shape=jax.ShapeDtypeStruct((q.shape[0], HEADS * V_HEAD), BF16),
        name="attn_meta",
    )(q, k, vt)


def _outproj_kernel(ya_ref, yb_ref, x_ref, ga_ref, gb_ref, wa_ref, wb_ref, o_ref):
    tm = x_ref.shape[0]
    n_sub = IN_SUB if tm % (IN_SUB * LANES) == 0 else 1
    ts = tm // n_sub
    for s in range(n_sub):
        rows = slice(s * ts, (s + 1) * ts)
        a = _rms(ya_ref[rows, :], ga_ref[...]).astype(BF16)
        b = _rms(yb_ref[rows, :], gb_ref[...]).astype(BF16)
        mix = jnp.dot(a, wa_ref[...], preferred_element_type=F32)
        mix = mix + jnp.dot(b, wb_ref[...], preferred_element_type=F32)
        o_ref[rows, :] = x_ref[rows, :].astype(F32) + mix


def _outproj(ya, yb, x2d, w, nb, nt, tm):
    rows, d = x2d.shape
    d_ssm, d_attn = w["wo_a"].shape[0], w["wo_b"].shape[0]
    row_spec = lambda width: pl.BlockSpec((tm, width), lambda b, j: (b * nt + j, 0))
    in_specs = [
        row_spec(d_ssm),
        row_spec(d_attn),
        row_spec(d),
        _const_spec((1, d_ssm)),
        _const_spec((1, d_attn)),
        _const_spec(w["wo_a"].shape),
        _const_spec(w["wo_b"].shape),
    ]
    weights = 2 * (w["wo_a"].size + w["wo_b"].size)
    tiles = 2 * tm * (2 * d_ssm + 2 * d_attn + 4 * d + 4 * d)
    temps = 4 * tm * d
    return pl.pallas_call(
        _outproj_kernel,
        grid=(nb, nt),
        in_specs=in_specs,
        out_specs=row_spec(d),
        out_shape=jax.ShapeDtypeStruct((rows, d), F32),
        compiler_params=pltpu.CompilerParams(
            dimension_semantics=("arbitrary", "arbitrary"),
            vmem_limit_bytes=_vmem_limit(weights + tiles + temps)),
        name="outproj",
    )(ya, yb, x2d, w["ga"], w["gb"], w["wo_a"], w["wo_b"])


def _gate_kernel(h_ref, g_ref, wg_ref, o_ref):
    xn = _rms(h_ref[...], g_ref[...]).astype(BF16)
    o_ref[...] = jnp.dot(xn, wg_ref[...], preferred_element_type=F32)


def _meta_gate(h_meta, wu, w):
    rows, d = h_meta.shape
    n_chunks = w["conv_w"].shape[0]
    gate = pl.pallas_call(
        _gate_kernel,
        grid=(n_chunks,),
        in_specs=[_const_spec((rows, d)), _const_spec((1, d)),
                  pl.BlockSpec((d, FFN_TF), lambda j: (0, j))],
        out_specs=pl.BlockSpec((rows, FFN_TF), lambda j: (0, j)),
        out_shape=jax.ShapeDtypeStruct((rows, n_chunks * FFN_TF), F32),
        compiler_params=pltpu.CompilerParams(dimension_semantics=("arbitrary",)),
        name="meta_gate",
    )(h_meta, w["ffn_g"], wu)
    halo = gate[rows - SUBLANES:].reshape(SUBLANES, n_chunks, FFN_TF)
    return jnp.swapaxes(halo, 0, 1)


def _shift_rows(x, prev, k):
    rolled = pltpu.roll(x, k, 0)
    head = jnp.concatenate([prev, x[:SUBLANES]], axis=0)[SUBLANES - k:2 * SUBLANES - k]
    return jnp.concatenate([head, rolled[SUBLANES:]], axis=0)


def _ffn_kernel(h_ref, g_ref, wu_hbm, wd_hbm, cw_ref, cb_ref, mh_ref, fg_ref, o_ref,
                xn_scr, halo_scr, wg_buf, wv_buf, wd_buf, sem, *, tiles_per_batch, d_ff):
    i = pl.program_id(0)
    tm = h_ref.shape[0]
    ts = tm // FFN_SUB
    n_full = d_ff // FFN_TF
    rem = d_ff - n_full * FFN_TF
    assert 0 < rem and rem % LANES == 0

    def chunk_copies(col0, width, slot):
        return (
            pltpu.make_async_copy(wu_hbm.at[:, pl.ds(col0, width)],
                                  wg_buf.at[slot, :, pl.ds(0, width)], sem.at[slot, 0]),
            pltpu.make_async_copy(wu_hbm.at[:, pl.ds(d_ff + col0, width)],
                                  wv_buf.at[slot, :, pl.ds(0, width)], sem.at[slot, 1]),
            pltpu.make_async_copy(wd_hbm.at[pl.ds(col0, width), :],
                                  wd_buf.at[slot, pl.ds(0, width), :], sem.at[slot, 2]),
        )

    def start(copies):
        for c in copies:
            c.start()

    def wait(copies):
        for c in copies:
            c.wait()

    ragged = lambda slot: chunk_copies(n_full * FFN_TF, rem, slot)
    full = lambda c, slot: chunk_copies(pl.multiple_of(c * FFN_TF, FFN_TF), FFN_TF, slot)

    def compute(idx, width, slot):
        wg = wg_buf[slot, :, :width]
        wv = wv_buf[slot, :, :width]
        wd = wd_buf[slot, :width, :]
        cw = cw_ref[idx][:, :width]
        cb = cb_ref[idx][:, :width]
        prev = halo_scr[idx][:, :width]
        for s in range(FFN_SUB):
            rows = pl.ds(s * ts, ts)
            xn = xn_scr[rows, :]
            gate = jnp.dot(xn, wg, preferred_element_type=F32)
            val = jnp.dot(xn, wv, preferred_element_type=F32)
            conv = (cw[0:1] * _shift_rows(gate, prev, 2)
                    + cw[1:2] * _shift_rows(gate, prev, 1)
                    + cw[2:3] * gate + cb)
            prev = gate[ts - SUBLANES:, :]
            act = (jax.nn.silu(conv) * val).astype(BF16)
            o_ref[rows, :] += jnp.dot(act, wd, preferred_element_type=F32)
        halo_scr[idx, :, pl.ds(0, width)] = prev

    assert n_full >= 2
    slot0 = (i * (n_full + 1)) % 2

    @pl.when(i == 0)
    def _():
        start(full(0, slot0))

    @pl.when((i % tiles_per_batch) == 0)
    def _():
        halo_scr[...] = mh_ref[...]

    h = h_ref[...]
    xn_scr[...] = _rms(h, g_ref[...]).astype(BF16)
    o_ref[...] = h

    def body(c, carry):
        slot = (slot0 + c) % 2
        wait(full(c, slot))

        @pl.when(c + 1 < n_full)
        def _():
            start(full(c + 1, 1 - slot))

        @pl.when(c + 1 == n_full)
        def _():
            start(ragged(1 - slot))

        compute(c, FFN_TF, slot)
        return carry

    lax.fori_loop(0, n_full, body, 0)

    slot_r = (slot0 + n_full) % 2
    wait(ragged(slot_r))

    @pl.when(i + 1 < pl.num_programs(0))
    def _():
        start(full(0, 1 - slot_r))

    compute(n_full, rem, slot_r)
    o_ref[...] = _rms(o_ref[...], fg_ref[...])


def _ffn(h1, meta_halo, wu, wd, w, tiles_per_batch, tm):
    rows, d = h1.shape
    d_ff = wd.shape[0]
    n_chunks = w["conv_w"].shape[0]
    in_specs = [
        pl.BlockSpec((tm, d), lambda i: (i, 0)),
        _const_spec((1, d)),
        pl.BlockSpec(memory_space=pl.ANY),
        pl.BlockSpec(memory_space=pl.ANY),
        _const_spec(w["conv_w"].shape),
        _const_spec(w["conv_b"].shape),
        _const_spec(meta_halo.shape),
        _const_spec((1, d)),
    ]
    slots = 2
    tiles = 2 * 2 * 4 * tm * d + 4 * (w["conv_w"].size + w["conv_b"].size + meta_halo.size)
    scratch = 2 * tm * d + 4 * meta_halo.size + slots * 3 * 2 * d * FFN_TF
    temps = 4 * (tm // FFN_SUB) * FFN_TF * 16
    return pl.pallas_call(
        functools.partial(_ffn_kernel, tiles_per_batch=tiles_per_batch, d_ff=d_ff),
        grid=(rows // tm,),
        in_specs=in_specs,
        out_specs=pl.BlockSpec((tm, d), lambda i: (i, 0)),
        out_shape=jax.ShapeDtypeStruct((rows, d), F32),
        scratch_shapes=[
            pltpu.VMEM((tm, d), BF16),
            pltpu.VMEM((n_chunks, SUBLANES, FFN_TF), F32),
            pltpu.VMEM((slots, d, FFN_TF), BF16),
            pltpu.VMEM((slots, d, FFN_TF), BF16),
            pltpu.VMEM((slots, FFN_TF, d), BF16),
            pltpu.SemaphoreType.DMA((slots, 3)),
        ],
        compiler_params=pltpu.CompilerParams(
            dimension_semantics=("arbitrary",),
            vmem_limit_bytes=_vmem_limit(tiles + scratch + temps)),
        name="ffn",
    )(h1, w["ffn_g"], wu, wd, w["conv_w"], w["conv_b"], meta_halo, w["final_g"])


def _swap_halves(w):
    half = w.shape[-1] // 2
    return jnp.concatenate([-w[..., half:], w[..., :half]], axis=-1)


def _block_diag(blocks, n):
    rows, c = blocks.shape[-2:]
    r = rows // n
    rep = (jnp.arange(c)[:, None] == jnp.arange(n * c)[None, :] % c).astype(blocks.dtype)
    tiled = jnp.einsum("...rc,cq->...rq", blocks, rep, precision=lax.Precision.HIGHEST)
    keep = (jnp.arange(rows)[:, None] // r) == (jnp.arange(n * c)[None, :] // c)
    return jnp.where(keep, tiled, 0.0)


def _prepare(p):
    d_ssm = p["d_skip"].shape[-1]
    q_lora = p["q_a_norm"].shape[-1]
    kv_lora = p["kv_a_norm"].shape[-1]
    d_ff = p["w_down"].shape[0]
    w = {"d_ssm": d_ssm, "scale": math.log2(math.e) / math.sqrt(QK_NOPE + QK_ROPE)}

    row = lambda v: v.reshape(1, -1).astype(F32)
    w["mix_g"] = row(p["mix_norm"])
    w["win"] = p["w_in"].astype(BF16)
    w["qg"] = row(p["q_a_norm"])
    w["kvg"] = row(p["kv_a_norm"])
    wq = p["w_q_b"].reshape(q_lora, HEADS, QK_NOPE + QK_ROPE)
    wq_pe = wq[..., QK_NOPE:]
    w["wq"] = jnp.concatenate([wq, _swap_halves(wq_pe)], axis=-1).reshape(q_lora, HEADS * HEAD_W).astype(BF16)
    wkv = p["w_kv_b"].reshape(kv_lora, HEADS, QK_NOPE + V_HEAD)
    w["wk"] = wkv[..., :QK_NOPE].reshape(kv_lora, HEADS * QK_NOPE).astype(BF16)
    wv = jnp.pad(jnp.moveaxis(wkv[..., QK_NOPE:], 0, 2), ((0, 0), (0, VT_ROWS - V_HEAD), (0, 0)))
    w["wv"] = wv.reshape(HEADS * VT_ROWS, kv_lora).astype(BF16)
    w["vone"] = jnp.tile((jnp.arange(VT_ROWS) == V_HEAD).astype(F32), HEADS).reshape(-1, 1)

    lam_re, lam_im = p["lam_re"].astype(F32), p["lam_im"].astype(F32)
    dt = jnp.exp(p["log_dt"].astype(F32))[:, None]
    mag = jnp.exp(lam_re * dt)
    a_re, a_im = mag * jnp.cos(lam_im * dt), mag * jnp.sin(lam_im * dt)
    den = lam_re * lam_re + lam_im * lam_im
    f_re = ((a_re - 1.0) * lam_re + a_im * lam_im) / den
    f_im = (a_im * lam_re - (a_re - 1.0) * lam_im) / den
    b_re, b_im = p["b_re"].astype(F32), p["b_im"].astype(F32)
    bb_re = f_re[..., None] * b_re - f_im[..., None] * b_im
    bb_im = f_re[..., None] * b_im + f_im[..., None] * b_re
    n_g = lam_re.shape[0]
    n_gb = n_g // GROUPS_PER_MXU
    bd = lambda t: _block_diag(t.reshape(n_gb, GROUPS_PER_MXU * t.shape[1], t.shape[2]), GROUPS_PER_MXU)
    w["bm"] = jnp.concatenate([bd(jnp.swapaxes(bb_re, 1, 2)),
                               bd(jnp.swapaxes(bb_im, 1, 2))], axis=2).astype(BF16)
    c_re, c_im = p["c_re"].astype(F32), p["c_im"].astype(F32)
    w["cm"] = jnp.concatenate([bd(jnp.swapaxes(c_re, 1, 2)),
                               bd(jnp.swapaxes(-c_im, 1, 2))], axis=1).astype(BF16)
    bcast = lambda a: jnp.broadcast_to(a.reshape(n_gb, 1, STATE_W), (n_gb, SUBLANES, STATE_W))
    w["are"], w["aim"] = bcast(a_re), bcast(a_im)
    w["dskip"] = row(p["d_skip"])
    w["wglu"] = p["w_glu"].astype(BF16)
    w["bglu"] = row(p["b_glu"])

    w["ga"], w["gb"] = row(p["out_norm_ssm"]), row(p["out_norm_attn"])
    w["wo_a"] = p["w_out"][:d_ssm].astype(BF16)
    w["wo_b"] = p["w_out"][d_ssm:].astype(BF16)

    n_chunks = -(-d_ff // FFN_TF)
    pad = n_chunks * FFN_TF - d_ff
    per_chunk = lambda a: jnp.swapaxes(
        jnp.pad(a.astype(F32), ((0, 0), (0, pad))).reshape(a.shape[0], n_chunks, FFN_TF), 0, 1)
    w["conv_w"] = per_chunk(p["conv_w"])
    w["conv_b"] = per_chunk(p["conv_b"].reshape(1, -1))
    w["ffn_g"] = row(p["ffn_norm"])
    w["final_g"] = row(p["final_norm"])
    return w


def _rope_table(n_pos):
    pos = jnp.arange(n_pos, dtype=F32)
    inv_freq = 1.0 / (ROPE_BASE ** (jnp.arange(0, QK_ROPE, 2, dtype=F32) / QK_ROPE))
    ang = pos[:, None] * inv_freq[None, :]
    cos, sin = jnp.cos(ang), jnp.sin(ang)
    return jnp.concatenate([cos, cos, sin, sin], axis=1)


def kernel(x, meta_tokens, mix_norm, w_in, lam_re, lam_im, log_dt, b_re, b_im, c_re, c_im, d_skip, w_glu, b_glu, q_a_norm, w_q_b, kv_a_norm, w_kv_b, out_norm_ssm, out_norm_attn, w_out, ffn_norm, w_up, conv_w, conv_b, w_down, final_norm):
    bsz, seq, d = x.shape
    assert meta_tokens.shape == (N_META, d) and bsz == SUBLANES
    p = dict(mix_norm=mix_norm[0], w_in=w_in[0], lam_re=lam_re[0], lam_im=lam_im[0], log_dt=log_dt[0],
             b_re=b_re[0], b_im=b_im[0], c_re=c_re[0], c_im=c_im[0], d_skip=d_skip[0],
             w_glu=w_glu[0], b_glu=b_glu[0], q_a_norm=q_a_norm[0], w_q_b=w_q_b[0],
             kv_a_norm=kv_a_norm[0], w_kv_b=w_kv_b[0], out_norm_ssm=out_norm_ssm[0],
             out_norm_attn=out_norm_attn[0], w_out=w_out[0], ffn_norm=ffn_norm[0], w_up=w_up[0],
             conv_w=conv_w[0], conv_b=conv_b[0], w_down=w_down[0], final_norm=final_norm)
    w = _prepare(p)
    d_ssm = w["d_ssm"]
    cs = _rope_table(N_META + seq)
    x2d = x.reshape(bsz * seq, d)
    meta = meta_tokens.astype(x.dtype)

    u_m, q_m, k_m, v_m = _inproj(meta, cs[:N_META], w, 1, 1, N_META)
    n_state = 2 * STATE_W * w["bm"].shape[0]
    ya_m8, h_meta_state = _s5(jnp.broadcast_to(u_m[None], (SUBLANES, N_META, d_ssm)),
                              jnp.zeros((SUBLANES, n_state), F32), w, N_META)
    ya_m = ya_m8[0]
    yb_m = _attention_meta(q_m, k_m, v_m[0])
    h1_m = _outproj(ya_m, yb_m, meta, w, 1, 1, N_META)

    tm = 512
    nt = seq // tm
    u, q, k, v = _inproj(x2d, cs[N_META:], w, bsz, nt, tm)
    ya, _, wu = _s5(u.reshape(bsz, seq, d_ssm), h_meta_state, w, 64, cast_src=w_up[0])
    yb, wd = _attention(q, k, v, k_m, v_m[0], w_down[0], bsz, seq)
    h1 = _outproj(ya.reshape(bsz * seq, d_ssm), yb, x2d, w, bsz, nt, tm)
    meta_halo = _meta_gate(h1_m, wu, w)
    out = _ffn(h1, meta_halo, wu, wd, w, nt, tm)
    return out.reshape(bsz, seq, d)
```

```python
import functools
import math

import jax
import jax.numpy as jnp
from jax import lax
from jax.experimental import pallas as pl
from jax.experimental.pallas import tpu as pltpu

F32 = jnp.float32
BF16 = jnp.bfloat16

EPS = 1e-6
CHUNK = 64
N_META = 16
SSM_GROUP = 16
SSM_STATE = 64
HEADS = 8
QK_NOPE = 128
QK_ROPE = 64
V_HEAD = 128
ROPE_BASE = 10000.0

LANES = 128
SUBLANES = 8
MXU_DIM = 256
VMEM_BYTES = 64 * 1024 * 1024

HEAD_W = QK_NOPE + 2 * QK_ROPE
GROUPS_PER_MXU = MXU_DIM // SSM_GROUP
STATE_W = GROUPS_PER_MXU * SSM_STATE

ATT_TQ = 256
ATT_HEADS = 2
VT_ROWS = V_HEAD + SUBLANES
FFN_TF = 1024
IN_SUB = 2
FFN_SUB = 2
S5_SUB = 2
S5_SCAN_W = 512


_NT = (((1,), (1,)), ((), ()))


def _vmem_limit(nbytes):
    return int(min(nbytes + (4 << 20), VMEM_BYTES - (4 << 20)))


def _rms(x, g):
    x = x.astype(F32)
    return x * lax.rsqrt(jnp.mean(x * x, axis=-1, keepdims=True) + EPS) * g


def _const_spec(shape):
    nd = len(shape)
    return pl.BlockSpec(shape, lambda *_: (0,) * nd, pipeline_mode=pl.Buffered(1))


def _inproj_kernel(x_ref, g_ref, win_ref, qg_ref, wq_ref, kvg_ref, wk_ref, wv_ref, vone_ref,
                   cs_ref, u_ref, q_ref, k_ref, v_ref, *, scale):
    d_ssm = u_ref.shape[1]
    q_lora = wq_ref.shape[0]
    kv_lora = wk_ref.shape[0]
    o1, o2, o3 = d_ssm, d_ssm + q_lora, d_ssm + q_lora + kv_lora
    tm = x_ref.shape[0]
    n_sub = IN_SUB if tm % (IN_SUB * LANES) == 0 else 1
    ts = tm // n_sub
    zs = []
    for s in range(n_sub):
        xn = _rms(x_ref[s * ts:(s + 1) * ts, :], g_ref[...]).astype(BF16)
        zs.append(jnp.dot(xn, win_ref[...], preferred_element_type=F32))
    for s, z in enumerate(zs):
        rows = slice(s * ts, (s + 1) * ts)
        u_ref[rows, :] = z[:, :o1].astype(u_ref.dtype)

        cs = cs_ref[rows, :]
        qn = _rms(z[:, o1:o2], qg_ref[...]).astype(BF16)
        q = jnp.dot(qn, wq_ref[...], preferred_element_type=F32)
        cs_q = cs * scale
        for h in range(HEADS):
            c0 = h * HEAD_W
            q_ref[rows, c0:c0 + QK_NOPE] = (q[:, c0:c0 + QK_NOPE] * scale).astype(q_ref.dtype)
            q_ref[rows, c0 + QK_NOPE:c0 + HEAD_W] = (q[:, c0 + QK_NOPE:c0 + HEAD_W] * cs_q).astype(q_ref.dtype)

        kvn = _rms(z[:, o2:o3], kvg_ref[...]).astype(BF16)
        kn = jnp.dot(kvn, wk_ref[...], preferred_element_type=F32)
        vt = lax.dot_general(wv_ref[...], kvn, _NT, preferred_element_type=F32)
        v_ref[:, rows] = (vt + vone_ref[...]).astype(v_ref.dtype)
        kpe = z[:, o3:o3 + QK_ROPE]
        x1, x2 = kpe[:, :QK_ROPE // 2], kpe[:, QK_ROPE // 2:]
        t = jnp.concatenate([kpe, -x2, x1], axis=1) * cs
        krot = (t + pltpu.roll(t, QK_ROPE, 1)).astype(k_ref.dtype)
        for h in range(HEADS):
            c0 = h * HEAD_W
            k_ref[rows, c0:c0 + QK_NOPE] = kn[:, h * QK_NOPE:(h + 1) * QK_NOPE].astype(k_ref.dtype)
            k_ref[rows, c0 + QK_NOPE:c0 + HEAD_W] = krot


def _inproj(x2d, cs, w, nb, nt, tm):
    rows, d = x2d.shape
    d_ssm = w["d_ssm"]
    n_in = w["win"].shape[1]
    row_spec = lambda width: pl.BlockSpec((tm, width), lambda b, j: (b * nt + j, 0))
    in_specs = [
        row_spec(d),
        _const_spec((1, d)),
        _const_spec(w["win"].shape),
        _const_spec(w["qg"].shape),
        _const_spec(w["wq"].shape),
        _const_spec(w["kvg"].shape),
        _const_spec(w["wk"].shape),
        _const_spec(w["wv"].shape),
        _const_spec(w["vone"].shape),
        pl.BlockSpec((tm, LANES), lambda b, j: (j, 0)),
    ]
    out_shape = (
        jax.ShapeDtypeStruct((rows, d_ssm), BF16),
        jax.ShapeDtypeStruct((rows, HEADS * HEAD_W), BF16),
        jax.ShapeDtypeStruct((rows, HEADS * HEAD_W), BF16),
        jax.ShapeDtypeStruct((nb, HEADS * VT_ROWS, nt * tm), BF16),
    )
    out_specs = (
        row_spec(d_ssm),
        row_spec(HEADS * HEAD_W),
        row_spec(HEADS * HEAD_W),
        pl.BlockSpec((None, HEADS * VT_ROWS, tm), lambda b, j: (b, 0, j)),
    )
    weights = 2 * (w["win"].size + w["wq"].size + w["wk"].size + w["wv"].size)
    tiles = 2 * tm * (4 * d + 2 * d_ssm + 4 * HEADS * HEAD_W + 2 * HEADS * V_HEAD + 4 * LANES)
    temps = 4 * tm * (d + n_in)
    return pl.pallas_call(
        functools.partial(_inproj_kernel, scale=w["scale"]),
        grid=(nb, nt),
        in_specs=in_specs,
        out_specs=out_specs,
        out_shape=out_shape,
        compiler_params=pltpu.CompilerParams(
            dimension_semantics=("arbitrary", "arbitrary"),
            vmem_limit_bytes=_vmem_limit(weights + tiles + temps)),
        name="inproj",
    )(x2d, w["mix_g"], w["win"], w["qg"], w["wq"], w["kvg"], w["wk"], w["wv"], w["vone"], cs)


def _s5_kernel(u_ref, h0_ref, bm_ref, cm_ref, are_ref, aim_ref, dskip_ref, wglu_ref, bglu_ref,
               y_ref, hT_ref, hbuf, state, *, steps):
    n_gb = bm_ref.shape[0]
    sw = are_ref.shape[2]
    cin = bm_ref.shape[1]
    nb, _, d_ssm = u_ref.shape

    @pl.when(pl.program_id(0) == 0)
    def _():
        state[...] = h0_ref[...]

    u = jnp.swapaxes(u_ref[...].astype(F32), 0, 1).reshape(nb * steps, d_ssm).astype(BF16)

    sub = steps // S5_SUB
    span = lambda k: slice(k * sub * SUBLANES, (k + 1) * sub * SUBLANES)
    for k in range(S5_SUB):
        for gb in range(n_gb):
            hbuf[span(k), 2 * sw * gb:2 * sw * (gb + 1)] = jnp.dot(
                u[span(k), cin * gb:cin * (gb + 1)], bm_ref[gb], preferred_element_type=F32)

    blocks = [(2 * sw * gb + o, 2 * sw * gb + sw + o, gb, o)
              for gb in range(n_gb) for o in range(0, sw, S5_SCAN_W)]
    carry = [(state[:, c_re:c_re + S5_SCAN_W], state[:, c_im:c_im + S5_SCAN_W])
             for c_re, c_im, _, _ in blocks]
    for k in range(S5_SUB):
        for j, (c_re, c_im, gb, o) in enumerate(blocks):
            are = are_ref[gb, :, o:o + S5_SCAN_W]
            aim = aim_ref[gb, :, o:o + S5_SCAN_W]
            hre, him = carry[j]
            for t in range(k * sub, (k + 1) * sub):
                r0 = t * SUBLANES
                nre = are * hre - aim * him + hbuf[r0:r0 + SUBLANES, c_re:c_re + S5_SCAN_W]
                nim = are * him + aim * hre + hbuf[r0:r0 + SUBLANES, c_im:c_im + S5_SCAN_W]
                hbuf[r0:r0 + SUBLANES, c_re:c_re + S5_SCAN_W] = nre
                hbuf[r0:r0 + SUBLANES, c_im:c_im + S5_SCAN_W] = nim
                hre, him = nre, nim
            carry[j] = (hre, him)
    for (c_re, c_im, _, _), (hre, him) in zip(blocks, carry):
        state[:, c_re:c_re + S5_SCAN_W] = hre
        state[:, c_im:c_im + S5_SCAN_W] = him

    outs = []
    for k in range(S5_SUB):
        ys = [jnp.dot(hbuf[span(k), 2 * sw * gb:2 * sw * (gb + 1)].astype(BF16), cm_ref[gb],
                      preferred_element_type=F32) for gb in range(n_gb)]
        y = jnp.concatenate(ys, axis=1) + dskip_ref[...] * u[span(k)].astype(F32)
        g = jax.nn.gelu(y)
        gate = jnp.dot(g.astype(BF16), wglu_ref[...], preferred_element_type=F32) + bglu_ref[...]
        outs.append(g * jax.nn.sigmoid(gate))
    out = jnp.concatenate(outs, axis=0).reshape(steps, nb, d_ssm)
    y_ref[...] = jnp.swapaxes(out, 0, 1).astype(y_ref.dtype)
    hT_ref[...] = state[...]


def _s5_cast_kernel(u_ref, h0_ref, bm_ref, cm_ref, are_ref, aim_ref, dskip_ref, wglu_ref, bglu_ref,
                    src_ref, y_ref, hT_ref, dst_ref, hbuf, state, *, steps):
    _s5_kernel(u_ref, h0_ref, bm_ref, cm_ref, are_ref, aim_ref, dskip_ref, wglu_ref, bglu_ref,
               y_ref, hT_ref, hbuf, state, steps=steps)
    dst_ref[...] = src_ref[...].astype(dst_ref.dtype)


def _s5(u, h0, w, steps, cast_src=None):
    nb, t_len, d_ssm = u.shape
    assert nb == SUBLANES
    r = steps * SUBLANES
    n_state = h0.shape[1]
    n_steps = t_len // steps
    blk = pl.BlockSpec((nb, steps, d_ssm), lambda i: (0, i, 0))
    in_specs = [
        blk,
        _const_spec(h0.shape),
        _const_spec(w["bm"].shape),
        _const_spec(w["cm"].shape),
        _const_spec(w["are"].shape),
        _const_spec(w["aim"].shape),
        _const_spec((1, d_ssm)),
        _const_spec(w["wglu"].shape),
        _const_spec((1, d_ssm)),
    ]
    args = [u, h0, w["bm"], w["cm"], w["are"], w["aim"], w["dskip"], w["wglu"], w["bglu"]]
    out_shape = [jax.ShapeDtypeStruct(u.shape, BF16), jax.ShapeDtypeStruct(h0.shape, F32)]
    out_specs = [blk, pl.BlockSpec(h0.shape, lambda i: (0, 0))]
    weights = 2 * (w["bm"].size + w["cm"].size + w["wglu"].size) + 4 * (w["are"].size + w["aim"].size)
    tiles = 2 * r * d_ssm * 2 * 2 + 3 * 4 * h0.size
    scratch = 4 * r * n_state + 4 * h0.size
    temps = 2 * 4 * r * d_ssm
    body = _s5_kernel
    if cast_src is not None:
        src_rows, src_cols = cast_src.shape
        rows_blk = src_rows // n_steps
        assert rows_blk * n_steps == src_rows and rows_blk % (2 * SUBLANES) == 0
        cast_spec = pl.BlockSpec((rows_blk, src_cols), lambda i: (i, 0))
        in_specs.append(cast_spec)
        args.append(cast_src)
        out_specs.append(cast_spec)
        out_shape.append(jax.ShapeDtypeStruct(cast_src.shape, BF16))
        tiles += 2 * rows_blk * src_cols * (4 + 2)
        body = _s5_cast_kernel
    return pl.pallas_call(
        functools.partial(body, steps=steps),
        grid=(n_steps,),
        in_specs=in_specs,
        out_specs=tuple(out_specs),
        out_shape=tuple(out_shape),
        scratch_shapes=[pltpu.VMEM((r, n_state), F32), pltpu.VMEM(h0.shape, F32)],
        compiler_params=pltpu.CompilerParams(
            dimension_semantics=("arbitrary",),
            vmem_limit_bytes=_vmem_limit(weights + tiles + scratch + temps)),
        name="s5",
    )(*args)


def _attn_kernel(q_ref, k_ref, vt_ref, km_ref, vmt_ref, src_ref, o_ref, dst_ref):
    @pl.when(pl.program_id(1) == 0)
    def _():
        dst_ref[...] = src_ref[...].astype(dst_ref.dtype)

    seq = q_ref.shape[0]
    half = ATT_TQ // 2
    neg = jnp.finfo(F32).min
    kc = lax.broadcasted_iota(jnp.int32, (half, half), 0) // CHUNK
    qc = lax.broadcasted_iota(jnp.int32, (half, half), 1) // CHUNK
    visible = kc <= qc
    colmax = lambda t: jnp.max(t, axis=0, keepdims=True)
    prob = lambda s, m: jnp.exp2(s - m).astype(BF16)
    order = [(hh, i) for hh in range(ATT_HEADS) for i in range(seq // ATT_TQ - 1, -1, -1)]
    scores = {}
    for hh, i in order:
        r0 = i * ATT_TQ
        qs = slice(hh * HEAD_W, (hh + 1) * HEAD_W)
        q = q_ref[r0:r0 + ATT_TQ, qs]
        s_full = (lax.dot_general(k_ref[0:r0, qs], q, _NT, preferred_element_type=F32)
                  if i > 0 else None)
        s_diag = lax.dot_general(k_ref[r0:r0 + ATT_TQ, qs], q, _NT, preferred_element_type=F32)
        s_meta = lax.dot_general(km_ref[:, qs], q, _NT, preferred_element_type=F32)
        scores[hh, i] = (s_meta, s_diag, s_full)
    for hh, i in order:
        r0 = i * ATT_TQ
        vs = slice(hh * VT_ROWS, (hh + 1) * VT_ROWS)
        vmt = vmt_ref[vs, :]
        s_meta, s_diag, s_full = scores[hh, i]
        s00 = jnp.where(visible, s_diag[:half, :half], neg)
        s01 = s_diag[:half, half:]
        s11 = jnp.where(visible, s_diag[half:, half:], neg)
        m = jnp.concatenate([colmax(s00), jnp.maximum(colmax(s01), colmax(s11))], axis=1)
        m = jnp.maximum(m, colmax(s_meta))
        if i > 0:
            m = jnp.maximum(m, colmax(s_full))
        p_meta = prob(s_meta, m)
        p00 = prob(s00, m[:, :half])
        p01 = prob(s01, m[:, half:])
        p11 = prob(s11, m[:, half:])
        p_diag = jnp.concatenate([jnp.concatenate([p00, p01], axis=1),
                                  jnp.concatenate([jnp.zeros_like(p11), p11], axis=1)], axis=0)
        o = jnp.dot(vmt, p_meta, preferred_element_type=F32)
        o = o + jnp.dot(vt_ref[vs, r0:r0 + ATT_TQ], p_diag, preferred_element_type=F32)
        if i > 0:
            o = o + jnp.dot(vt_ref[vs, 0:r0], prob(s_full, m), preferred_element_type=F32)
        out = o[:V_HEAD] / o[V_HEAD:V_HEAD + 1]
        o_ref[r0:r0 + ATT_TQ, hh * V_HEAD:(hh + 1) * V_HEAD] = out.T.astype(o_ref.dtype)


def _attention(q, k, vt, km, vmt, cast_src, nb, seq):
    assert ATT_TQ % CHUNK == 0 and seq % ATT_TQ == 0 and V_HEAD == LANES and HEADS % ATT_HEADS == 0
    n_meta = km.shape[0]
    hw, vr = ATT_HEADS * HEAD_W, ATT_HEADS * VT_ROWS
    src_rows, src_cols = cast_src.shape
    blk = src_rows // nb
    assert blk * nb == src_rows and blk % (2 * SUBLANES) == 0
    cast_spec = pl.BlockSpec((blk, src_cols), lambda b, h: (b, 0))
    in_specs = [
        pl.BlockSpec((seq, hw), lambda b, h: (b, h)),
        pl.BlockSpec((seq, hw), lambda b, h: (b, h)),
        pl.BlockSpec((None, vr, seq), lambda b, h: (b, h, 0)),
        pl.BlockSpec((n_meta, hw), lambda b, h: (0, h)),
        pl.BlockSpec((vr, n_meta), lambda b, h: (h, 0)),
        cast_spec,
    ]
    tiles = 2 * 2 * seq * (2 * hw + vr + ATT_HEADS * V_HEAD) + 2 * blk * src_cols * (4 + 2)
    temps = 4 * 4 * seq * ATT_TQ * ATT_HEADS
    return pl.pallas_call(
        _attn_kernel,
        grid=(nb, HEADS // ATT_HEADS),
        in_specs=in_specs,
        out_specs=(pl.BlockSpec((seq, ATT_HEADS * V_HEAD), lambda b, h: (b, h)), cast_spec),
        out_shape=(jax.ShapeDtypeStruct((nb * seq, HEADS * V_HEAD), BF16),
                   jax.ShapeDtypeStruct(cast_src.shape, BF16)),
        compiler_params=pltpu.CompilerParams(
            dimension_semantics=("arbitrary", "arbitrary"),
            vmem_limit_bytes=_vmem_limit(tiles + temps)),
        name="attn",
    )(q, k, vt, km, vmt, cast_src)


def _attn_meta_kernel(q_ref, k_ref, vt_ref, o_ref):
    for h in range(HEADS):
        s = lax.dot_general(q_ref[:, h * HEAD_W:(h + 1) * HEAD_W],
                            k_ref[:, h * HEAD_W:(h + 1) * HEAD_W], _NT, preferred_element_type=F32)
        p = jnp.exp2(s - jnp.max(s, axis=1, keepdims=True))
        pv = lax.dot_general(p.astype(BF16), vt_ref[h * VT_ROWS:h * VT_ROWS + V_HEAD, :], _NT,
                             preferred_element_type=F32)
        o_ref[:, h * V_HEAD:(h + 1) * V_HEAD] = (pv / jnp.sum(p, axis=1, keepdims=True)).astype(o_ref.dtype)


def _attention_meta(q, k, vt):
    return pl.pallas_call(
        _attn_meta_kernel,
        out_shape=jax.ShapeDtypeStruct((q.shape[0], HEADS * V_HEAD), BF16),
        name="attn_meta",
    )(q, k, vt)


def _outproj_kernel(ya_ref, yb_ref, x_ref, ga_ref, gb_ref, wa_ref, wb_ref, o_ref):
    tm = x_ref.shape[0]
    n_sub = IN_SUB if tm % (IN_SUB * LANES) == 0 else 1
    ts = tm // n_sub
    for s in range(n_sub):
        rows = slice(s * ts, (s + 1) * ts)
        a = _rms(ya_ref[rows, :], ga_ref[...]).astype(BF16)
        b = _rms(yb_ref[rows, :], gb_ref[...]).astype(BF16)
        mix = jnp.dot(a, wa_ref[...], preferred_element_type=F32)
        mix = mix + jnp.dot(b, wb_ref[...], preferred_element_type=F32)
        o_ref[rows, :] = x_ref[rows, :].astype(F32) + mix


def _outproj(ya, yb, x2d, w, nb, nt, tm):
    rows, d = x2d.shape
    d_ssm, d_attn = w["wo_a"].shape[0], w["wo_b"].shape[0]
    row_spec = lambda width: pl.BlockSpec((tm, width), lambda b, j: (b * nt + j, 0))
    in_specs = [
        row_spec(d_ssm),
        row_spec(d_attn),
        row_spec(d),
        _const_spec((1, d_ssm)),
        _const_spec((1, d_attn)),
        _const_spec(w["wo_a"].shape),
        _const_spec(w["wo_b"].shape),
    ]
    weights = 2 * (w["wo_a"].size + w["wo_b"].size)
    tiles = 2 * tm * (2 * d_ssm + 2 * d_attn + 4 * d + 4 * d)
    temps = 4 * tm * d
    return pl.pallas_call(
        _outproj_kernel,
        grid=(nb, nt),
        in_specs=in_specs,
        out_specs=row_spec(d),
        out_shape=jax.ShapeDtypeStruct((rows, d), F32),
        compiler_params=pltpu.CompilerParams(
            dimension_semantics=("arbitrary", "arbitrary"),
            vmem_limit_bytes=_vmem_limit(weights + tiles + temps)),
        name="outproj",
    )(ya, yb, x2d, w["ga"], w["gb"], w["wo_a"], w["wo_b"])


def _gate_kernel(h_ref, g_ref, wg_ref, o_ref):
    xn = _rms(h_ref[...], g_ref[...]).astype(BF16)
    o_ref[...] = jnp.dot(xn, wg_ref[...], preferred_element_type=F32)


def _meta_gate(h_meta, wu, w):
    rows, d = h_meta.shape
    n_chunks = w["conv_w"].shape[0]
    gate = pl.pallas_call(
        _gate_kernel,
        grid=(n_chunks,),
        in_specs=[_const_spec((rows, d)), _const_spec((1, d)),
                  pl.BlockSpec((d, FFN_TF), lambda j: (0, j))],
        out_specs=pl.BlockSpec((rows, FFN_TF), lambda j: (0, j)),
        out_shape=jax.ShapeDtypeStruct((rows, n_chunks * FFN_TF), F32),
        compiler_params=pltpu.CompilerParams(dimension_semantics=("arbitrary",)),
        name="meta_gate",
    )(h_meta, w["ffn_g"], wu)
    halo = gate[rows - SUBLANES:].reshape(SUBLANES, n_chunks, FFN_TF)
    return jnp.swapaxes(halo, 0, 1)


def _shift_rows(x, prev, k):
    rolled = pltpu.roll(x, k, 0)
    head = jnp.concatenate([prev, x[:SUBLANES]], axis=0)[SUBLANES - k:2 * SUBLANES - k]
    return jnp.concatenate([head, rolled[SUBLANES:]], axis=0)


def _ffn_kernel(h_ref, g_ref, wu_hbm, wd_hbm, cw_ref, cb_ref, mh_ref, fg_ref, o_ref,
                xn_scr, halo_scr, wg_buf, wv_buf, wd_buf, sem, *, tiles_per_batch, d_ff):
    i = pl.program_id(0)
    tm = h_ref.shape[0]
    ts = tm // FFN_SUB
    n_full = d_ff // FFN_TF
    rem = d_ff - n_full * FFN_TF
    assert 0 < rem and rem % LANES == 0

    def chunk_copies(col0, width, slot):
        return (
            pltpu.make_async_copy(wu_hbm.at[:, pl.ds(col0, width)],
                                  wg_buf.at[slot, :, pl.ds(0, width)], sem.at[slot, 0]),
            pltpu.make_async_copy(wu_hbm.at[:, pl.ds(d_ff + col0, width)],
                                  wv_buf.at[slot, :, pl.ds(0, width)], sem.at[slot, 1]),
            pltpu.make_async_copy(wd_hbm.at[pl.ds(col0, width), :],
                                  wd_buf.at[slot, pl.ds(0, width), :], sem.at[slot, 2]),
        )

    def start(copies):
        for c in copies:
            c.start()

    def wait(copies):
        for c in copies:
            c.wait()

    ragged = lambda slot: chunk_copies(n_full * FFN_TF, rem, slot)
    full = lambda c, slot: chunk_copies(pl.multiple_of(c * FFN_TF, FFN_TF), FFN_TF, slot)

    def compute(idx, width, slot):
        wg = wg_buf[slot, :, :width]
        wv = wv_buf[slot, :, :width]
        wd = wd_buf[slot, :width, :]
        cw = cw_ref[idx][:, :width]
        cb = cb_ref[idx][:, :width]
        prev = halo_scr[idx][:, :width]
        for s in range(FFN_SUB):
            rows = pl.ds(s * ts, ts)
            xn = xn_scr[rows, :]
            gate = jnp.dot(xn, wg, preferred_element_type=F32)
            val = jnp.dot(xn, wv, preferred_element_type=F32)
            conv = (cw[0:1] * _shift_rows(gate, prev, 2)
                    + cw[1:2] * _shift_rows(gate, prev, 1)
                    + cw[2:3] * gate + cb)
            prev = gate[ts - SUBLANES:, :]
            act = (jax.nn.silu(conv) * val).astype(BF16)
            o_ref[rows, :] += jnp.dot(act, wd, preferred_element_type=F32)
        halo_scr[idx, :, pl.ds(0, width)] = prev

    assert n_full >= 2
    slot0 = (i * (n_full + 1)) % 2

    @pl.when(i == 0)
    def _():
        start(full(0, slot0))

    @pl.when((i % tiles_per_batch) == 0)
    def _():
        halo_scr[...] = mh_ref[...]

    h = h_ref[...]
    xn_scr[...] = _rms(h, g_ref[...]).astype(BF16)
    o_ref[...] = h

    def body(c, carry):
        slot = (slot0 + c) % 2
        wait(full(c, slot))

        @pl.when(c + 1 < n_full)
        def _():
            start(full(c + 1, 1 - slot))

        @pl.when(c + 1 == n_full)
        def _():
            start(ragged(1 - slot))

        compute(c, FFN_TF, slot)
        return carry

    lax.fori_loop(0, n_full, body, 0)

    slot_r = (slot0 + n_full) % 2
    wait(ragged(slot_r))

    @pl.when(i + 1 < pl.num_programs(0))
    def _():
        start(full(0, 1 - slot_r))

    compute(n_full, rem, slot_r)
    o_ref[...] = _rms(o_ref[...], fg_ref[...])


def _ffn(h1, meta_halo, wu, wd, w, tiles_per_batch, tm):
    rows, d = h1.shape
    d_ff = wd.shape[0]
    n_chunks = w["conv_w"].shape[0]
    in_specs = [
        pl.BlockSpec((tm, d), lambda i: (i, 0)),
        _const_spec((1, d)),
        pl.BlockSpec(memory_space=pl.ANY),
        pl.BlockSpec(memory_space=pl.ANY),
        _const_spec(w["conv_w"].shape),
        _const_spec(w["conv_b"].shape),
        _const_spec(meta_halo.shape),
        _const_spec((1, d)),
    ]
    slots = 2
    tiles = 2 * 2 * 4 * tm * d + 4 * (w["conv_w"].size + w["conv_b"].size + meta_halo.size)
    scratch = 2 * tm * d + 4 * meta_halo.size + slots * 3 * 2 * d * FFN_TF
    temps = 4 * (tm // FFN_SUB) * FFN_TF * 16
    return pl.pallas_call(
        functools.partial(_ffn_kernel, tiles_per_batch=tiles_per_batch, d_ff=d_ff),
        grid=(rows // tm,),
        in_specs=in_specs,
        out_specs=pl.BlockSpec((tm, d), lambda i: (i, 0)),
        out_shape=jax.ShapeDtypeStruct((rows, d), F32),
        scratch_shapes=[
            pltpu.VMEM((tm, d), BF16),
            pltpu.VMEM((n_chunks, SUBLANES, FFN_TF), F32),
            pltpu.VMEM((slots, d, FFN_TF), BF16),
            pltpu.VMEM((slots, d, FFN_TF), BF16),
            pltpu.VMEM((slots, FFN_TF, d), BF16),
            pltpu.SemaphoreType.DMA((slots, 3)),
        ],
        compiler_params=pltpu.CompilerParams(
            dimension_semantics=("arbitrary",),
            vmem_limit_bytes=_vmem_limit(tiles + scratch + temps)),
        name="ffn",
    )(h1, w["ffn_g"], wu, wd, w["conv_w"], w["conv_b"], meta_halo, w["final_g"])


def _swap_halves(w):
    half = w.shape[-1] // 2
    return jnp.concatenate([-w[..., half:], w[..., :half]], axis=-1)


def _block_diag(blocks, n):
    rows, c = blocks.shape[-2:]
    r = rows // n
    rep = (jnp.arange(c)[:, None] == jnp.arange(n * c)[None, :] % c).astype(blocks.dtype)
    tiled = jnp.einsum("...rc,cq->...rq", blocks, rep, precision=lax.Precision.HIGHEST)
    keep = (jnp.arange(rows)[:, None] // r) == (jnp.arange(n * c)[None, :] // c)
    return jnp.where(keep, tiled, 0.0)


def _prepare(p):
    d_ssm = p["d_skip"].shape[-1]
    q_lora = p["q_a_norm"].shape[-1]
    kv_lora = p["kv_a_norm"].shape[-1]
    d_ff = p["w_down"].shape[0]
    w = {"d_ssm": d_ssm, "scale": math.log2(math.e) / math.sqrt(QK_NOPE + QK_ROPE)}

    row = lambda v: v.reshape(1, -1).astype(F32)
    w["mix_g"] = row(p["mix_norm"])
    w["win"] = p["w_in"].astype(BF16)
    w["qg"] = row(p["q_a_norm"])
    w["kvg"] = row(p["kv_a_norm"])
    wq = p["w_q_b"].reshape(q_lora, HEADS, QK_NOPE + QK_ROPE)
    wq_pe = wq[..., QK_NOPE:]
    w["wq"] = jnp.concatenate([wq, _swap_halves(wq_pe)], axis=-1).reshape(q_lora, HEADS * HEAD_W).astype(BF16)
    wkv = p["w_kv_b"].reshape(kv_lora, HEADS, QK_NOPE + V_HEAD)
    w["wk"] = wkv[..., :QK_NOPE].reshape(kv_lora, HEADS * QK_NOPE).astype(BF16)
    wv = jnp.pad(jnp.moveaxis(wkv[..., QK_NOPE:], 0, 2), ((0, 0), (0, VT_ROWS - V_HEAD), (0, 0)))
    w["wv"] = wv.reshape(HEADS * VT_ROWS, kv_lora).astype(BF16)
    w["vone"] = jnp.tile((jnp.arange(VT_ROWS) == V_HEAD).astype(F32), HEADS).reshape(-1, 1)

    lam_re, lam_im = p["lam_re"].astype(F32), p["lam_im"].astype(F32)
    dt = jnp.exp(p["log_dt"].astype(F32))[:, None]
    mag = jnp.exp(lam_re * dt)
    a_re, a_im = mag * jnp.cos(lam_im * dt), mag * jnp.sin(lam_im * dt)
    den = lam_re * lam_re + lam_im * lam_im
    f_re = ((a_re - 1.0) * lam_re + a_im * lam_im) / den
    f_im = (a_im * lam_re - (a_re - 1.0) * lam_im) / den
    b_re, b_im = p["b_re"].astype(F32), p["b_im"].astype(F32)
    bb_re = f_re[..., None] * b_re - f_im[..., None] * b_im
    bb_im = f_re[..., None] * b_im + f_im[..., None] * b_re
    n_g = lam_re.shape[0]
    n_gb = n_g // GROUPS_PER_MXU
    bd = lambda t: _block_diag(t.reshape(n_gb, GROUPS_PER_MXU * t.shape[1], t.shape[2]), GROUPS_PER_MXU)
    w["bm"] = jnp.concatenate([bd(jnp.swapaxes(bb_re, 1, 2)),
                               bd(jnp.swapaxes(bb_im, 1, 2))], axis=2).astype(BF16)
    c_re, c_im = p["c_re"].astype(F32), p["c_im"].astype(F32)
    w["cm"] = jnp.concatenate([bd(jnp.swapaxes(c_re, 1, 2)),
                               bd(jnp.swapaxes(-c_im, 1, 2))], axis=1).astype(BF16)
    bcast = lambda a: jnp.broadcast_to(a.reshape(n_gb, 1, STATE_W), (n_gb, SUBLANES, STATE_W))
    w["are"], w["aim"] = bcast(a_re), bcast(a_im)
    w["dskip"] = row(p["d_skip"])
    w["wglu"] = p["w_glu"].astype(BF16)
    w["bglu"] = row(p["b_glu"])

    w["ga"], w["gb"] = row(p["out_norm_ssm"]), row(p["out_norm_attn"])
    w["wo_a"] = p["w_out"][:d_ssm].astype(BF16)
    w["wo_b"] = p["w_out"][d_ssm:].astype(BF16)

    n_chunks = -(-d_ff // FFN_TF)
    pad = n_chunks * FFN_TF - d_ff
    per_chunk = lambda a: jnp.swapaxes(
        jnp.pad(a.astype(F32), ((0, 0), (0, pad))).reshape(a.shape[0], n_chunks, FFN_TF), 0, 1)
    w["conv_w"] = per_chunk(p["conv_w"])
    w["conv_b"] = per_chunk(p["conv_b"].reshape(1, -1))
    w["ffn_g"] = row(p["ffn_norm"])
    w["final_g"] = row(p["final_norm"])
    return w


def _rope_table(n_pos):
    pos = jnp.arange(n_pos, dtype=F32)
    inv_freq = 1.0 / (ROPE_BASE ** (jnp.arange(0, QK_ROPE, 2, dtype=F32) / QK_ROPE))
    ang = pos[:, None] * inv_freq[None, :]
    cos, sin = jnp.cos(ang), jnp.sin(ang)
    return jnp.concatenate([cos, cos, sin, sin], axis=1)


def kernel(x, meta_tokens, mix_norm, w_in, lam_re, lam_im, log_dt, b_re, b_im, c_re, c_im, d_skip, w_glu, b_glu, q_a_norm, w_q_b, kv_a_norm, w_kv_b, out_norm_ssm, out_norm_attn, w_out, ffn_norm, w_up, conv_w, conv_b, w_down, final_norm):
    bsz, seq, d = x.shape
    assert meta_tokens.shape == (N_META, d) and bsz == SUBLANES
    p = dict(mix_norm=mix_norm[0], w_in=w_in[0], lam_re=lam_re[0], lam_im=lam_im[0], log_dt=log_dt[0],
             b_re=b_re[0], b_im=b_im[0], c_re=c_re[0], c_im=c_im[0], d_skip=d_skip[0],
             w_glu=w_glu[0], b_glu=b_glu[0], q_a_norm=q_a_norm[0], w_q_b=w_q_b[0],
             kv_a_norm=kv_a_norm[0], w_kv_b=w_kv_b[0], out_norm_ssm=out_norm_ssm[0],
             out_norm_attn=out_norm_attn[0], w_out=w_out[0], ffn_norm=ffn_norm[0], w_up=w_up[0],
             conv_w=conv_w[0], conv_b=conv_b[0], w_down=w_down[0], final_norm=final_norm)
    w = _prepare(p)
    d_ssm = w["d_ssm"]
    cs = _rope_table(N_META + seq)
    x2d = x.reshape(bsz * seq, d)
    meta = meta_tokens.astype(x.dtype)

    u_m, q_m, k_m, v_m = _inproj(meta, cs[:N_META], w, 1, 1, N_META)
    n_state = 2 * STATE_W * w["bm"].shape[0]
    ya_m8, h_meta_state = _s5(jnp.broadcast_to(u_m[None], (SUBLANES, N_META, d_ssm)),
                              jnp.zeros((SUBLANES, n_state), F32), w, N_META)
    ya_m = ya_m8[0]
    yb_m = _attention_meta(q_m, k_m, v_m[0])
    h1_m = _outproj(ya_m, yb_m, meta, w, 1, 1, N_META)

    tm = 512
    nt = seq // tm
    u, q, k, v = _inproj(x2d, cs[N_META:], w, bsz, nt, tm)
    ya, _, wu = _s5(u.reshape(bsz, seq, d_ssm), h_meta_state, w, 64, cast_src=w_up[0])
    yb, wd = _attention(q, k, v, k_m, v_m[0], w_down[0], bsz, seq)
    h1 = _outproj(ya.reshape(bsz * seq, d_ssm), yb, x2d, w, bsz, nt, tm)
    meta_halo = _meta_gate(h1_m, wu, w)
    out = _ffn(h1, meta_halo, wu, wd, w, nt, tm)
    return out.reshape(bsz, seq, d)
```
